```python
import math
import jax, jax.numpy as jnp
from jax import lax
import numpy as np

D_MODEL = 1024
BATCH = 8
SEQ = 2048
DEPTH = 4
DEC_BATCH = 128
DEC_SEQ = 1
PAST_LEN = 16384
PAGE_SIZE = 128

N_MIXERS = 3
N_POOL_LAYERS = (DEPTH + 2) // 3
N_GDN_LAYERS = (DEPTH + 1) // 3
N_RET_LAYERS = DEPTH // 3

POOL_WINDOWS = (2, 4, 8, 16)
N_POOL_GROUPS = len(POOL_WINDOWS)
POOL_GROUP = D_MODEL // N_POOL_GROUPS
POOL_BUF = max(POOL_WINDOWS) - 1

GDN_HEADS = 8
GDN_DK = D_MODEL // GDN_HEADS
GDN_DV = D_MODEL // GDN_HEADS
GDN_CONV = 4
GDN_CONV_CH = GDN_HEADS * (2 * GDN_DK + GDN_DV)
GDN_IN = GDN_CONV_CH + GDN_HEADS * GDN_DV + 2 * GDN_HEADS

RET_HEADS = 8
RET_DK = D_MODEL // RET_HEADS
RET_DV = 2 * RET_DK
RET_IN = 2 * RET_HEADS * RET_DK + 2 * RET_HEADS * RET_DV
ROPE_BASE = 10000.0

D_FF = -(-8 * D_MODEL // (3 * 256)) * 256
CHUNK = 64
DN_ALPHA = (2 * DEPTH) ** 0.25
DN_BETA = (8 * DEPTH) ** -0.25
LN_EPS = 1e-5
RMS_EPS = 1e-6

kernel_name = 'hybrid_pool_gdn_retention_decode_step'

F32 = jnp.float32


def chunk_len(T):
    return CHUNK if T % CHUNK == 0 else T


def layer_norm(x, g, b):
    x32 = x.astype(F32)
    mu = jnp.mean(x32, -1, keepdims=True)
    var = jnp.mean(jnp.square(x32 - mu), -1, keepdims=True)
    return ((x32 - mu) * lax.rsqrt(var + LN_EPS) * g.astype(F32) + b.astype(F32)).astype(x.dtype)


def swiglu(x, w13, w2):
    a, b = jnp.split(x @ w13, 2, axis=-1)
    return (jax.nn.silu(a) * b) @ w2


def l2norm(x):
    return x * lax.rsqrt(jnp.sum(x * x, -1, keepdims=True) + RMS_EPS)


def rope(x, pos):
    half = x.shape[-1] // 2
    freqs = ROPE_BASE ** (-jnp.arange(half, dtype=F32) / half)
    ang = pos[:, None] * freqs[None, :]
    cos = jnp.cos(ang)[None, :, None, :]
    sin = jnp.sin(ang)[None, :, None, :]
    x1, x2 = x[..., :half], x[..., half:]
    return jnp.concatenate([x1 * cos - x2 * sin, x1 * sin + x2 * cos], axis=-1)


def pool_mixer(x, buf, n_valid, w, scale):
    T = x.shape[1]
    P = POOL_BUF
    ext = jnp.concatenate([buf.astype(x.dtype), x], axis=1)
    c = jnp.pad(jnp.cumsum(ext.astype(F32), axis=1), ((0, 0), (1, 0), (0, 0)))
    t = jnp.arange(T)
    means = []
    for gi, win in enumerate(POOL_WINDOWS):
        lo, hi = gi * POOL_GROUP, (gi + 1) * POOL_GROUP
        s = c[:, P + 1:P + 1 + T, lo:hi] - c[:, P + 1 - win:P + 1 - win + T, lo:hi]
        cnt = jnp.minimum(t + 1 + n_valid, win).astype(F32)
        means.append(s / cnt[None, :, None])
    pooled = jnp.concatenate(means, axis=-1) - x.astype(F32)
    B = x.shape[0]
    pg = pooled.reshape(B, T, N_POOL_GROUPS, POOL_GROUP)
    out = jnp.einsum('btgc,gcd->btgd', pg, w.astype(F32)).reshape(B, T, D_MODEL)
    return (out * scale.astype(F32)).astype(x.dtype), ext[:, -P:]


def causal_conv(u, buf, w):
    T = u.shape[1]
    ext = jnp.concatenate([buf.astype(u.dtype), u], axis=1)
    y = ext[:, 0:T] * w[0]
    for j in range(1, GDN_CONV):
        y = y + ext[:, j:j + T] * w[j]
    return jax.nn.silu(y), ext[:, -(GDN_CONV - 1):]


def gated_delta_rule(q, k, v, g, beta, S0):
    B, T, H, dk = q.shape
    dv = v.shape[-1]
    C = chunk_len(T)
    N = T // C

    def blocks(a):
        a = jnp.moveaxis(a, 2, 1)
        return a.reshape((B, H, N, C) + a.shape[3:])

    q, k, v, g, beta = blocks(q), blocks(k), blocks(v), blocks(g), blocks(beta)
    gc = jnp.cumsum(g, axis=-1)
    lower = jnp.tril(jnp.ones((C, C), bool))
    strict = jnp.tril(jnp.ones((C, C), bool), -1)
    diff = gc[..., :, None] - gc[..., None, :]
    decay = jnp.where(lower, jnp.exp(jnp.where(lower, diff, 0.0)), 0.0)
    kb = k * beta[..., None]
    Lm = jnp.where(strict, jnp.einsum('bhnid,bhnjd->bhnij', kb, k) * decay, 0.0)
    eye = jnp.eye(C, dtype=F32)
    Tinv = lax.linalg.triangular_solve(eye + Lm, jnp.broadcast_to(eye, Lm.shape),
                                       left_side=True, lower=True, unit_diagonal=True)
    u = jnp.einsum('bhnij,bhnje->bhnie', Tinv, v * beta[..., None])
    wk = jnp.einsum('bhnij,bhnjd->bhnid', Tinv, kb * jnp.exp(gc)[..., None])
    attn = jnp.einsum('bhnid,bhnjd->bhnij', q, k) * decay
    qg = q * jnp.exp(gc)[..., None]
    kg = k * jnp.exp(gc[..., -1:] - gc)[..., None]
    glast = jnp.exp(gc[..., -1])
    xs = tuple(jnp.moveaxis(a, 2, 0) for a in (u, wk, attn, qg, kg, glast))

    def step(S, xn):
        u_n, w_n, a_n, qg_n, kg_n, gl_n = xn
        v_new = u_n - jnp.einsum('bhcd,bhde->bhce', w_n, S)
        o_n = jnp.einsum('bhcd,bhde->bhce', qg_n, S) + jnp.einsum('bhij,bhje->bhie', a_n, v_new)
        S = S * gl_n[..., None, None] + jnp.einsum('bhcd,bhce->bhde', kg_n, v_new)
        return S, o_n

    S, o = lax.scan(step, S0, xs)
    o = jnp.moveaxis(o, 0, 2).reshape(B, H, T, dv)
    return jnp.moveaxis(o, 1, 2), S


def gdn_mixer(x, conv_buf, S0, w_in, conv_w, a_log, dt_bias, norm_g, w_out):
    B, T, _ = x.shape
    H, dk, dv = GDN_HEADS, GDN_DK, GDN_DV
    proj = x @ w_in
    qkv, new_buf = causal_conv(proj[..., :GDN_CONV_CH], conv_buf, conv_w)
    o1 = GDN_CONV_CH + H * dv
    z = proj[..., GDN_CONV_CH:o1].astype(F32)
    b_logit = proj[..., o1:o1 + H].astype(F32)
    a_in = proj[..., o1 + H:].astype(F32)
    qkv = qkv.astype(F32)
    q = l2norm(qkv[..., :H * dk].reshape(B, T, H, dk)) * dk ** -0.5
    k = l2norm(qkv[..., H * dk:2 * H * dk].reshape(B, T, H, dk))
    v = qkv[..., 2 * H * dk:].reshape(B, T, H, dv)
    beta = jax.nn.sigmoid(b_logit)
    g = -jnp.exp(a_log.astype(F32)) * jax.nn.softplus(a_in + dt_bias.astype(F32))
    o, S = gated_delta_rule(q, k, v, g, beta, S0.astype(F32))
    o = o * lax.rsqrt(jnp.mean(o * o, -1, keepdims=True) + RMS_EPS) * norm_g.astype(F32)
    o = (o.reshape(B, T, H * dv) * jax.nn.silu(z)).astype(x.dtype)
    return o @ w_out, new_buf, S.astype(S0.dtype)


def retention(q, k, v, S0):
    B, T, H, dk = q.shape
    dv = v.shape[-1]
    C = chunk_len(T)
    N = T // C
    lg = jnp.log(1.0 - 2.0 ** (-5.0 - jnp.arange(H, dtype=F32)))
    idx = jnp.arange(C, dtype=F32)
    diff = idx[:, None] - idx[None, :]
    decay = jnp.where(diff >= 0, jnp.exp(jnp.maximum(diff, 0.0)[None] * lg[:, None, None]), 0.0)
    xi = jnp.exp((idx + 1.0)[None, :] * lg[:, None])
    zeta = jnp.exp((C - 1.0 - idx)[None, :] * lg[:, None])
    gC = jnp.exp(C * lg)

    def blocks(a):
        a = jnp.moveaxis(a, 2, 1)
        return a.reshape((B, H, N, C) + a.shape[3:])

    q, k, v = blocks(q), blocks(k), blocks(v)
    attn = jnp.einsum('bhnid,bhnjd->bhnij', q, k) * decay[:, None]
    inner = jnp.einsum('bhnij,bhnje->bhnie', attn, v)
    qx = q * xi[:, None, :, None]
    kz = k * zeta[:, None, :, None]
    xs = tuple(jnp.moveaxis(a, 2, 0) for a in (inner, qx, kz, v))

    def step(S, xn):
        in_n, qx_n, kz_n, v_n = xn
        o_n = in_n + jnp.einsum('bhcd,bhde->bhce', qx_n, S)
        S = gC[:, None, None] * S + jnp.einsum('bhcd,bhce->bhde', kz_n, v_n)
        return S, o_n

    S, o = lax.scan(step, S0, xs)
    o = jnp.moveaxis(o, 0, 2).reshape(B, H, T, dv)
    return jnp.moveaxis(o, 1, 2), S


def retention_mixer(x, S0, pos0, w_in, w_out):
    B, T, _ = x.shape
    H, dk, dv = RET_HEADS, RET_DK, RET_DV
    proj = (x @ w_in).astype(F32)
    q = proj[..., :H * dk].reshape(B, T, H, dk)
    k = proj[..., H * dk:2 * H * dk].reshape(B, T, H, dk)
    v = proj[..., 2 * H * dk:2 * H * dk + H * dv].reshape(B, T, H, dv)
    gate = proj[..., 2 * H * dk + H * dv:]
    pos = pos0 + jnp.arange(T, dtype=F32)
    q = rope(q, pos)
    k = rope(k, pos) * dk ** -0.5
    o, S = retention(q, k, v, S0.astype(F32))
    mu = jnp.mean(o, -1, keepdims=True)
    var = jnp.mean(jnp.square(o - mu), -1, keepdims=True)
    o = (o - mu) * lax.rsqrt(var + LN_EPS)
    o = (jax.nn.silu(gate) * o.reshape(B, T, H * dv)).astype(x.dtype)
    return o @ w_out, S.astype(S0.dtype)


def trunk(x, pool_buf, pool_valid, conv_buf, gdn_S, ret_S, pos0,
          pool_w, pool_scale, gdn_w_in, gdn_conv_w, gdn_a_log, gdn_dt_bias, gdn_norm_g, gdn_w_out,
          ret_w_in, ret_w_out, ffn_w13, ffn_w2, ln_g, ln_b):
    new_pool, new_conv, new_gdn, new_ret = [], [], [], []
    for i in range(DEPTH):
        kind, j = i % N_MIXERS, i // N_MIXERS
        if kind == 0:
            mix, nb = pool_mixer(x, pool_buf[j], pool_valid, pool_w[j], pool_scale[j])
            new_pool.append(nb)
        elif kind == 1:
            mix, nc, ns = gdn_mixer(x, conv_buf[j], gdn_S[j], gdn_w_in[j], gdn_conv_w[j], gdn_a_log[j],
                                    gdn_dt_bias[j], gdn_norm_g[j], gdn_w_out[j])
            new_conv.append(nc)
            new_gdn.append(ns)
        else:
            mix, ns = retention_mixer(x, ret_S[j], pos0, ret_w_in[j], ret_w_out[j])
            new_ret.append(ns)
        x = layer_norm(DN_ALPHA * x + mix, ln_g[i, 0], ln_b[i, 0])
        x = layer_norm(DN_ALPHA * x + swiglu(x, ffn_w13[i], ffn_w2[i]), ln_g[i, 1], ln_b[i, 1])
    return x, jnp.stack(new_pool), jnp.stack(new_conv), jnp.stack(new_gdn), jnp.stack(new_ret)


def setup_inputs(seed: int = 0) -> dict:
    key = jax.random.key(seed)
    ks = jax.random.split(key, 20)

    def nrm(k, shape, s):
        return jax.random.normal(k, shape, F32) * s

    x_prompt = nrm(ks[0], (BATCH, SEQ, D_MODEL), 1.0)
    x_sample = nrm(ks[1], (DEC_BATCH, DEC_SEQ, D_MODEL), 1.0)
    state_pool = nrm(ks[2], (N_POOL_LAYERS, DEC_BATCH, POOL_BUF, D_MODEL), 1.0)
    state_gdn_conv = nrm(ks[3], (N_GDN_LAYERS, DEC_BATCH, GDN_CONV - 1, GDN_CONV_CH), 1.0)
    state_gdn = nrm(ks[4], (N_GDN_LAYERS, DEC_BATCH, GDN_HEADS, GDN_DK, GDN_DV), GDN_DK ** -0.5)
    state_ret = nrm(ks[5], (N_RET_LAYERS, DEC_BATCH, RET_HEADS, RET_DK, RET_DV), 0.1)
    pool_w = nrm(ks[6], (N_POOL_LAYERS, N_POOL_GROUPS, POOL_GROUP, POOL_GROUP), POOL_GROUP ** -0.5 * DN_BETA)
    pool_scale = 1.0 + nrm(ks[7], (N_POOL_LAYERS, D_MODEL), 0.05)
    gdn_w_in = nrm(ks[8], (N_GDN_LAYERS, D_MODEL, GDN_IN), D_MODEL ** -0.5)
    gdn_conv_w = nrm(ks[9], (N_GDN_LAYERS, GDN_CONV, GDN_CONV_CH), GDN_CONV ** -0.5)
    gdn_a_log = jnp.log(jax.random.uniform(ks[10], (N_GDN_LAYERS, GDN_HEADS), F32, 1.0, 16.0))
    dt = jnp.exp(jax.random.uniform(ks[11], (N_GDN_LAYERS, GDN_HEADS), F32, math.log(1e-3), math.log(1e-1)))
    gdn_dt_bias = dt + jnp.log(-jnp.expm1(-dt))
    gdn_norm_g = 1.0 + nrm(ks[12], (N_GDN_LAYERS, GDN_DV), 0.05)
    gdn_w_out = nrm(ks[13], (N_GDN_LAYERS, GDN_HEADS * GDN_DV, D_MODEL), (GDN_HEADS * GDN_DV) ** -0.5 * DN_BETA)
    ret_w_in = nrm(ks[14], (N_RET_LAYERS, D_MODEL, RET_IN), D_MODEL ** -0.5)
    ret_w_out = nrm(ks[15], (N_RET_LAYERS, RET_HEADS * RET_DV, D_MODEL), (RET_HEADS * RET_DV) ** -0.5 * DN_BETA)
    ffn_w13 = nrm(ks[16], (DEPTH, D_MODEL, 2 * D_FF), D_MODEL ** -0.5)
    ffn_w2 = nrm(ks[17], (DEPTH, D_FF, D_MODEL), D_FF ** -0.5 * DN_BETA)
    ln_g = 1.0 + nrm(ks[18], (DEPTH, 2, D_MODEL), 0.05)
    ln_b = nrm(ks[19], (DEPTH, 2, D_MODEL), 0.02)
    return {'x_prompt': x_prompt, 'x_sample': x_sample,
            'state_pool': state_pool, 'state_gdn_conv': state_gdn_conv,
            'state_gdn': state_gdn, 'state_ret': state_ret,
            'pool_w': pool_w, 'pool_scale': pool_scale,
            'gdn_w_in': gdn_w_in, 'gdn_conv_w': gdn_conv_w, 'gdn_a_log': gdn_a_log,
            'gdn_dt_bias': gdn_dt_bias, 'gdn_norm_g': gdn_norm_g, 'gdn_w_out': gdn_w_out,
            'ret_w_in': ret_w_in, 'ret_w_out': ret_w_out,
            'ffn_w13': ffn_w13, 'ffn_w2': ffn_w2, 'ln_g': ln_g, 'ln_b': ln_b}


def reference(x_prompt, x_sample, state_pool, state_gdn_conv, state_gdn, state_ret,
              pool_w, pool_scale, gdn_w_in, gdn_conv_w, gdn_a_log, gdn_dt_bias, gdn_norm_g, gdn_w_out,
              ret_w_in, ret_w_out, ffn_w13, ffn_w2, ln_g, ln_b):
    pool0 = jnp.zeros((N_POOL_LAYERS, BATCH, POOL_BUF, D_MODEL), x_prompt.dtype)
    conv0 = jnp.zeros((N_GDN_LAYERS, BATCH, GDN_CONV - 1, GDN_CONV_CH), x_prompt.dtype)
    gdn0 = jnp.zeros((N_GDN_LAYERS, BATCH, GDN_HEADS, GDN_DK, GDN_DV), state_gdn.dtype)
    ret0 = jnp.zeros((N_RET_LAYERS, BATCH, RET_HEADS, RET_DK, RET_DV), state_ret.dtype)
    y_prompt, pool_p, conv_p, gdn_p, ret_p = trunk(
        x_prompt, pool0, 0, conv0, gdn0, ret0, 0,
        pool_w, pool_scale, gdn_w_in, gdn_conv_w, gdn_a_log, gdn_dt_bias, gdn_norm_g, gdn_w_out,
        ret_w_in, ret_w_out, ffn_w13, ffn_w2, ln_g, ln_b)
    y_sample, pool_s, conv_s, gdn_s, ret_s = trunk(
        x_sample, state_pool, min(PAST_LEN, POOL_BUF), state_gdn_conv, state_gdn, state_ret, PAST_LEN,
        pool_w, pool_scale, gdn_w_in, gdn_conv_w, gdn_a_log, gdn_dt_bias, gdn_norm_g, gdn_w_out,
        ret_w_in, ret_w_out, ffn_w13, ffn_w2, ln_g, ln_b)
    return (y_prompt, y_sample, pool_p, pool_s, conv_p, conv_s, gdn_p, gdn_s, ret_p, ret_s)
```

```python
import functools
import math

import jax
import jax.numpy as jnp
from jax import lax
from jax.experimental import pallas as pl
from jax.experimental.pallas import tpu as pltpu

F32 = jnp.float32
BF16 = jnp.bfloat16

D_MODEL = 1024
DEPTH = 4
PAST_LEN = 16384
N_MIXERS = 3

POOL_WINDOWS = (2, 4, 8, 16)
POOL_GROUP = D_MODEL // len(POOL_WINDOWS)
POOL_BUF = max(POOL_WINDOWS) - 1
POOL_HALO = 16

HEADS = 8
DK = D_MODEL // HEADS
GDN_DV = DK
GDN_CONV = 4
GDN_CONV_CH = HEADS * (2 * DK + GDN_DV)
CONV_HALO = 8
GDN_CHUNK = 64
RET_DV = 2 * DK
RET_CHUNK = 256
ROPE_BASE = 10000.0

D_FF = -(-8 * D_MODEL // (3 * 256)) * 256
FF_CHUNK = 256
DN_ALPHA = (2 * DEPTH) ** 0.25
LN_EPS = 1e-5
RMS_EPS = 1e-6

SUBLANES = 8
LANES = 128
VMEM_LIMIT = 56 * 1024 * 1024

HIGHEST = lax.Precision.HIGHEST


def _params(n_grid):
    return pltpu.CompilerParams(dimension_semantics=("arbitrary",) * n_grid, vmem_limit_bytes=VMEM_LIMIT)


def _resident(shape):
    nd = len(shape)
    return pl.BlockSpec(shape, lambda *_: (0,) * nd, pipeline_mode=pl.Buffered(1))


def _whole(shape):
    nd = len(shape)
    return pl.BlockSpec(shape, lambda *_: (0,) * nd)


def _dot(a, b):
    return jnp.dot(a, b, preferred_element_type=F32)


def _bdot(spec, a, b):
    return jnp.einsum(spec, a, b, preferred_element_type=F32)


def _silu(x):
    return x * jax.nn.sigmoid(x)


def _layer_norm(y, g, b):
    mu = jnp.mean(y, -1, keepdims=True)
    yc = y - mu
    var = jnp.mean(yc * yc, -1, keepdims=True)
    return yc * lax.rsqrt(var + LN_EPS) * g + b


def _ffn_kernel(x_ref, w13_ref, w2_ref, g_ref, b_ref, o_ref):
    x = x_ref[...]
    xb = x.astype(BF16)
    acc = None
    for c in range(D_FF // FF_CHUNK):
        lo = c * FF_CHUNK
        a = _dot(xb, w13_ref[:, lo:lo + FF_CHUNK])
        b = _dot(xb, w13_ref[:, D_FF + lo:D_FF + lo + FF_CHUNK])
        h = (_silu(a) * b).astype(BF16)
        p = _dot(h, w2_ref[lo:lo + FF_CHUNK, :])
        acc = p if acc is None else acc + p
    o_ref[...] = _layer_norm(DN_ALPHA * x + acc, g_ref[...], b_ref[...])


def _ffn(x2d, w13, w2, g, b, tm):
    n = x2d.shape[0]
    assert n % tm == 0
    row = pl.BlockSpec((tm, D_MODEL), lambda i: (i, 0))
    return pl.pallas_call(
        _ffn_kernel,
        grid=(n // tm,),
        in_specs=[row, _resident(w13.shape), _resident(w2.shape), _resident(g.shape), _resident(b.shape)],
        out_specs=row,
        out_shape=jax.ShapeDtypeStruct((n, D_MODEL), F32),
        compiler_params=_params(1),
        name="ffn",
    )(x2d, w13, w2, g, b)


def _pool_mix(x, shifted, cnt_of, pw_ref, sc_ref):
    outs = []
    for gi, win in enumerate(POOL_WINDOWS):
        lo, hi = gi * POOL_GROUP, (gi + 1) * POOL_GROUP
        xs = x[:, lo:hi]
        s = xs
        for d in range(1, win):
            s = s + shifted(d, lo, hi)
        pooled = s / cnt_of(win) - xs
        outs.append(_dot(pooled.astype(BF16), pw_ref[gi]))
    return jnp.concatenate(outs, axis=-1) * sc_ref[...]


def _pool_prompt_kernel(x_ref, pw_ref, sc_ref, g_ref, b_ref, o_ref, ext_ref, *, tm, n_valid):
    ti = pl.program_id(1)

    @pl.when(ti == 0)
    def _():
        ext_ref[0:POOL_HALO, :] = jnp.zeros((POOL_HALO, D_MODEL), F32)

    x = x_ref[...]
    ext_ref[POOL_HALO:POOL_HALO + tm, :] = x
    t = ti * tm + lax.broadcasted_iota(jnp.int32, (tm, 1), 0)
    mix = _pool_mix(
        x,
        lambda d, lo, hi: ext_ref[pl.ds(POOL_HALO - d, tm), lo:hi],
        lambda win: jnp.minimum(t + 1 + n_valid, win).astype(F32),
        pw_ref, sc_ref)
    o_ref[...] = _layer_norm(DN_ALPHA * x + mix, g_ref[...], b_ref[...])
    ext_ref[0:POOL_HALO, :] = ext_ref[tm:tm + POOL_HALO, :]


def _pool_prompt(x, pw, sc, g, b, tm):
    bsz, t, _ = x.shape
    assert t % tm == 0
    blk = pl.BlockSpec((None, tm, D_MODEL), lambda bi, ti: (bi, ti, 0))
    return pl.pallas_call(
        functools.partial(_pool_prompt_kernel, tm=tm, n_valid=0),
        grid=(bsz, t // tm),
        in_specs=[blk, _resident(pw.shape), _resident(sc.shape), _resident(g.shape), _resident(b.shape)],
        out_specs=blk,
        out_shape=jax.ShapeDtypeStruct(x.shape, F32),
        scratch_shapes=[pltpu.VMEM((tm + POOL_HALO, D_MODEL), F32)],
        compiler_params=_params(2),
        name="pool_prompt",
    )(x, pw, sc, g, b)


def _pool_sample_kernel(x_ref, buf_ref, pw_ref, sc_ref, g_ref, b_ref, o_ref, *, n_valid):
    x = x_ref[...]
    mix = _pool_mix(
        x,
        lambda d, lo, hi: buf_ref[POOL_BUF - d, :, lo:hi],
        lambda win: float(min(1 + n_valid, win)),
        pw_ref, sc_ref)
    o_ref[...] = _layer_norm(DN_ALPHA * x + mix, g_ref[...], b_ref[...])


def _pool_sample(x2d, buf_t, pw, sc, g, b):
    args = (x2d, buf_t, pw, sc, g, b)
    return pl.pallas_call(
        functools.partial(_pool_sample_kernel, n_valid=min(PAST_LEN, POOL_BUF)),
        grid=(1,),
        in_specs=[_resident(a.shape) for a in args],
        out_specs=_whole(x2d.shape),
        out_shape=jax.ShapeDtypeStruct(x2d.shape, F32),
        compiler_params=_params(1),
        name="pool_sample",
    )(*args)


def _softplus(x):
    return jnp.maximum(x, 0.0) + jnp.log1p(jnp.exp(-jnp.abs(x)))


def _gdn_gates(ba, alog_ref, dtb_ref):
    return jax.nn.sigmoid(ba), -jnp.exp(alog_ref[...]) * _softplus(ba + dtb_ref[...])


def _l2norm(v):
    return v * lax.rsqrt(jnp.sum(v * v, -1, keepdims=True) + RMS_EPS)


def _gdn_out(o_heads, z, x, ng_ref, wo_ref, g_ref, b_ref):
    gated = []
    for h, oh in enumerate(o_heads):
        on = oh * lax.rsqrt(jnp.mean(oh * oh, -1, keepdims=True) + RMS_EPS) * ng_ref[...]
        gated.append((on * _silu(z[:, h * GDN_DV:(h + 1) * GDN_DV])).astype(BF16))
    mix = _dot(jnp.concatenate(gated, axis=-1), wo_ref[...])
    return _layer_norm(DN_ALPHA * x + mix, g_ref[...], b_ref[...])


def _unit_lower_inverse(a):
    c = a.shape[-1]
    ii = lax.broadcasted_iota(jnp.int32, (c, c), 0)
    jj = lax.broadcasted_iota(jnp.int32, (c, c), 1)
    eye = jnp.where(ii == jj, 1.0, 0.0).astype(F32)
    x = eye - jnp.where((ii // 2 == jj // 2) & (ii > jj), a, 0.0)
    s = 4
    while s <= c:
        off = jnp.where((ii // s == jj // s) & (ii // (s // 2) != jj // (s // 2)) & (ii > jj), a, 0.0)
        xb = x.astype(BF16)
        y = _bdot('hij,hjk->hik', off.astype(BF16), xb)
        x = x - _bdot('hij,hjk->hik', xb, y.astype(BF16))
        s *= 2
    return x


def _gdn_prompt_kernel(x_ref, wqkv_ref, wz_ref, wba_ref, cw_ref, alog_ref, dtb_ref, ng_ref, wo_ref, g_ref, b_ref,
                       o_ref, conv_ref, sout_ref,
                       ext_ref, s_ref, q_s, k_s, kb_s, vb_s, kbg_s, qg_s, kg_s, dec_s, gl_s, o_s, *, tm):
    c = GDN_CHUNK
    nc = tm // c
    ti = pl.program_id(1)

    @pl.when(ti == 0)
    def _():
        ext_ref[0:CONV_HALO, :] = jnp.zeros((CONV_HALO, GDN_CONV_CH), F32)
        s_ref[...] = jnp.zeros(s_ref.shape, F32)

    x = x_ref[...]
    xb = x.astype(BF16)
    ext_ref[CONV_HALO:CONV_HALO + tm, :] = _dot(xb, wqkv_ref[...])
    conv_ref[...] = ext_ref[tm + CONV_HALO - (GDN_CONV - 1):tm + CONV_HALO, :]

    def conv_silu(col):
        lo, hi = col * LANES, (col + 1) * LANES
        y = ext_ref[CONV_HALO:CONV_HALO + tm, lo:hi] * cw_ref[GDN_CONV - 1:GDN_CONV, lo:hi]
        for j in range(GDN_CONV - 1):
            y = y + ext_ref[pl.ds(CONV_HALO - (GDN_CONV - 1) + j, tm), lo:hi] * cw_ref[j:j + 1, lo:hi]
        return _silu(y)

    beta_all, gl = _gdn_gates(_dot(xb, wba_ref[...]), alog_ref, dtb_ref)
    ri = lax.broadcasted_iota(jnp.int32, (tm, tm), 0)
    ci = lax.broadcasted_iota(jnp.int32, (tm, tm), 1)
    same = (ri // c) == (ci // c)
    gc = jnp.dot(jnp.where(same & (ci <= ri), 1.0, 0.0).astype(F32), gl, precision=HIGHEST, preferred_element_type=F32)
    gcl = jnp.dot(jnp.where(same, 1.0, 0.0).astype(F32), gl, precision=HIGHEST, preferred_element_type=F32)
    eg = jnp.exp(gc)
    kdec = jnp.exp(gcl - gc)
    glast = jnp.exp(gcl)
    gct = gc.T

    ii = lax.broadcasted_iota(jnp.int32, (c, c), 0)
    jj = lax.broadcasted_iota(jnp.int32, (c, c), 1)
    lower = ii >= jj
    for h in range(HEADS):
        qn = _l2norm(conv_silu(h)) * DK ** -0.5
        kn = _l2norm(conv_silu(HEADS + h))
        vh = conv_silu(2 * HEADS + h)
        bh = beta_all[:, h:h + 1]
        egh = eg[:, HEADS + h:HEADS + h + 1]
        kb = kn * bh
        q_s[h] = qn.astype(BF16)
        k_s[h] = kn.astype(BF16)
        kb_s[h] = kb.astype(BF16)
        vb_s[h] = (vh * bh).astype(BF16)
        kbg_s[h] = (kb * egh).astype(BF16)
        qg_s[h] = (qn * egh).astype(BF16)
        kg_s[h] = (kn * kdec[:, HEADS + h:HEADS + h + 1]).astype(BF16)
        for n in range(nc):
            col = gc[n * c:(n + 1) * c, HEADS + h:HEADS + h + 1]
            row = gct[HEADS + h:HEADS + h + 1, n * c:(n + 1) * c]
            dec_s[h, n] = jnp.where(lower, jnp.exp(jnp.where(lower, col - row, 0.0)), 0.0)
            gl_s[h, n] = jnp.broadcast_to(glast[n * c:n * c + 1, HEADS + h:HEADS + h + 1], (SUBLANES, LANES))

    ext_ref[0:CONV_HALO, :] = ext_ref[tm:tm + CONV_HALO, :]

    strict = ii > jj

    def chunk(n, carry):
        sl = pl.ds(pl.multiple_of(n * c, c), c)
        kn_c = k_s[:, sl, :]
        dec = dec_s[:, n]
        a = jnp.where(strict, _bdot('hid,hjd->hij', kb_s[:, sl, :], kn_c) * dec, 0.0)
        tinv = _unit_lower_inverse(a).astype(BF16)
        u = _bdot('hij,hjd->hid', tinv, vb_s[:, sl, :])
        wk = _bdot('hij,hjd->hid', tinv, kbg_s[:, sl, :])
        attn = _bdot('hid,hjd->hij', q_s[:, sl, :], kn_c) * dec
        s = s_ref[...]
        sb = s.astype(BF16)
        vnb = (u - _bdot('hcd,hde->hce', wk.astype(BF16), sb)).astype(BF16)
        o_s[:, sl, :] = _bdot('hcd,hde->hce', qg_s[:, sl, :], sb) + _bdot('hij,hje->hie', attn.astype(BF16), vnb)
        s_ref[...] = s * gl_s[:, n][:, 0:1, :] + _bdot('hcd,hce->hde', kg_s[:, sl, :], vnb)
        return carry

    lax.fori_loop(0, nc, chunk, 0)

    z = _dot(xb, wz_ref[...])
    o_ref[...] = _gdn_out([o_s[h] for h in range(HEADS)], z, x, ng_ref, wo_ref, g_ref, b_ref)

    @pl.when(ti == pl.num_programs(1) - 1)
    def _():
        sout_ref[...] = s_ref[...]


def _gdn_prompt(x, w, g, b, tm):
    bsz, t, _ = x.shape
    assert t % tm == 0 and tm % GDN_CHUNK == 0
    nc = tm // GDN_CHUNK
    blk = pl.BlockSpec((None, tm, D_MODEL), lambda bi, ti: (bi, ti, 0))
    consts = (w["wqkv"], w["wz"], w["wba"], w["conv_w"], w["alog"], w["dtb"], w["ng"], w["wo"], g, b)
    head_bf16 = pltpu.VMEM((HEADS, tm, DK), BF16)
    return pl.pallas_call(
        functools.partial(_gdn_prompt_kernel, tm=tm),
        grid=(bsz, t // tm),
        in_specs=[blk] + [_resident(a.shape) for a in consts],
        out_specs=[blk,
                   pl.BlockSpec((None, GDN_CONV - 1, GDN_CONV_CH), lambda bi, ti: (bi, 0, 0)),
                   pl.BlockSpec((None, HEADS, DK, GDN_DV), lambda bi, ti: (bi, 0, 0, 0))],
        out_shape=[jax.ShapeDtypeStruct(x.shape, F32),
                   jax.ShapeDtypeStruct((bsz, GDN_CONV - 1, GDN_CONV_CH), F32),
                   jax.ShapeDtypeStruct((bsz, HEADS, DK, GDN_DV), F32)],
        scratch_shapes=[pltpu.VMEM((tm + CONV_HALO, GDN_CONV_CH), F32),
                        pltpu.VMEM((HEADS, DK, GDN_DV), F32)]
        + [head_bf16] * 7
        + [pltpu.VMEM((HEADS, nc, GDN_CHUNK, GDN_CHUNK), F32),
           pltpu.VMEM((HEADS, nc, SUBLANES, LANES), F32),
           pltpu.VMEM((HEADS, tm, GDN_DV), F32)],
        compiler_params=_params(2),
        name="gdn_prompt",
    )(x, *consts)


def _gdn_sample_a_kernel(x_ref, cb_ref, wqkv_ref, wz_ref, wba_ref, cw_ref, alog_ref, dtb_ref,
                         rows_ref, pre_ref, z_ref):
    xb = x_ref[...].astype(BF16)
    pre = _dot(xb, wqkv_ref[...])
    pre_ref[...] = pre
    z_ref[...] = _dot(xb, wz_ref[...])
    y = pre * cw_ref[GDN_CONV - 1:GDN_CONV, :]
    for j in range(GDN_CONV - 1):
        y = y + cb_ref[j] * cw_ref[j:j + 1, :]
    qkv = _silu(y)
    beta_all, gl = _gdn_gates(_dot(xb, wba_ref[...]), alog_ref, dtb_ref)
    eg = jnp.exp(gl)
    rows = x_ref.shape[0]
    for h in range(HEADS):
        hs = slice(h * DK, (h + 1) * DK)
        qn = _l2norm(qkv[:, hs]) * DK ** -0.5
        kn = _l2norm(qkv[:, HEADS * DK + h * DK:HEADS * DK + (h + 1) * DK])
        vh = qkv[:, 2 * HEADS * DK + h * GDN_DV:2 * HEADS * DK + (h + 1) * GDN_DV]
        bh = beta_all[:, h:h + 1]
        egh = eg[:, HEADS + h:HEADS + h + 1]
        rows_ref[0, :, hs] = kn
        rows_ref[1, :, hs] = qn
        rows_ref[2, :, hs] = vh * bh
        rows_ref[3, :, hs] = jnp.broadcast_to(bh * egh, (rows, DK))
        rows_ref[4, :, hs] = jnp.broadcast_to(egh, (rows, DK))
        rows_ref[5, :, hs] = jnp.broadcast_to(jnp.sum(qn * kn, -1, keepdims=True), (rows, DK))
    rows_ref[6:8] = jnp.zeros((2, rows, D_MODEL), F32)


def _gdn_sample_b_kernel(p_ref, s_ref, o_ref, sout_ref):
    bb = p_ref.shape[0]
    n = bb * HEADS
    p = p_ref[...].reshape(n, SUBLANES, DK)
    s = s_ref[...].reshape(n, DK, GDN_DV)
    pb = p.astype(BF16)
    r = _bdot('nqd,nde->nqe', pb, s.astype(BF16))
    v_new = p[:, 2:3, :] - p[:, 3:4, :] * r[:, 0:1, :]
    o = p[:, 4:5, :] * r[:, 1:2, :] + p[:, 5:6, :] * v_new
    row0 = lax.broadcasted_iota(jnp.int32, (n, SUBLANES, GDN_DV), 1) == 0
    vn8 = jnp.where(row0, jnp.broadcast_to(v_new, (n, SUBLANES, GDN_DV)), 0.0)
    s_new = s * p[:, 4:5, :] + _bdot('nqd,nqe->nde', pb, vn8.astype(BF16))
    o_ref[...] = jnp.broadcast_to(o, (n, SUBLANES, GDN_DV)).reshape(bb, HEADS, SUBLANES, GDN_DV)
    sout_ref[...] = s_new.reshape(bb, HEADS, DK, GDN_DV)


def _gdn_sample_c_kernel(o_ref, z_ref, x_ref, ng_ref, wo_ref, g_ref, b_ref, out_ref):
    o = o_ref[...]
    out_ref[...] = _gdn_out([o[:, h * GDN_DV:(h + 1) * GDN_DV] for h in range(HEADS)],
                            z_ref[...], x_ref[...], ng_ref, wo_ref, g_ref, b_ref)


def _whole_call(kernel, args, out_shapes, name):
    outs = [jax.ShapeDtypeStruct(s, F32) for s in out_shapes]
    return pl.pallas_call(
        kernel,
        grid=(1,),
        in_specs=[_resident(a.shape) for a in args],
        out_specs=[_whole(s) for s in out_shapes],
        out_shape=outs,
        compiler_params=_params(1),
        name=name,
    )(*args)


def _state_step(kernel, p, extra, s0, bb, name):
    bsz = s0.shape[0]
    assert bsz % bb == 0

    def blk(a):
        return pl.BlockSpec((bb,) + a.shape[1:], lambda i: (i,) + (0,) * (a.ndim - 1))

    ins = (p,) + tuple(extra) + (s0,)
    o_shape = (bsz, HEADS, SUBLANES, s0.shape[-1])
    return pl.pallas_call(
        kernel,
        grid=(bsz // bb,),
        in_specs=[blk(a) for a in ins],
        out_specs=[pl.BlockSpec((bb,) + o_shape[1:], lambda i: (i, 0, 0, 0)), blk(s0)],
        out_shape=[jax.ShapeDtypeStruct(o_shape, F32), jax.ShapeDtypeStruct(s0.shape, F32)],
        compiler_params=_params(1),
        name=name,
    )(*ins)


def _head_rows(rows, width):
    r, bsz, _ = rows.shape
    return rows.reshape(r, bsz, HEADS, width).transpose(1, 2, 0, 3)


def _gdn_sample(x2d, conv_buf, s0, w, g, b):
    bsz = x2d.shape[0]
    cb_t = conv_buf.transpose(1, 0, 2)
    rows, pre, z = _whole_call(
        _gdn_sample_a_kernel,
        (x2d, cb_t, w["wqkv"], w["wz"], w["wba"], w["conv_w"], w["alog"], w["dtb"]),
        [(SUBLANES, bsz, D_MODEL), (bsz, GDN_CONV_CH), (bsz, D_MODEL)], "gdn_sample_a")
    o8, s_new = _state_step(_gdn_sample_b_kernel, _head_rows(rows, DK), (), s0, 8, "gdn_sample_b")
    o = o8[:, :, 0, :].reshape(bsz, HEADS * GDN_DV)
    (x1,) = _whole_call(_gdn_sample_c_kernel, (o, z, x2d, w["ng"], w["wo"], g, b), [(bsz, D_MODEL)], "gdn_sample_c")
    new_conv = jnp.concatenate([conv_buf[:, 1:], pre[:, None, :]], axis=1)
    return x1, new_conv, s_new


def _log_gamma(h):
    return math.log(1.0 - 2.0 ** (-5.0 - h))


def _rope_tables(pos, freq_ref):
    ang = pos * freq_ref[...]
    lane = lax.broadcasted_iota(jnp.int32, ang.shape, 1)
    return jnp.cos(ang), jnp.where(lane < DK // 2, -1.0, 1.0) * jnp.sin(ang)


def _rope(v, cos2, sin2):
    return v * cos2 + pltpu.roll(v, DK // 2, 1) * sin2


def _ret_out(o_heads, gate, x, wo_ref, g_ref, b_ref):
    gated = []
    for h, oh in enumerate(o_heads):
        mu = jnp.mean(oh, -1, keepdims=True)
        oc = oh - mu
        on = oc * lax.rsqrt(jnp.mean(oc * oc, -1, keepdims=True) + LN_EPS)
        gated.append((_silu(gate[:, h * RET_DV:(h + 1) * RET_DV]) * on).astype(BF16))
    mix = _dot(jnp.concatenate(gated, axis=-1), wo_ref[...])
    return _layer_norm(DN_ALPHA * x + mix, g_ref[...], b_ref[...])


def _ret_prompt_kernel(x_ref, win_ref, freq_ref, wo_ref, g_ref, b_ref, o_ref, sout_ref,
                       s_ref, q_s, k_s, qx_s, kz_s, v_s, dec_s, o_s, *, tm, c, pos0):
    nc = tm // c
    ti = pl.program_id(1)
    hk = HEADS * DK

    @pl.when(ti == 0)
    def _():
        s_ref[...] = jnp.zeros(s_ref.shape, F32)
        diff = (lax.broadcasted_iota(jnp.int32, (c, c), 0) - lax.broadcasted_iota(jnp.int32, (c, c), 1)).astype(F32)
        for h in range(HEADS):
            dec_s[h] = jnp.where(diff >= 0, jnp.exp(jnp.maximum(diff, 0.0) * _log_gamma(h)), 0.0)

    x = x_ref[...]
    xb = x.astype(BF16)
    row = lax.broadcasted_iota(jnp.int32, (tm, 1), 0)
    cos2, sin2 = _rope_tables((pos0 + ti * tm + row).astype(F32), freq_ref)
    idx = (row % c).astype(F32)
    qall = _dot(xb, win_ref[:, 0:hk])
    kall = _dot(xb, win_ref[:, hk:2 * hk])
    vall = _dot(xb, win_ref[:, 2 * hk:2 * hk + HEADS * RET_DV])
    for h in range(HEADS):
        hs = slice(h * DK, (h + 1) * DK)
        qr = _rope(qall[:, hs], cos2, sin2)
        kr = _rope(kall[:, hs], cos2, sin2) * DK ** -0.5
        q_s[h] = qr.astype(BF16)
        k_s[h] = kr.astype(BF16)
        qx_s[h] = (qr * jnp.exp((idx + 1.0) * _log_gamma(h))).astype(BF16)
        kz_s[h] = (kr * jnp.exp((c - 1.0 - idx) * _log_gamma(h))).astype(BF16)
        v_s[h] = vall[:, h * RET_DV:(h + 1) * RET_DV].astype(BF16)

    for n in range(nc):
        sl = slice(n * c, (n + 1) * c)
        v_c = v_s[:, sl, :]
        attn = _bdot('hid,hjd->hij', q_s[:, sl, :], k_s[:, sl, :]) * dec_s[...]
        s = s_ref[...]
        o_s[:, sl, :] = (_bdot('hij,hje->hie', attn.astype(BF16), v_c)
                         + _bdot('hcd,hde->hce', qx_s[:, sl, :], s.astype(BF16)))
        upd = _bdot('hcd,hce->hde', kz_s[:, sl, :], v_c)
        for h in range(HEADS):
            s_ref[h] = math.exp(c * _log_gamma(h)) * s[h] + upd[h]

    gate = _dot(xb, win_ref[:, 2 * hk + HEADS * RET_DV:])
    o_ref[...] = _ret_out([o_s[h] for h in range(HEADS)], gate, x, wo_ref, g_ref, b_ref)

    @pl.when(ti == pl.num_programs(1) - 1)
    def _():
        sout_ref[...] = s_ref[...]


def _ret_prompt(x, w, g, b, tm):
    bsz, t, _ = x.shape
    c = min(RET_CHUNK, tm)
    assert t % tm == 0 and tm % c == 0
    blk = pl.BlockSpec((None, tm, D_MODEL), lambda bi, ti: (bi, ti, 0))
    consts = (w["win"], w["freq"], w["wo"], g, b)
    head_k = pltpu.VMEM((HEADS, tm, DK), BF16)
    return pl.pallas_call(
        functools.partial(_ret_prompt_kernel, tm=tm, c=c, pos0=0),
        grid=(bsz, t // tm),
        in_specs=[blk] + [_resident(a.shape) for a in consts],
        out_specs=[blk, pl.BlockSpec((None, HEADS, DK, RET_DV), lambda bi, ti: (bi, 0, 0, 0))],
        out_shape=[jax.ShapeDtypeStruct(x.shape, F32), jax.ShapeDtypeStruct((bsz, HEADS, DK, RET_DV), F32)],
        scratch_shapes=[pltpu.VMEM((HEADS, DK, RET_DV), F32), head_k, head_k, head_k, head_k,
                        pltpu.VMEM((HEADS, tm, RET_DV), BF16),
                        pltpu.VMEM((HEADS, c, c), F32),
                        pltpu.VMEM((HEADS, tm, RET_DV), F32)],
        compiler_params=_params(2),
        name="ret_prompt",
    )(x, *consts)


def _ret_sample_a_kernel(x_ref, win_ref, freq_ref, kq_ref, vrows_ref, gate_ref, *, pos0):
    rows = x_ref.shape[0]
    hk = HEADS * DK
    xb = x_ref[...].astype(BF16)
    cos2, sin2 = _rope_tables(jnp.full((1, 1), pos0, F32), freq_ref)
    qall = _dot(xb, win_ref[:, 0:hk])
    kall = _dot(xb, win_ref[:, hk:2 * hk])
    vrows_ref[0] = _dot(xb, win_ref[:, 2 * hk:2 * hk + HEADS * RET_DV])
    gate_ref[...] = _dot(xb, win_ref[:, 2 * hk + HEADS * RET_DV:])
    for h in range(HEADS):
        hs = slice(h * DK, (h + 1) * DK)
        vs = slice(h * RET_DV, (h + 1) * RET_DV)
        qr = _rope(qall[:, hs], cos2, sin2)
        kr = _rope(kall[:, hs], cos2, sin2) * DK ** -0.5
        kq_ref[0, :, hs] = kr
        kq_ref[1, :, hs] = qr
        vrows_ref[1, :, vs] = jnp.broadcast_to(jnp.sum(qr * kr, -1, keepdims=True), (rows, RET_DV))
        vrows_ref[2, :, vs] = jnp.full((rows, RET_DV), math.exp(_log_gamma(h)), F32)
    kq_ref[2:8] = jnp.zeros((6, rows, hk), F32)
    vrows_ref[3:8] = jnp.zeros((5, rows, HEADS * RET_DV), F32)


def _ret_sample_b_kernel(kq_ref, v_ref, s_ref, o_ref, sout_ref):
    bb = kq_ref.shape[0]
    n = bb * HEADS
    kqb = kq_ref[...].reshape(n, SUBLANES, DK).astype(BF16)
    v8 = v_ref[...].reshape(n, SUBLANES, RET_DV)
    s = s_ref[...].reshape(n, DK, RET_DV)
    r = _bdot('nqd,nde->nqe', kqb, s.astype(BF16))
    gamma = v8[:, 2:3, :]
    o = v8[:, 1:2, :] * v8[:, 0:1, :] + gamma * r[:, 1:2, :]
    row0 = lax.broadcasted_iota(jnp.int32, (n, SUBLANES, RET_DV), 1) == 0
    s_new = s * gamma + _bdot('nqd,nqe->nde', kqb, jnp.where(row0, v8, 0.0).astype(BF16))
    o_ref[...] = jnp.broadcast_to(o, (n, SUBLANES, RET_DV)).reshape(bb, HEADS, SUBLANES, RET_DV)
    sout_ref[...] = s_new.reshape(bb, HEADS, DK, RET_DV)


def _ret_sample_c_kernel(o_ref, gate_ref, x_ref, wo_ref, g_ref, b_ref, out_ref):
    o = o_ref[...]
    out_ref[...] = _ret_out([o[:, h * RET_DV:(h + 1) * RET_DV] for h in range(HEADS)],
                            gate_ref[...], x_ref[...], wo_ref, g_ref, b_ref)


def _ret_sample(x2d, s0, w, g, b):
    bsz = x2d.shape[0]
    kq, vrows, gate = _whole_call(
        functools.partial(_ret_sample_a_kernel, pos0=float(PAST_LEN)),
        (x2d, w["win"], w["freq"]),
        [(SUBLANES, bsz, HEADS * DK), (SUBLANES, bsz, HEADS * RET_DV), (bsz, HEADS * RET_DV)], "ret_sample_a")
    o8, s_new = _state_step(_ret_sample_b_kernel, _head_rows(kq, DK), (_head_rows(vrows, RET_DV),), s0, 4,
                            "ret_sample_b")
    o = o8[:, :, 0, :].reshape(bsz, HEADS * RET_DV)
    (x1,) = _whole_call(_ret_sample_c_kernel, (o, gate, x2d, w["wo"], g, b), [(bsz, D_MODEL)], "ret_sample_c")
    return x1, s_new


def _lane_row(values, offset):
    return jnp.zeros((1, LANES), F32).at[0, offset:offset + HEADS].set(values.astype(F32))


def _gdn_weights(w_in, conv_w, a_log, dt_bias, norm_g, w_out):
    o1 = GDN_CONV_CH + HEADS * GDN_DV
    wba = jnp.zeros((D_MODEL, LANES), F32).at[:, :2 * HEADS].set(w_in[:, o1:])
    return {"wqkv": w_in[:, :GDN_CONV_CH].astype(BF16), "wz": w_in[:, GDN_CONV_CH:o1].astype(BF16),
            "wba": wba.astype(BF16), "conv_w": conv_w,
            "alog": _lane_row(a_log, HEADS), "dtb": _lane_row(dt_bias, HEADS),
            "ng": norm_g.reshape(1, GDN_DV), "wo": w_out.astype(BF16)}


def _ret_weights(w_in, w_out):
    half = DK // 2
    freqs = ROPE_BASE ** (-jnp.arange(half, dtype=F32) / half)
    return {"win": w_in.astype(BF16), "wo": w_out.astype(BF16),
            "freq": jnp.concatenate([freqs, freqs]).reshape(1, DK)}


def kernel(x_prompt, x_sample, state_pool, state_gdn_conv, state_gdn, state_ret, pool_w, pool_scale, gdn_w_in,
           gdn_conv_w, gdn_a_log, gdn_dt_bias, gdn_norm_g, gdn_w_out, ret_w_in, ret_w_out, ffn_w13, ffn_w2,
           ln_g, ln_b):
    bsz, seq, _ = x_prompt.shape
    dec = x_sample.shape[0]
    tm = min(512, seq)
    xp = x_prompt
    xs = x_sample.reshape(dec, D_MODEL)
    pool_p, pool_s, conv_p, conv_s, gdn_p, gdn_s, ret_p, ret_s = [], [], [], [], [], [], [], []
    for i in range(DEPTH):
        kind, j = i % N_MIXERS, i // N_MIXERS
        g1, b1 = ln_g[i, 0].reshape(1, D_MODEL), ln_b[i, 0].reshape(1, D_MODEL)
        g2, b2 = ln_g[i, 1].reshape(1, D_MODEL), ln_b[i, 1].reshape(1, D_MODEL)
        if kind == 0:
            pw, sc = pool_w[j].astype(BF16), pool_scale[j].reshape(1, D_MODEL)
            pool_p.append(xp[:, seq - POOL_BUF:])
            pool_s.append(jnp.concatenate([state_pool[j][:, 1:], xs[:, None, :]], axis=1))
            xp = _pool_prompt(xp, pw, sc, g1, b1, tm)
            xs = _pool_sample(xs, state_pool[j].transpose(1, 0, 2), pw, sc, g1, b1)
        elif kind == 1:
            w = _gdn_weights(gdn_w_in[j], gdn_conv_w[j], gdn_a_log[j], gdn_dt_bias[j], gdn_norm_g[j], gdn_w_out[j])
            xp, cp, sp = _gdn_prompt(xp, w, g1, b1, tm)
            xs, cs, ss = _gdn_sample(xs, state_gdn_conv[j], state_gdn[j], w, g1, b1)
            conv_p.append(cp), conv_s.append(cs), gdn_p.append(sp), gdn_s.append(ss)
        else:
            w = _ret_weights(ret_w_in[j], ret_w_out[j])
            xp, sp = _ret_prompt(xp, w, g1, b1, tm)
            xs, ss = _ret_sample(xs, state_ret[j], w, g1, b1)
            ret_p.append(sp), ret_s.append(ss)
        w13, w2 = ffn_w13[i].astype(BF16), ffn_w2[i].astype(BF16)
        xp = _ffn(xp.reshape(bsz * seq, D_MODEL), w13, w2, g2, b2, tm).reshape(bsz, seq, D_MODEL)
        xs = _ffn(xs, w13, w2, g2, b2, dec)
    return (xp, xs.reshape(dec, 1, D_MODEL), jnp.stack(pool_p), jnp.stack(pool_s), jnp.stack(conv_p),
            jnp.stack(conv_s), jnp.stack(gdn_p), jnp.stack(gdn_s), jnp.stack(ret_p), jnp.stack(ret_s))
```

```python
import functools
import math

import jax
import jax.numpy as jnp
from jax import lax
from jax.experimental import pallas as pl
from jax.experimental.pallas import tpu as pltpu

F32 = jnp.float32
BF16 = jnp.bfloat16

D_MODEL = 1024
DEPTH = 4
PAST_LEN = 16384
N_MIXERS = 3

POOL_WINDOWS = (2, 4, 8, 16)
POOL_GROUP = D_MODEL // len(POOL_WINDOWS)
POOL_BUF = max(POOL_WINDOWS) - 1
POOL_HALO = 16

HEADS = 8
DK = D_MODEL // HEADS
GDN_DV = DK
GDN_CONV = 4
GDN_CONV_CH = HEADS * (2 * DK + GDN_DV)
CONV_HALO = 8
GDN_CHUNK = 64
HEAD_GROUP = 2
RET_DV = 2 * DK
RET_CHUNK = 256
ROPE_BASE = 10000.0

D_FF = -(-8 * D_MODEL // (3 * 256)) * 256
FF_CHUNK = 256
DN_ALPHA = (2 * DEPTH) ** 0.25
LN_EPS = 1e-5
RMS_EPS = 1e-6

SUBLANES = 8
LANES = 128
VMEM_LIMIT = 56 * 1024 * 1024

HIGHEST = lax.Precision.HIGHEST


def _params(n_grid):
    return pltpu.CompilerParams(dimension_semantics=("arbitrary",) * n_grid, vmem_limit_bytes=VMEM_LIMIT)


def _resident(shape):
    nd = len(shape)
    return pl.BlockSpec(shape, lambda *_: (0,) * nd, pipeline_mode=pl.Buffered(1))


def _whole(shape):
    nd = len(shape)
    return pl.BlockSpec(shape, lambda *_: (0,) * nd)


def _dot(a, b):
    return jnp.dot(a, b, preferred_element_type=F32)


def _bdot(spec, a, b):
    return jnp.einsum(spec, a, b, preferred_element_type=F32)


def _silu(x):
    return x * jax.nn.sigmoid(x)


def _layer_norm(y, g, b):
    mu = jnp.mean(y, -1, keepdims=True)
    yc = y - mu
    var = jnp.mean(yc * yc, -1, keepdims=True)
    return yc * lax.rsqrt(var + LN_EPS) * g + b


def _ffn_kernel(x_ref, w13_ref, w2_ref, g_ref, b_ref, o_ref):
    x = x_ref[...]
    xb = x.astype(BF16)
    acc = None
    for c in range(D_FF // FF_CHUNK):
        lo = c * FF_CHUNK
        a = _dot(xb, w13_ref[:, lo:lo + FF_CHUNK])
        b = _dot(xb, w13_ref[:, D_FF + lo:D_FF + lo + FF_CHUNK])
        h = (_silu(a) * b).astype(BF16)
        p = _dot(h, w2_ref[lo:lo + FF_CHUNK, :])
        acc = p if acc is None else acc + p
    o_ref[...] = _layer_norm(DN_ALPHA * x + acc, g_ref[...], b_ref[...])


def _ffn(x2d, w13, w2, g, b, tm):
    n = x2d.shape[0]
    assert n % tm == 0
    row = pl.BlockSpec((tm, D_MODEL), lambda i: (i, 0))
    return pl.pallas_call(
        _ffn_kernel,
        grid=(n // tm,),
        in_specs=[row, _resident(w13.shape), _resident(w2.shape), _resident(g.shape), _resident(b.shape)],
        out_specs=row,
        out_shape=jax.ShapeDtypeStruct((n, D_MODEL), F32),
        compiler_params=_params(1),
        name="ffn",
    )(x2d, w13, w2, g, b)


def _pool_mix(x, shifted, cnt_of, pw_ref, sc_ref):
    outs = []
    for gi, win in enumerate(POOL_WINDOWS):
        lo, hi = gi * POOL_GROUP, (gi + 1) * POOL_GROUP
        xs = x[:, lo:hi]
        s = xs
        for d in range(1, win):
            s = s + shifted(d, lo, hi)
        pooled = s / cnt_of(win) - xs
        outs.append(_dot(pooled.astype(BF16), pw_ref[gi]))
    return jnp.concatenate(outs, axis=-1) * sc_ref[...]


def _pool_prompt_kernel(x_ref, pw_ref, sc_ref, g_ref, b_ref, o_ref, ext_ref, *, tm, n_valid):
    ti = pl.program_id(1)

    @pl.when(ti == 0)
    def _():
        ext_ref[0:POOL_HALO, :] = jnp.zeros((POOL_HALO, D_MODEL), F32)

    x = x_ref[...]
    ext_ref[POOL_HALO:POOL_HALO + tm, :] = x
    t = ti * tm + lax.broadcasted_iota(jnp.int32, (tm, 1), 0)
    mix = _pool_mix(
        x,
        lambda d, lo, hi: ext_ref[pl.ds(POOL_HALO - d, tm), lo:hi],
        lambda win: jnp.minimum(t + 1 + n_valid, win).astype(F32),
        pw_ref, sc_ref)
    o_ref[...] = _layer_norm(DN_ALPHA * x + mix, g_ref[...], b_ref[...])
    ext_ref[0:POOL_HALO, :] = ext_ref[tm:tm + POOL_HALO, :]


def _pool_prompt(x, pw, sc, g, b, tm):
    bsz, t, _ = x.shape
    assert t % tm == 0
    blk = pl.BlockSpec((None, tm, D_MODEL), lambda bi, ti: (bi, ti, 0))
    return pl.pallas_call(
        functools.partial(_pool_prompt_kernel, tm=tm, n_valid=0),
        grid=(bsz, t // tm),
        in_specs=[blk, _resident(pw.shape), _resident(sc.shape), _resident(g.shape), _resident(b.shape)],
        out_specs=blk,
        out_shape=jax.ShapeDtypeStruct(x.shape, F32),
        scratch_shapes=[pltpu.VMEM((tm + POOL_HALO, D_MODEL), F32)],
        compiler_params=_params(2),
        name="pool_prompt",
    )(x, pw, sc, g, b)


def _pool_sample_kernel(x_ref, buf_ref, pw_ref, sc_ref, g_ref, b_ref, o_ref, *, n_valid):
    x = x_ref[...]
    mix = _pool_mix(
        x,
        lambda d, lo, hi: buf_ref[POOL_BUF - d, :, lo:hi],
        lambda win: float(min(1 + n_valid, win)),
        pw_ref, sc_ref)
    o_ref[...] = _layer_norm(DN_ALPHA * x + mix, g_ref[...], b_ref[...])


def _pool_sample(x2d, buf_t, pw, sc, g, b):
    args = (x2d, buf_t, pw, sc, g, b)
    return pl.pallas_call(
        functools.partial(_pool_sample_kernel, n_valid=min(PAST_LEN, POOL_BUF)),
        grid=(1,),
        in_specs=[_resident(a.shape) for a in args],
        out_specs=_whole(x2d.shape),
        out_shape=jax.ShapeDtypeStruct(x2d.shape, F32),
        compiler_params=_params(1),
        name="pool_sample",
    )(*args)


def _softplus(x):
    return jnp.maximum(x, 0.0) + jnp.log1p(jnp.exp(-jnp.abs(x)))


def _gdn_gates(ba, alog_ref, dtb_ref):
    return jax.nn.sigmoid(ba), -jnp.exp(alog_ref[...]) * _softplus(ba + dtb_ref[...])


def _l2norm(v, scale=1.0):
    return v * (lax.rsqrt(jnp.sum(v * v, -1, keepdims=True) + RMS_EPS) * scale)


def _gdn_out(o_heads, z, x, ng_ref, wo_ref, g_ref, b_ref):
    gated = []
    for h, oh in enumerate(o_heads):
        on = oh * lax.rsqrt(jnp.mean(oh * oh, -1, keepdims=True) + RMS_EPS) * ng_ref[...]
        gated.append((on * _silu(z[:, h * GDN_DV:(h + 1) * GDN_DV])).astype(BF16))
    mix = _dot(jnp.concatenate(gated, axis=-1), wo_ref[...])
    return _layer_norm(DN_ALPHA * x + mix, g_ref[...], b_ref[...])


def _unit_lower_inverse(a):
    c = a.shape[-1]
    ii = lax.broadcasted_iota(jnp.int32, (c, c), 0)
    jj = lax.broadcasted_iota(jnp.int32, (c, c), 1)
    eye = jnp.where(ii == jj, 1.0, 0.0).astype(F32)
    x = eye - jnp.where((ii // 2 == jj // 2) & (ii > jj), a, 0.0)
    s = 4
    while s <= c:
        off = jnp.where((ii // s == jj // s) & (ii // (s // 2) != jj // (s // 2)) & (ii > jj), a, 0.0)
        xb = x.astype(BF16)
        y = _bdot('hij,hjk->hik', off.astype(BF16), xb)
        x = x - _bdot('hij,hjk->hik', xb, y.astype(BF16))
        s *= 2
    return x


def _gdn_prompt_kernel(x_ref, wqkv_ref, wz_ref, wba_ref, cw_ref, alog_ref, dtb_ref, ng_ref, wo_ref, g_ref, b_ref,
                       o_ref, conv_ref, sout_ref,
                       ext_ref, s_ref, q_s, k_s, kb_s, vbk_s, kg_s, wkqg_s, u_s, at_s, dec_s, gl_s, o_s, *, tm):
    c = GDN_CHUNK
    nc = tm // c
    ti = pl.program_id(1)

    @pl.when(ti == 0)
    def _():
        ext_ref[0:CONV_HALO, :] = jnp.zeros((CONV_HALO, GDN_CONV_CH), F32)
        s_ref[...] = jnp.zeros(s_ref.shape, F32)

    x = x_ref[...]
    xb = x.astype(BF16)
    ext_ref[CONV_HALO:CONV_HALO + tm, :] = _dot(xb, wqkv_ref[...])
    conv_ref[...] = ext_ref[tm + CONV_HALO - (GDN_CONV - 1):tm + CONV_HALO, :]

    def conv_silu(col):
        lo, hi = col * LANES, (col + 1) * LANES
        e = ext_ref[:, lo:hi]
        y = e[CONV_HALO:] * cw_ref[GDN_CONV - 1:GDN_CONV, lo:hi]
        for j in range(GDN_CONV - 1):
            y = y + pltpu.roll(e, GDN_CONV - 1 - j, 0)[CONV_HALO:] * cw_ref[j:j + 1, lo:hi]
        return _silu(y)

    beta_all, gl = _gdn_gates(_dot(xb, wba_ref[...]), alog_ref, dtb_ref)
    pos = lax.broadcasted_iota(jnp.int32, (tm, LANES), 0) % c
    gc = gl
    step = 1
    while step < c:
        gc = gc + jnp.where(pos >= step, pltpu.roll(gc, step, 0), 0.0)
        step *= 2
    gcl = jnp.broadcast_to(gc.reshape(nc, c, LANES)[:, c - 1:c, :], (nc, c, LANES)).reshape(tm, LANES)
    eg = jnp.exp(gc)
    kdec = jnp.exp(gcl - gc)
    glast = jnp.exp(gcl)
    gct = gc.T

    ii = lax.broadcasted_iota(jnp.int32, (c, c), 0)
    jj = lax.broadcasted_iota(jnp.int32, (c, c), 1)
    lower = ii >= jj
    for h in range(HEADS):
        qn = _l2norm(conv_silu(h), DK ** -0.5)
        kn = _l2norm(conv_silu(HEADS + h))
        vh = conv_silu(2 * HEADS + h)
        bh = beta_all[:, h:h + 1]
        egh = eg[:, HEADS + h:HEADS + h + 1]
        kb = kn * bh
        q_s[h] = qn.astype(BF16)
        k_s[h] = kn.astype(BF16)
        kb_s[h] = kb.astype(BF16)
        vbk_s[h, :, 0:GDN_DV] = (vh * bh).astype(BF16)
        vbk_s[h, :, GDN_DV:] = (kb * egh).astype(BF16)
        wkqg_s[h, :, c:, :] = (qn * egh).astype(BF16).reshape(nc, c, DK)
        kg_s[h] = (kn * kdec[:, HEADS + h:HEADS + h + 1]).astype(BF16)
        for n in range(nc):
            col = gc[n * c:(n + 1) * c, HEADS + h:HEADS + h + 1]
            row = gct[HEADS + h:HEADS + h + 1, n * c:(n + 1) * c]
            dec_s[h, n] = jnp.where(lower, jnp.exp(jnp.where(lower, col - row, 0.0)), 0.0)
            gl_s[h, n] = jnp.broadcast_to(glast[n * c:n * c + 1, HEADS + h:HEADS + h + 1], (SUBLANES, LANES))

    ext_ref[0:CONV_HALO, :] = ext_ref[tm:tm + CONV_HALO, :]

    nb = HEAD_GROUP * nc

    def precompute(gi, carry):
        hs = pl.ds(gi * HEAD_GROUP, HEAD_GROUP)
        dec = dec_s[hs].reshape(nb, c, c)
        kn_c = k_s[hs].reshape(nb, c, DK)
        a = jnp.where(ii > jj, _bdot('bid,bjd->bij', kb_s[hs].reshape(nb, c, DK), kn_c) * dec, 0.0)
        tinv = _unit_lower_inverse(a).astype(BF16)
        uw = _bdot('bij,bjd->bid', tinv, vbk_s[hs].reshape(nb, c, 2 * GDN_DV))
        u_s[hs] = uw[:, :, 0:GDN_DV].reshape(HEAD_GROUP, nc, c, GDN_DV)
        wkqg_s[hs, :, 0:c, :] = uw[:, :, GDN_DV:].astype(BF16).reshape(HEAD_GROUP, nc, c, DK)
        attn = _bdot('bid,bjd->bij', q_s[hs].reshape(nb, c, DK), kn_c) * dec
        at_s[hs] = attn.astype(BF16).reshape(HEAD_GROUP, nc, c, c)
        return carry

    lax.fori_loop(0, HEADS // HEAD_GROUP, precompute, 0)

    def chunk(n, carry):
        sl = pl.ds(pl.multiple_of(n * c, c), c)
        s = s_ref[...]
        r = _bdot('hid,hde->hie', wkqg_s[:, n], s.astype(BF16))
        vnb = (u_s[:, n] - r[:, 0:c, :]).astype(BF16)
        o_s[:, sl, :] = r[:, c:, :] + _bdot('hij,hje->hie', at_s[:, n], vnb)
        s_ref[...] = s * gl_s[:, n][:, 0:1, :] + _bdot('hcd,hce->hde', kg_s[:, sl, :], vnb)
        return carry

    lax.fori_loop(0, nc, chunk, 0)

    z = _dot(xb, wz_ref[...])
    o_ref[...] = _gdn_out([o_s[h] for h in range(HEADS)], z, x, ng_ref, wo_ref, g_ref, b_ref)

    @pl.when(ti == pl.num_programs(1) - 1)
    def _():
        sout_ref[...] = s_ref[...]


def _gdn_prompt(x, w, g, b, tm):
    bsz, t, _ = x.shape
    assert t % tm == 0 and tm % GDN_CHUNK == 0
    c = GDN_CHUNK
    nc = tm // c
    blk = pl.BlockSpec((None, tm, D_MODEL), lambda bi, ti: (bi, ti, 0))
    consts = (w["wqkv"], w["wz"], w["wba"], w["conv_w"], w["alog"], w["dtb"], w["ng"], w["wo"], g, b)
    head_bf16 = pltpu.VMEM((HEADS, tm, DK), BF16)
    return pl.pallas_call(
        functools.partial(_gdn_prompt_kernel, tm=tm),
        grid=(bsz, t // tm),
        in_specs=[blk] + [_resident(a.shape) for a in consts],
        out_specs=[blk,
                   pl.BlockSpec((None, GDN_CONV - 1, GDN_CONV_CH), lambda bi, ti: (bi, 0, 0)),
                   pl.BlockSpec((None, HEADS, DK, GDN_DV), lambda bi, ti: (bi, 0, 0, 0))],
        out_shape=[jax.ShapeDtypeStruct(x.shape, F32),
                   jax.ShapeDtypeStruct((bsz, GDN_CONV - 1, GDN_CONV_CH), F32),
                   jax.ShapeDtypeStruct((bsz, HEADS, DK, GDN_DV), F32)],
        scratch_shapes=[pltpu.VMEM((tm + CONV_HALO, GDN_CONV_CH), F32),
                        pltpu.VMEM((HEADS, DK, GDN_DV), F32),
                        head_bf16, head_bf16, head_bf16,
                        pltpu.VMEM((HEADS, tm, 2 * GDN_DV), BF16),
                        head_bf16,
                        pltpu.VMEM((HEADS, nc, 2 * c, DK), BF16),
                        pltpu.VMEM((HEADS, nc, c, GDN_DV), F32),
                        pltpu.VMEM((HEADS, nc, c, c), BF16),
                        pltpu.VMEM((HEADS, nc, c, c), F32),
                        pltpu.VMEM((HEADS, nc, SUBLANES, LANES), F32),
                        pltpu.VMEM((HEADS, tm, GDN_DV), F32)],
        compiler_params=_params(2),
        name="gdn_prompt",
    )(x, *consts)


def _gdn_sample_a_kernel(x_ref, cb_ref, wqkv_ref, wz_ref, wba_ref, cw_ref, alog_ref, dtb_ref,
                         rows_ref, pre_ref, z_ref):
    xb = x_ref[...].astype(BF16)
    pre = _dot(xb, wqkv_ref[...])
    pre_ref[...] = pre
    z_ref[...] = _dot(xb, wz_ref[...])
    y = pre * cw_ref[GDN_CONV - 1:GDN_CONV, :]
    for j in range(GDN_CONV - 1):
        y = y + cb_ref[j] * cw_ref[j:j + 1, :]
    qkv = _silu(y)
    beta_all, gl = _gdn_gates(_dot(xb, wba_ref[...]), alog_ref, dtb_ref)
    eg = jnp.exp(gl)
    rows = x_ref.shape[0]
    for h in range(HEADS):
        hs = slice(h * DK, (h + 1) * DK)
        qn = _l2norm(qkv[:, hs], DK ** -0.5)
        kn = _l2norm(qkv[:, HEADS * DK + h * DK:HEADS * DK + (h + 1) * DK])
        vh = qkv[:, 2 * HEADS * DK + h * GDN_DV:2 * HEADS * DK + (h + 1) * GDN_DV]
        bh = beta_all[:, h:h + 1]
        egh = eg[:, HEADS + h:HEADS + h + 1]
        rows_ref[0, :, hs] = kn
        rows_ref[1, :, hs] = qn
        rows_ref[2, :, hs] = vh * bh
        rows_ref[3, :, hs] = jnp.broadcast_to(bh * egh, (rows, DK))
        rows_ref[4, :, hs] = jnp.broadcast_to(egh, (rows, DK))
        rows_ref[5, :, hs] = jnp.broadcast_to(jnp.sum(qn * kn, -1, keepdims=True), (rows, DK))
    rows_ref[6:8] = jnp.zeros((2, rows, D_MODEL), F32)


def _gdn_sample_b_kernel(p_ref, s_ref, o_ref, sout_ref):
    bb = p_ref.shape[0]
    n = bb * HEADS
    p = p_ref[...].reshape(n, SUBLANES, DK)
    s = s_ref[...].reshape(n, DK, GDN_DV)
    pb = p.astype(BF16)
    r = _bdot('nqd,nde->nqe', pb, s.astype(BF16))
    v_new = p[:, 2:3, :] - p[:, 3:4, :] * r[:, 0:1, :]
    o = p[:, 4:5, :] * r[:, 1:2, :] + p[:, 5:6, :] * v_new
    row0 = lax.broadcasted_iota(jnp.int32, (n, SUBLANES, GDN_DV), 1) == 0
    vn8 = jnp.where(row0, jnp.broadcast_to(v_new, (n, SUBLANES, GDN_DV)), 0.0)
    s_new = s * p[:, 4:5, :] + _bdot('nqd,nqe->nde', pb, vn8.astype(BF16))
    o_ref[...] = jnp.broadcast_to(o, (n, SUBLANES, GDN_DV)).reshape(bb, HEADS, SUBLANES, GDN_DV)
    sout_ref[...] = s_new.reshape(bb, HEADS, DK, GDN_DV)


def _gdn_sample_c_kernel(o_ref, z_ref, x_ref, ng_ref, wo_ref, g_ref, b_ref, out_ref):
    o = o_ref[...]
    out_ref[...] = _gdn_out([o[:, h * GDN_DV:(h + 1) * GDN_DV] for h in range(HEADS)],
                            z_ref[...], x_ref[...], ng_ref, wo_ref, g_ref, b_ref)


def _whole_call(kernel, args, out_shapes, name):
    outs = [jax.ShapeDtypeStruct(s, F32) for s in out_shapes]
    return pl.pallas_call(
        kernel,
        grid=(1,),
        in_specs=[_resident(a.shape) for a in args],
        out_specs=[_whole(s) for s in out_shapes],
        out_shape=outs,
        compiler_params=_params(1),
        name=name,
    )(*args)


def _state_step(kernel, p, extra, s0, bb, name):
    bsz = s0.shape[0]
    assert bsz % bb == 0

    def blk(a):
        return pl.BlockSpec((bb,) + a.shape[1:], lambda i: (i,) + (0,) * (a.ndim - 1))

    ins = (p,) + tuple(extra) + (s0,)
    o_shape = (bsz, HEADS, SUBLANES, s0.shape[-1])
    return pl.pallas_call(
        kernel,
        grid=(bsz // bb,),
        in_specs=[blk(a) for a in ins],
        out_specs=[pl.BlockSpec((bb,) + o_shape[1:], lambda i: (i, 0, 0, 0)), blk(s0)],
        out_shape=[jax.ShapeDtypeStruct(o_shape, F32), jax.ShapeDtypeStruct(s0.shape, F32)],
        compiler_params=_params(1),
        name=name,
    )(*ins)


def _head_rows(rows, width):
    r, bsz, _ = rows.shape
    return rows.reshape(r, bsz, HEADS, width).transpose(1, 2, 0, 3)


def _gdn_sample(x2d, conv_buf, s0, w, g, b):
    bsz = x2d.shape[0]
    cb_t = conv_buf.transpose(1, 0, 2)
    rows, pre, z = _whole_call(
        _gdn_sample_a_kernel,
        (x2d, cb_t, w["wqkv"], w["wz"], w["wba"], w["conv_w"], w["alog"], w["dtb"]),
        [(SUBLANES, bsz, D_MODEL), (bsz, GDN_CONV_CH), (bsz, D_MODEL)], "gdn_sample_a")
    o8, s_new = _state_step(_gdn_sample_b_kernel, _head_rows(rows, DK), (), s0, 8, "gdn_sample_b")
    o = o8[:, :, 0, :].reshape(bsz, HEADS * GDN_DV)
    (x1,) = _whole_call(_gdn_sample_c_kernel, (o, z, x2d, w["ng"], w["wo"], g, b), [(bsz, D_MODEL)], "gdn_sample_c")
    new_conv = jnp.concatenate([conv_buf[:, 1:], pre[:, None, :]], axis=1)
    return x1, new_conv, s_new


def _log_gamma(h):
    return math.log(1.0 - 2.0 ** (-5.0 - h))


def _rope_tables(pos, freq_ref):
    ang = pos * freq_ref[...]
    lane = lax.broadcasted_iota(jnp.int32, ang.shape, 1)
    return jnp.cos(ang), jnp.where(lane < DK // 2, -1.0, 1.0) * jnp.sin(ang)


def _rope(v, cos2, sin2):
    return v * cos2 + pltpu.roll(v, DK // 2, 1) * sin2


def _ret_out(o_heads, gate, x, wo_ref, g_ref, b_ref):
    gated = []
    for h, oh in enumerate(o_heads):
        mu = jnp.mean(oh, -1, keepdims=True)
        oc = oh - mu
        on = oc * lax.rsqrt(jnp.mean(oc * oc, -1, keepdims=True) + LN_EPS)
        gated.append((_silu(gate[:, h * RET_DV:(h + 1) * RET_DV]) * on).astype(BF16))
    mix = _dot(jnp.concatenate(gated, axis=-1), wo_ref[...])
    return _layer_norm(DN_ALPHA * x + mix, g_ref[...], b_ref[...])


def _ret_prompt_kernel(x_ref, win_ref, freq_ref, wo_ref, g_ref, b_ref, o_ref, sout_ref,
                       s_ref, q_s, k_s, qx_s, kz_s, v_s, dec_s, o_s, *, tm, c, pos0):
    nc = tm // c
    ti = pl.program_id(1)
    hk = HEADS * DK

    @pl.when(ti == 0)
    def _():
        s_ref[...] = jnp.zeros(s_ref.shape, F32)
        diff = (lax.broadcasted_iota(jnp.int32, (c, c), 0) - lax.broadcasted_iota(jnp.int32, (c, c), 1)).astype(F32)
        for h in range(HEADS):
            dec_s[h] = jnp.where(diff >= 0, jnp.exp(jnp.maximum(diff, 0.0) * _log_gamma(h)), 0.0)

    x = x_ref[...]
    xb = x.astype(BF16)
    row = lax.broadcasted_iota(jnp.int32, (tm, 1), 0)
    cos2, sin2 = _rope_tables((pos0 + ti * tm + row).astype(F32), freq_ref)
    idx = (row % c).astype(F32)
    qall = _dot(xb, win_ref[:, 0:hk])
    kall = _dot(xb, win_ref[:, hk:2 * hk])
    vall = _dot(xb, win_ref[:, 2 * hk:2 * hk + HEADS * RET_DV])
    for h in range(HEADS):
        hs = slice(h * DK, (h + 1) * DK)
        qr = _rope(qall[:, hs], cos2, sin2)
        kr = _rope(kall[:, hs], cos2, sin2) * DK ** -0.5
        q_s[h] = qr.astype(BF16)
        k_s[h] = kr.astype(BF16)
        qx_s[h] = (qr * jnp.exp((idx + 1.0) * _log_gamma(h))).astype(BF16)
        kz_s[h] = (kr * jnp.exp((c - 1.0 - idx) * _log_gamma(h))).astype(BF16)
        v_s[h] = vall[:, h * RET_DV:(h + 1) * RET_DV].astype(BF16)

    for n in range(nc):
        sl = slice(n * c, (n + 1) * c)
        v_c = v_s[:, sl, :]
        attn = _bdot('hid,hjd->hij', q_s[:, sl, :], k_s[:, sl, :]) * dec_s[...]
        s = s_ref[...]
        o_s[:, sl, :] = (_bdot('hij,hje->hie', attn.astype(BF16), v_c)
                         + _bdot('hcd,hde->hce', qx_s[:, sl, :], s.astype(BF16)))
        upd = _bdot('hcd,hce->hde', kz_s[:, sl, :], v_c)
        for h in range(HEADS):
            s_ref[h] = math.exp(c * _log_gamma(h)) * s[h] + upd[h]

    gate = _dot(xb, win_ref[:, 2 * hk + HEADS * RET_DV:])
    o_ref[...] = _ret_out([o_s[h] for h in range(HEADS)], gate, x, wo_ref, g_ref, b_ref)

    @pl.when(ti == pl.num_programs(1) - 1)
    def _():
        sout_ref[...] = s_ref[...]


def _ret_prompt(x, w, g, b, tm):
    bsz, t, _ = x.shape
    c = min(RET_CHUNK, tm)
    assert t % tm == 0 and tm % c == 0
    blk = pl.BlockSpec((None, tm, D_MODEL), lambda bi, ti: (bi, ti, 0))
    consts = (w["win"], w["freq"], w["wo"], g, b)
    head_k = pltpu.VMEM((HEADS, tm, DK), BF16)
    return pl.pallas_call(
        functools.partial(_ret_prompt_kernel, tm=tm, c=c, pos0=0),
        grid=(bsz, t // tm),
        in_specs=[blk] + [_resident(a.shape) for a in consts],
        out_specs=[blk, pl.BlockSpec((None, HEADS, DK, RET_DV), lambda bi, ti: (bi, 0, 0, 0))],
        out_shape=[jax.ShapeDtypeStruct(x.shape, F32), jax.ShapeDtypeStruct((bsz, HEADS, DK, RET_DV), F32)],
        scratch_shapes=[pltpu.VMEM((HEADS, DK, RET_DV), F32), head_k, head_k, head_k, head_k,
                        pltpu.VMEM((HEADS, tm, RET_DV), BF16),
                        pltpu.VMEM((HEADS, c, c), F32),
                        pltpu.VMEM((HEADS, tm, RET_DV), F32)],
        compiler_params=_params(2),
        name="ret_prompt",
    )(x, *consts)


def _ret_sample_a_kernel(x_ref, win_ref, freq_ref, kq_ref, vrows_ref, gate_ref, *, pos0):
    rows = x_ref.shape[0]
    hk = HEADS * DK
    xb = x_ref[...].astype(BF16)
    cos2, sin2 = _rope_tables(jnp.full((1, 1), pos0, F32), freq_ref)
    qall = _dot(xb, win_ref[:, 0:hk])
    kall = _dot(xb, win_ref[:, hk:2 * hk])
    vrows_ref[0] = _dot(xb, win_ref[:, 2 * hk:2 * hk + HEADS * RET_DV])
    gate_ref[...] = _dot(xb, win_ref[:, 2 * hk + HEADS * RET_DV:])
    for h in range(HEADS):
        hs = slice(h * DK, (h + 1) * DK)
        vs = slice(h * RET_DV, (h + 1) * RET_DV)
        qr = _rope(qall[:, hs], cos2, sin2)
        kr = _rope(kall[:, hs], cos2, sin2) * DK ** -0.5
        kq_ref[0, :, hs] = kr
        kq_ref[1, :, hs] = qr
        vrows_ref[1, :, vs] = jnp.broadcast_to(jnp.sum(qr * kr, -1, keepdims=True), (rows, RET_DV))
        vrows_ref[2, :, vs] = jnp.full((rows, RET_DV), math.exp(_log_gamma(h)), F32)
    kq_ref[2:8] = jnp.zeros((6, rows, hk), F32)
    vrows_ref[3:8] = jnp.zeros((5, rows, HEADS * RET_DV), F32)


def _ret_sample_b_kernel(kq_ref, v_ref, s_ref, o_ref, sout_ref):
    bb = kq_ref.shape[0]
    n = bb * HEADS
    kqb = kq_ref[...].reshape(n, SUBLANES, DK).astype(BF16)
    v8 = v_ref[...].reshape(n, SUBLANES, RET_DV)
    s = s_ref[...].reshape(n, DK, RET_DV)
    r = _bdot('nqd,nde->nqe', kqb, s.astype(BF16))
    gamma = v8[:, 2:3, :]
    o = v8[:, 1:2, :] * v8[:, 0:1, :] + gamma * r[:, 1:2, :]
    row0 = lax.broadcasted_iota(jnp.int32, (n, SUBLANES, RET_DV), 1) == 0
    s_new = s * gamma + _bdot('nqd,nqe->nde', kqb, jnp.where(row0, v8, 0.0).astype(BF16))
    o_ref[...] = jnp.broadcast_to(o, (n, SUBLANES, RET_DV)).reshape(bb, HEADS, SUBLANES, RET_DV)
    sout_ref[...] = s_new.reshape(bb, HEADS, DK, RET_DV)


def _ret_sample_c_kernel(o_ref, gate_ref, x_ref, wo_ref, g_ref, b_ref, out_ref):
    o = o_ref[...]
    out_ref[...] = _ret_out([o[:, h * RET_DV:(h + 1) * RET_DV] for h in range(HEADS)],
                            gate_ref[...], x_ref[...], wo_ref, g_ref, b_ref)


def _ret_sample(x2d, s0, w, g, b):
    bsz = x2d.shape[0]
    kq, vrows, gate = _whole_call(
        functools.partial(_ret_sample_a_kernel, pos0=float(PAST_LEN)),
        (x2d, w["win"], w["freq"]),
        [(SUBLANES, bsz, HEADS * DK), (SUBLANES, bsz, HEADS * RET_DV), (bsz, HEADS * RET_DV)], "ret_sample_a")
    o8, s_new = _state_step(_ret_sample_b_kernel, _head_rows(kq, DK), (_head_rows(vrows, RET_DV),), s0, 4,
                            "ret_sample_b")
    o = o8[:, :, 0, :].reshape(bsz, HEADS * RET_DV)
    (x1,) = _whole_call(_ret_sample_c_kernel, (o, gate, x2d, w["wo"], g, b), [(bsz, D_MODEL)], "ret_sample_c")
    return x1, s_new


def _lane_row(values, offset):
    return jnp.zeros((1, LANES), F32).at[0, offset:offset + HEADS].set(values.astype(F32))


def _gdn_weights(w_in, conv_w, a_log, dt_bias, norm_g, w_out):
    o1 = GDN_CONV_CH + HEADS * GDN_DV
    wba = jnp.zeros((D_MODEL, LANES), F32).at[:, :2 * HEADS].set(w_in[:, o1:])
    return {"wqkv": w_in[:, :GDN_CONV_CH].astype(BF16), "wz": w_in[:, GDN_CONV_CH:o1].astype(BF16),
            "wba": wba.astype(BF16), "conv_w": conv_w,
            "alog": _lane_row(a_log, HEADS), "dtb": _lane_row(dt_bias, HEADS),
            "ng": norm_g.reshape(1, GDN_DV), "wo": w_out.astype(BF16)}


def _ret_weights(w_in, w_out):
    half = DK // 2
    freqs = ROPE_BASE ** (-jnp.arange(half, dtype=F32) / half)
    return {"win": w_in.astype(BF16), "wo": w_out.astype(BF16),
            "freq": jnp.concatenate([freqs, freqs]).reshape(1, DK)}


def kernel(x_prompt, x_sample, state_pool, state_gdn_conv, state_gdn, state_ret, pool_w, pool_scale, gdn_w_in,
           gdn_conv_w, gdn_a_log, gdn_dt_bias, gdn_norm_g, gdn_w_out, ret_w_in, ret_w_out, ffn_w13, ffn_w2,
           ln_g, ln_b):
    bsz, seq, _ = x_prompt.shape
    dec = x_sample.shape[0]
    tm = min(512, seq)
    xp = x_prompt
    xs = x_sample.reshape(dec, D_MODEL)
    pool_p, pool_s, conv_p, conv_s, gdn_p, gdn_s, ret_p, ret_s = [], [], [], [], [], [], [], []
    for i in range(DEPTH):
        kind, j = i % N_MIXERS, i // N_MIXERS
        g1, b1 = ln_g[i, 0].reshape(1, D_MODEL), ln_b[i, 0].reshape(1, D_MODEL)
        g2, b2 = ln_g[i, 1].reshape(1, D_MODEL), ln_b[i, 1].reshape(1, D_MODEL)
        if kind == 0:
            pw, sc = pool_w[j].astype(BF16), pool_scale[j].reshape(1, D_MODEL)
            pool_p.append(xp[:, seq - POOL_BUF:])
            pool_s.append(jnp.concatenate([state_pool[j][:, 1:], xs[:, None, :]], axis=1))
            xp = _pool_prompt(xp, pw, sc, g1, b1, tm)
            xs = _pool_sample(xs, state_pool[j].transpose(1, 0, 2), pw, sc, g1, b1)
        elif kind == 1:
            w = _gdn_weights(gdn_w_in[j], gdn_conv_w[j], gdn_a_log[j], gdn_dt_bias[j], gdn_norm_g[j], gdn_w_out[j])
            xp, cp, sp = _gdn_prompt(xp, w, g1, b1, tm)
            xs, cs, ss = _gdn_sample(xs, state_gdn_conv[j], state_gdn[j], w, g1, b1)
            conv_p.append(cp), conv_s.append(cs), gdn_p.append(sp), gdn_s.append(ss)
        else:
            w = _ret_weights(ret_w_in[j], ret_w_out[j])
            xp, sp = _ret_prompt(xp, w, g1, b1, tm)
            xs, ss = _ret_sample(xs, state_ret[j], w, g1, b1)
            ret_p.append(sp), ret_s.append(ss)
        w13, w2 = ffn_w13[i].astype(BF16), ffn_w2[i].astype(BF16)
        xp = _ffn(xp.reshape(bsz * seq, D_MODEL), w13, w2, g2, b2, tm).reshape(bsz, seq, D_MODEL)
        xs = _ffn(xs, w13, w2, g2, b2, dec)
    return (xp, xs.reshape(dec, 1, D_MODEL), jnp.stack(pool_p), jnp.stack(pool_s), jnp.stack(conv_p),
            jnp.stack(conv_s), jnp.stack(gdn_p), jnp.stack(gdn_s), jnp.stack(ret_p), jnp.stack(ret_s))
```

```python
import functools
import math

import jax
import jax.numpy as jnp
from jax import lax
from jax.experimental import pallas as pl
from jax.experimental.pallas import tpu as pltpu

F32 = jnp.float32
BF16 = jnp.bfloat16

D_MODEL = 1024
DEPTH = 4
PAST_LEN = 16384
N_MIXERS = 3

POOL_WINDOWS = (2, 4, 8, 16)
POOL_GROUP = D_MODEL // len(POOL_WINDOWS)
POOL_BUF = max(POOL_WINDOWS) - 1
POOL_HALO = 16

HEADS = 8
DK = D_MODEL // HEADS
GDN_DV = DK
GDN_CONV = 4
GDN_CONV_CH = HEADS * (2 * DK + GDN_DV)
GDN_QKVZ = GDN_CONV_CH + HEADS * GDN_DV
CONV_HALO = 8
GDN_CHUNK = 64
HEAD_GROUP = 2
RET_DV = 2 * DK
RET_IN = 2 * HEADS * DK + 2 * HEADS * RET_DV
RET_CHUNK = 256
ROPE_BASE = 10000.0
W_COLS = 1024

D_FF = -(-8 * D_MODEL // (3 * 256)) * 256
FF_CHUNK = 256
DN_ALPHA = (2 * DEPTH) ** 0.25
LN_EPS = 1e-5
RMS_EPS = 1e-6

SUBLANES = 8
LANES = 128
VMEM_LIMIT = 56 * 1024 * 1024


def _params(n_grid):
    return pltpu.CompilerParams(dimension_semantics=("arbitrary",) * n_grid, vmem_limit_bytes=VMEM_LIMIT)


def _resident(shape):
    nd = len(shape)
    return pl.BlockSpec(shape, lambda *_: (0,) * nd, pipeline_mode=pl.Buffered(1))


def _whole(shape):
    nd = len(shape)
    return pl.BlockSpec(shape, lambda *_: (0,) * nd)


def _dot(a, b):
    return jnp.dot(a, b, preferred_element_type=F32)


def _bdot(spec, a, b):
    return jnp.einsum(spec, a, b, preferred_element_type=F32)


def _silu(x):
    return x * jax.nn.sigmoid(x)


def _layer_norm(y, g, b):
    mu = jnp.mean(y, -1, keepdims=True)
    yc = y - mu
    var = jnp.mean(yc * yc, -1, keepdims=True)
    return yc * lax.rsqrt(var + LN_EPS) * g + b


def _f32(*shape):
    return jax.ShapeDtypeStruct(shape, F32)


def _bf16(*shape):
    return jax.ShapeDtypeStruct(shape, BF16)


def _ffn_tail(x, w1_ref, w3_ref, w2_ref, g_ref, b_ref):
    xb = x.astype(BF16)
    acc = None
    for c in range(D_FF // FF_CHUNK):
        lo = c * FF_CHUNK
        a = _dot(xb, w1_ref[:, lo:lo + FF_CHUNK])
        b = _dot(xb, w3_ref[:, lo:lo + FF_CHUNK])
        h = (_silu(a) * b).astype(BF16)
        p = _dot(h, w2_ref[lo:lo + FF_CHUNK, :])
        acc = p if acc is None else acc + p
    return _layer_norm(DN_ALPHA * x + acc, g_ref[...], b_ref[...])


def _ffn_kernel(x_ref, w1_ref, w3_ref, w2_ref, g_ref, b_ref, o_ref):
    o_ref[...] = _ffn_tail(x_ref[...], w1_ref, w3_ref, w2_ref, g_ref, b_ref)


def _ffn(x2d, fw, g, b, tm):
    n = x2d.shape[0]
    assert n % tm == 0
    row = pl.BlockSpec((tm, D_MODEL), lambda i: (i, 0))
    consts = tuple(fw) + (g, b)
    return pl.pallas_call(
        _ffn_kernel,
        grid=(n // tm,),
        in_specs=[row] + [_resident(a.shape) for a in consts],
        out_specs=row,
        out_shape=_f32(n, D_MODEL),
        compiler_params=_params(1),
        name="ffn",
    )(x2d, *consts)


def _ffn_cast_kernel(x_ref, w1_ref, w3_ref, w2_ref, g_ref, b_ref, o_ref, w1b_ref, w3b_ref, w2b_ref, acc_ref):
    c = pl.program_id(0)

    @pl.when(c == 0)
    def _():
        acc_ref[...] = jnp.zeros(acc_ref.shape, F32)

    xb = x_ref[...].astype(BF16)
    w1b, w3b, w2b = w1_ref[...].astype(BF16), w3_ref[...].astype(BF16), w2_ref[...].astype(BF16)
    w1b_ref[...], w3b_ref[...], w2b_ref[...] = w1b, w3b, w2b
    h = (_silu(_dot(xb, w1b)) * _dot(xb, w3b)).astype(BF16)
    acc_ref[...] += _dot(h, w2b)

    @pl.when(c == pl.num_programs(0) - 1)
    def _():
        o_ref[...] = _layer_norm(DN_ALPHA * x_ref[...] + acc_ref[...], g_ref[...], b_ref[...])


def _ffn_cast(x2d, w13_all, w2_all, layer, g, b):
    n = x2d.shape[0]
    nch = D_FF // FF_CHUNK
    return pl.pallas_call(
        _ffn_cast_kernel,
        grid=(nch,),
        in_specs=[_resident(x2d.shape),
                  pl.BlockSpec((None, D_MODEL, FF_CHUNK), lambda c: (layer, 0, c)),
                  pl.BlockSpec((None, D_MODEL, FF_CHUNK), lambda c: (layer, 0, nch + c)),
                  pl.BlockSpec((None, FF_CHUNK, D_MODEL), lambda c: (layer, c, 0)),
                  _resident(g.shape), _resident(b.shape)],
        out_specs=[_whole(x2d.shape),
                   pl.BlockSpec((D_MODEL, FF_CHUNK), lambda c: (0, c)),
                   pl.BlockSpec((D_MODEL, FF_CHUNK), lambda c: (0, c)),
                   pl.BlockSpec((FF_CHUNK, D_MODEL), lambda c: (c, 0))],
        out_shape=[_f32(n, D_MODEL), _bf16(D_MODEL, D_FF), _bf16(D_MODEL, D_FF), _bf16(D_FF, D_MODEL)],
        scratch_shapes=[pltpu.VMEM((n, D_MODEL), F32)],
        compiler_params=_params(1),
        name="ffn_cast",
    )(x2d, w13_all, w13_all, w2_all, g, b)


def _pool_mix(x, shifted, cnt_of, pw, sc_ref):
    outs = []
    for gi, win in enumerate(POOL_WINDOWS):
        lo, hi = gi * POOL_GROUP, (gi + 1) * POOL_GROUP
        xs = x[:, lo:hi]
        s = xs
        for d in range(1, win):
            s = s + shifted(d, lo, hi)
        pooled = s / cnt_of(win) - xs
        outs.append(_dot(pooled.astype(BF16), pw[gi]))
    return jnp.concatenate(outs, axis=-1) * sc_ref[...]


def _pool_ffn_prompt_kernel(x_ref, pw_ref, sc_ref, g1_ref, b1_ref, w1_ref, w3_ref, w2_ref, g2_ref, b2_ref,
                            o_ref, ext_ref, *, tm, n_valid):
    ti = pl.program_id(1)

    @pl.when(ti == 0)
    def _():
        ext_ref[0:POOL_HALO, :] = jnp.zeros((POOL_HALO, D_MODEL), F32)

    x = x_ref[...]
    ext_ref[POOL_HALO:POOL_HALO + tm, :] = x
    t = ti * tm + lax.broadcasted_iota(jnp.int32, (tm, 1), 0)
    mix = _pool_mix(
        x,
        lambda d, lo, hi: ext_ref[pl.ds(POOL_HALO - d, tm), lo:hi],
        lambda win: jnp.minimum(t + 1 + n_valid, win).astype(F32),
        pw_ref, sc_ref)
    x1 = _layer_norm(DN_ALPHA * x + mix, g1_ref[...], b1_ref[...])
    ext_ref[0:POOL_HALO, :] = ext_ref[tm:tm + POOL_HALO, :]
    o_ref[...] = _ffn_tail(x1, w1_ref, w3_ref, w2_ref, g2_ref, b2_ref)


def _pool_ffn_prompt(x, pwb, sc, g1, b1, fw, g2, b2, tm):
    bsz, t, _ = x.shape
    assert t % tm == 0
    blk = pl.BlockSpec((None, tm, D_MODEL), lambda bi, ti: (bi, ti, 0))
    consts = (pwb, sc, g1, b1) + tuple(fw) + (g2, b2)
    return pl.pallas_call(
        functools.partial(_pool_ffn_prompt_kernel, tm=tm, n_valid=0),
        grid=(bsz, t // tm),
        in_specs=[blk] + [_resident(a.shape) for a in consts],
        out_specs=blk,
        out_shape=_f32(*x.shape),
        scratch_shapes=[pltpu.VMEM((tm + POOL_HALO, D_MODEL), F32)],
        compiler_params=_params(2),
        name="pool_ffn_prompt",
    )(x, *consts)


def _pool_sample_kernel(x_ref, buf_ref, pw_ref, sc_ref, g_ref, b_ref, o_ref, pwb_ref, *, n_valid):
    x = x_ref[...]
    pwb = pw_ref[...].astype(BF16)
    pwb_ref[...] = pwb
    mix = _pool_mix(
        x,
        lambda d, lo, hi: buf_ref[POOL_BUF - d, :, lo:hi],
        lambda win: float(min(1 + n_valid, win)),
        pwb, sc_ref)
    o_ref[...] = _layer_norm(DN_ALPHA * x + mix, g_ref[...], b_ref[...])


def _whole_call(kernel, args, outs, name):
    return pl.pallas_call(
        kernel,
        grid=(1,),
        in_specs=[_resident(a.shape) for a in args],
        out_specs=[_whole(o.shape) for o in outs],
        out_shape=list(outs),
        compiler_params=_params(1),
        name=name,
    )(*args)


def _pool_sample(x2d, buf_t, pw, sc, g, b):
    return _whole_call(functools.partial(_pool_sample_kernel, n_valid=min(PAST_LEN, POOL_BUF)),
                       (x2d, buf_t, pw, sc, g, b), [_f32(*x2d.shape), _bf16(*pw.shape)], "pool_sample")


def _softplus(x):
    return jnp.maximum(x, 0.0) + jnp.log1p(jnp.exp(-jnp.abs(x)))


def _gdn_gates(ba, alog_ref, dtb_ref):
    return jax.nn.sigmoid(ba), -jnp.exp(alog_ref[...]) * _softplus(ba + dtb_ref[...])


def _l2norm(v, scale=1.0):
    return v * (lax.rsqrt(jnp.sum(v * v, -1, keepdims=True) + RMS_EPS) * scale)


def _gdn_out(o_heads, z, x, ng_ref, wo, g_ref, b_ref):
    gated = []
    for h, oh in enumerate(o_heads):
        on = oh * lax.rsqrt(jnp.mean(oh * oh, -1, keepdims=True) + RMS_EPS) * ng_ref[...]
        gated.append((on * _silu(z[:, h * GDN_DV:(h + 1) * GDN_DV])).astype(BF16))
    mix = _dot(jnp.concatenate(gated, axis=-1), wo)
    return _layer_norm(DN_ALPHA * x + mix, g_ref[...], b_ref[...])


def _unit_lower_inverse(a):
    c = a.shape[-1]
    ii = lax.broadcasted_iota(jnp.int32, (c, c), 0)
    jj = lax.broadcasted_iota(jnp.int32, (c, c), 1)
    eye = jnp.where(ii == jj, 1.0, 0.0).astype(F32)
    x = eye - jnp.where((ii // 2 == jj // 2) & (ii > jj), a, 0.0)
    s = 4
    while s <= c:
        off = jnp.where((ii // s == jj // s) & (ii // (s // 2) != jj // (s // 2)) & (ii > jj), a, 0.0)
        xb = x.astype(BF16)
        y = _bdot('hij,hjk->hik', off.astype(BF16), xb)
        x = x - _bdot('hij,hjk->hik', xb, y.astype(BF16))
        s *= 2
    return x


def _gdn_prompt_kernel(x_ref, wqkvz_ref, wba_ref, cw_ref, alog_ref, dtb_ref, ng_ref, wo_ref, g_ref, b_ref,
                       o_ref, conv_ref, sout_ref,
                       ext_ref, s_ref, q_s, k_s, kb_s, vbk_s, kg_s, wkqg_s, u_s, at_s, dec_s, gl_s, o_s, *, tm):
    c = GDN_CHUNK
    nc = tm // c
    ti = pl.program_id(1)

    @pl.when(ti == 0)
    def _():
        ext_ref[0:CONV_HALO, :] = jnp.zeros((CONV_HALO, GDN_CONV_CH), F32)
        s_ref[...] = jnp.zeros(s_ref.shape, F32)

    x = x_ref[...]
    xb = x.astype(BF16)
    ext_ref[CONV_HALO:CONV_HALO + tm, :] = _dot(xb, wqkvz_ref[:, 0:GDN_CONV_CH])
    conv_ref[...] = ext_ref[tm + CONV_HALO - (GDN_CONV - 1):tm + CONV_HALO, :]

    def conv_silu(col):
        lo, hi = col * LANES, (col + 1) * LANES
        e = ext_ref[:, lo:hi]
        y = e[CONV_HALO:] * cw_ref[GDN_CONV - 1:GDN_CONV, lo:hi]
        for j in range(GDN_CONV - 1):
            y = y + pltpu.roll(e, GDN_CONV - 1 - j, 0)[CONV_HALO:] * cw_ref[j:j + 1, lo:hi]
        return _silu(y)

    beta_all, gl = _gdn_gates(_dot(xb, wba_ref[...]), alog_ref, dtb_ref)
    pos = lax.broadcasted_iota(jnp.int32, (tm, LANES), 0) % c
    gc = gl
    step = 1
    while step < c:
        gc = gc + jnp.where(pos >= step, pltpu.roll(gc, step, 0), 0.0)
        step *= 2
    gcl = jnp.broadcast_to(gc.reshape(nc, c, LANES)[:, c - 1:c, :], (nc, c, LANES)).reshape(tm, LANES)
    eg = jnp.exp(gc)
    kdec = jnp.exp(gcl - gc)
    glast = jnp.exp(gcl)
    gct = gc.T

    ii = lax.broadcasted_iota(jnp.int32, (c, c), 0)
    jj = lax.broadcasted_iota(jnp.int32, (c, c), 1)
    lower = ii >= jj
    for h in range(HEADS):
        qn = _l2norm(conv_silu(h), DK ** -0.5)
        kn = _l2norm(conv_silu(HEADS + h))
        vh = conv_silu(2 * HEADS + h)
        bh = beta_all[:, h:h + 1]
        egh = eg[:, HEADS + h:HEADS + h + 1]
        kb = kn * bh
        q_s[h] = qn.astype(BF16)
        k_s[h] = kn.astype(BF16)
        kb_s[h] = kb.astype(BF16)
        vbk_s[h, :, 0:GDN_DV] = (vh * bh).astype(BF16)
        vbk_s[h, :, GDN_DV:] = (kb * egh).astype(BF16)
        wkqg_s[h, :, c:, :] = (qn * egh).astype(BF16).reshape(nc, c, DK)
        kg_s[h] = (kn * kdec[:, HEADS + h:HEADS + h + 1]).astype(BF16)
        for n in range(nc):
            col = gc[n * c:(n + 1) * c, HEADS + h:HEADS + h + 1]
            row = gct[HEADS + h:HEADS + h + 1, n * c:(n + 1) * c]
            dec_s[h, n] = jnp.where(lower, jnp.exp(jnp.where(lower, col - row, 0.0)), 0.0)
            gl_s[h, n] = jnp.broadcast_to(glast[n * c:n * c + 1, HEADS + h:HEADS + h + 1], (SUBLANES, LANES))

    ext_ref[0:CONV_HALO, :] = ext_ref[tm:tm + CONV_HALO, :]

    nb = HEAD_GROUP * nc

    def precompute(gi, carry):
        hs = pl.ds(gi * HEAD_GROUP, HEAD_GROUP)
        dec = dec_s[hs].reshape(nb, c, c)
        kn_c = k_s[hs].reshape(nb, c, DK)
        a = jnp.where(ii > jj, _bdot('bid,bjd->bij', kb_s[hs].reshape(nb, c, DK), kn_c) * dec, 0.0)
        tinv = _unit_lower_inverse(a).astype(BF16)
        uw = _bdot('bij,bjd->bid', tinv, vbk_s[hs].reshape(nb, c, 2 * GDN_DV))
        u_s[hs] = uw[:, :, 0:GDN_DV].reshape(HEAD_GROUP, nc, c, GDN_DV)
        wkqg_s[hs, :, 0:c, :] = uw[:, :, GDN_DV:].astype(BF16).reshape(HEAD_GROUP, nc, c, DK)
        attn = _bdot('bid,bjd->bij', q_s[hs].reshape(nb, c, DK), kn_c) * dec
        at_s[hs] = attn.astype(BF16).reshape(HEAD_GROUP, nc, c, c)
        return carry

    lax.fori_loop(0, HEADS // HEAD_GROUP, precompute, 0)

    def chunk(n, carry):
        sl = pl.ds(pl.multiple_of(n * c, c), c)
        s = s_ref[...]
        r = _bdot('hid,hde->hie', wkqg_s[:, n], s.astype(BF16))
        vnb = (u_s[:, n] - r[:, 0:c, :]).astype(BF16)
        o_s[:, sl, :] = r[:, c:, :] + _bdot('hij,hje->hie', at_s[:, n], vnb)
        s_ref[...] = s * gl_s[:, n][:, 0:1, :] + _bdot('hcd,hce->hde', kg_s[:, sl, :], vnb)
        return carry

    lax.fori_loop(0, nc, chunk, 0)

    z = _dot(xb, wqkvz_ref[:, GDN_CONV_CH:])
    o_ref[...] = _gdn_out([o_s[h] for h in range(HEADS)], z, x, ng_ref, wo_ref[...], g_ref, b_ref)

    @pl.when(ti == pl.num_programs(1) - 1)
    def _():
        sout_ref[...] = s_ref[...]


def _gdn_prompt(x, w, g, b, tm):
    bsz, t, _ = x.shape
    assert t % tm == 0 and tm % GDN_CHUNK == 0
    c = GDN_CHUNK
    nc = tm // c
    blk = pl.BlockSpec((None, tm, D_MODEL), lambda bi, ti: (bi, ti, 0))
    consts = (w["wqkvz"], w["wba"], w["conv_w"], w["alog"], w["dtb"], w["ng"], w["wo"], g, b)
    head_bf16 = pltpu.VMEM((HEADS, tm, DK), BF16)
    return pl.pallas_call(
        functools.partial(_gdn_prompt_kernel, tm=tm),
        grid=(bsz, t // tm),
        in_specs=[blk] + [_resident(a.shape) for a in consts],
        out_specs=[blk,
                   pl.BlockSpec((None, GDN_CONV - 1, GDN_CONV_CH), lambda bi, ti: (bi, 0, 0)),
                   pl.BlockSpec((None, HEADS, DK, GDN_DV), lambda bi, ti: (bi, 0, 0, 0))],
        out_shape=[_f32(*x.shape), _f32(bsz, GDN_CONV - 1, GDN_CONV_CH), _f32(bsz, HEADS, DK, GDN_DV)],
        scratch_shapes=[pltpu.VMEM((tm + CONV_HALO, GDN_CONV_CH), F32),
                        pltpu.VMEM((HEADS, DK, GDN_DV), F32),
                        head_bf16, head_bf16, head_bf16,
                        pltpu.VMEM((HEADS, tm, 2 * GDN_DV), BF16),
                        head_bf16,
                        pltpu.VMEM((HEADS, nc, 2 * c, DK), BF16),
                        pltpu.VMEM((HEADS, nc, c, GDN_DV), F32),
                        pltpu.VMEM((HEADS, nc, c, c), BF16),
                        pltpu.VMEM((HEADS, nc, c, c), F32),
                        pltpu.VMEM((HEADS, nc, SUBLANES, LANES), F32),
                        pltpu.VMEM((HEADS, tm, GDN_DV), F32)],
        compiler_params=_params(2),
        name="gdn_prompt",
    )(x, *consts)


def _stream_projection(xb, win_ref, wb_ref, proj_s):
    wb = win_ref[...].astype(BF16)
    wb_ref[...] = wb
    proj_s[pl.program_id(0)] = _dot(xb, wb)


def _gdn_sample_a_kernel(x_ref, cb_ref, win_ref, wba_ref, cw_ref, alog_ref, dtb_ref,
                         rows_ref, pre_ref, z_ref, wb_ref, proj_s):
    xb = x_ref[...].astype(BF16)
    _stream_projection(xb, win_ref, wb_ref, proj_s)

    @pl.when(pl.program_id(0) == pl.num_programs(0) - 1)
    def _():
        nq = GDN_CONV_CH // W_COLS
        pre = jnp.concatenate([proj_s[i] for i in range(nq)], axis=-1)
        pre_ref[...] = pre
        z_ref[...] = proj_s[nq]
        y = pre * cw_ref[GDN_CONV - 1:GDN_CONV, :]
        for j in range(GDN_CONV - 1):
            y = y + cb_ref[j] * cw_ref[j:j + 1, :]
        qkv = _silu(y)
        beta_all, gl = _gdn_gates(_dot(xb, wba_ref[...]), alog_ref, dtb_ref)
        eg = jnp.exp(gl)
        rows = x_ref.shape[0]
        for h in range(HEADS):
            hs = slice(h * DK, (h + 1) * DK)
            qn = _l2norm(qkv[:, hs], DK ** -0.5)
            kn = _l2norm(qkv[:, HEADS * DK + h * DK:HEADS * DK + (h + 1) * DK])
            vh = qkv[:, 2 * HEADS * DK + h * GDN_DV:2 * HEADS * DK + (h + 1) * GDN_DV]
            bh = beta_all[:, h:h + 1]
            egh = eg[:, HEADS + h:HEADS + h + 1]
            rows_ref[0, :, hs] = kn
            rows_ref[1, :, hs] = qn
            rows_ref[2, :, hs] = vh * bh
            rows_ref[3, :, hs] = jnp.broadcast_to(bh * egh, (rows, DK))
            rows_ref[4, :, hs] = jnp.broadcast_to(egh, (rows, DK))
            rows_ref[5, :, hs] = jnp.broadcast_to(jnp.sum(qn * kn, -1, keepdims=True), (rows, DK))
        rows_ref[6:8] = jnp.zeros((2, rows, D_MODEL), F32)


def _gdn_sample_b_kernel(p_ref, s_ref, o_ref, sout_ref):
    bb = p_ref.shape[0]
    n = bb * HEADS
    p = p_ref[...].reshape(n, SUBLANES, DK)
    s = s_ref[...].reshape(n, DK, GDN_DV)
    pb = p.astype(BF16)
    r = _bdot('nqd,nde->nqe', pb, s.astype(BF16))
    v_new = p[:, 2:3, :] - p[:, 3:4, :] * r[:, 0:1, :]
    o = p[:, 4:5, :] * r[:, 1:2, :] + p[:, 5:6, :] * v_new
    row0 = lax.broadcasted_iota(jnp.int32, (n, SUBLANES, GDN_DV), 1) == 0
    vn8 = jnp.where(row0, jnp.broadcast_to(v_new, (n, SUBLANES, GDN_DV)), 0.0)
    s_new = s * p[:, 4:5, :] + _bdot('nqd,nqe->nde', pb, vn8.astype(BF16))
    o_ref[...] = jnp.broadcast_to(o, (n, SUBLANES, GDN_DV)).reshape(bb, HEADS, SUBLANES, GDN_DV)
    sout_ref[...] = s_new.reshape(bb, HEADS, DK, GDN_DV)


def _gdn_sample_c_kernel(o_ref, z_ref, x_ref, ng_ref, wo_ref, g_ref, b_ref, out_ref, wob_ref):
    o = o_ref[...]
    wob = wo_ref[...].astype(BF16)
    wob_ref[...] = wob
    out_ref[...] = _gdn_out([o[:, h * GDN_DV:(h + 1) * GDN_DV] for h in range(HEADS)],
                            z_ref[...], x_ref[...], ng_ref, wob, g_ref, b_ref)


def _projection_call(kernel, resident_args, w_in_all, layer, n_steps, n_cols, outs, rows, name):
    return pl.pallas_call(
        kernel,
        grid=(n_steps,),
        in_specs=[_resident(a.shape) for a in resident_args[:2]]
        + [pl.BlockSpec((None, D_MODEL, W_COLS), lambda c: (layer, 0, c))]
        + [_resident(a.shape) for a in resident_args[2:]],
        out_specs=[_whole(o.shape) for o in outs] + [pl.BlockSpec((D_MODEL, W_COLS), lambda c: (0, c))],
        out_shape=list(outs) + [_bf16(D_MODEL, n_cols)],
        scratch_shapes=[pltpu.VMEM((n_steps, rows, W_COLS), F32)],
        compiler_params=_params(1),
        name=name,
    )(*resident_args[:2], w_in_all, *resident_args[2:])


def _state_step(kernel, p, extra, s0, bb, name):
    bsz = s0.shape[0]
    assert bsz % bb == 0

    def blk(a):
        return pl.BlockSpec((bb,) + a.shape[1:], lambda i: (i,) + (0,) * (a.ndim - 1))

    ins = (p,) + tuple(extra) + (s0,)
    o_shape = (bsz, HEADS, SUBLANES, s0.shape[-1])
    return pl.pallas_call(
        kernel,
        grid=(bsz // bb,),
        in_specs=[blk(a) for a in ins],
        out_specs=[pl.BlockSpec((bb,) + o_shape[1:], lambda i: (i, 0, 0, 0)), blk(s0)],
        out_shape=[_f32(*o_shape), _f32(*s0.shape)],
        compiler_params=_params(1),
        name=name,
    )(*ins)


def _head_rows(rows, width):
    r, bsz, _ = rows.shape
    return rows.reshape(r, bsz, HEADS, width).transpose(1, 2, 0, 3)


def _gdn_sample(x2d, conv_buf, s0, w_in_all, w_out, layer, w, g, b):
    bsz = x2d.shape[0]
    cb_t = conv_buf.transpose(1, 0, 2)
    rows, pre, z, wqkvz = _projection_call(
        _gdn_sample_a_kernel, (x2d, cb_t, w["wba"], w["conv_w"], w["alog"], w["dtb"]), w_in_all, layer,
        GDN_QKVZ // W_COLS, GDN_QKVZ, [_f32(SUBLANES, bsz, D_MODEL), _f32(bsz, GDN_CONV_CH), _f32(bsz, D_MODEL)],
        bsz, "gdn_sample_a")
    o8, s_new = _state_step(_gdn_sample_b_kernel, _head_rows(rows, DK), (), s0, 8, "gdn_sample_b")
    o = o8[:, :, 0, :].reshape(bsz, HEADS * GDN_DV)
    x1, wob = _whole_call(_gdn_sample_c_kernel, (o, z, x2d, w["ng"], w_out, g, b),
                          [_f32(bsz, D_MODEL), _bf16(*w_out.shape)], "gdn_sample_c")
    new_conv = jnp.concatenate([conv_buf[:, 1:], pre[:, None, :]], axis=1)
    return x1, new_conv, s_new, wqkvz, wob


def _log_gamma(h):
    return math.log(1.0 - 2.0 ** (-5.0 - h))


def _rope_tables(pos, freq_ref):
    ang = pos * freq_ref[...]
    lane = lax.broadcasted_iota(jnp.int32, ang.shape, 1)
    return jnp.cos(ang), jnp.where(lane < DK // 2, -1.0, 1.0) * jnp.sin(ang)


def _rope(v, cos2, sin2):
    return v * cos2 + pltpu.roll(v, DK // 2, 1) * sin2


def _ret_out(o_heads, gate, x, wo, g_ref, b_ref):
    gated = []
    for h, oh in enumerate(o_heads):
        mu = jnp.mean(oh, -1, keepdims=True)
        oc = oh - mu
        on = oc * lax.rsqrt(jnp.mean(oc * oc, -1, keepdims=True) + LN_EPS)
        gated.append((_silu(gate[:, h * RET_DV:(h + 1) * RET_DV]) * on).astype(BF16))
    mix = _dot(jnp.concatenate(gated, axis=-1), wo)
    return _layer_norm(DN_ALPHA * x + mix, g_ref[...], b_ref[...])


def _ret_prompt_kernel(x_ref, win_ref, freq_ref, wo_ref, g_ref, b_ref, o_ref, sout_ref,
                       s_ref, q_s, k_s, qx_s, kz_s, v_s, dec_s, o_s, *, tm, c, pos0):
    nc = tm // c
    ti = pl.program_id(1)
    hk = HEADS * DK

    @pl.when(ti == 0)
    def _():
        s_ref[...] = jnp.zeros(s_ref.shape, F32)
        diff = (lax.broadcasted_iota(jnp.int32, (c, c), 0) - lax.broadcasted_iota(jnp.int32, (c, c), 1)).astype(F32)
        for h in range(HEADS):
            dec_s[h] = jnp.where(diff >= 0, jnp.exp(jnp.maximum(diff, 0.0) * _log_gamma(h)), 0.0)

    x = x_ref[...]
    xb = x.astype(BF16)
    row = lax.broadcasted_iota(jnp.int32, (tm, 1), 0)
    cos2, sin2 = _rope_tables((pos0 + ti * tm + row).astype(F32), freq_ref)
    idx = (row % c).astype(F32)
    qall = _dot(xb, win_ref[:, 0:hk])
    kall = _dot(xb, win_ref[:, hk:2 * hk])
    vall = _dot(xb, win_ref[:, 2 * hk:2 * hk + HEADS * RET_DV])
    for h in range(HEADS):
        hs = slice(h * DK, (h + 1) * DK)
        qr = _rope(qall[:, hs], cos2, sin2)
        kr = _rope(kall[:, hs], cos2, sin2) * DK ** -0.5
        q_s[h] = qr.astype(BF16)
        k_s[h] = kr.astype(BF16)
        qx_s[h] = (qr * jnp.exp((idx + 1.0) * _log_gamma(h))).astype(BF16)
        kz_s[h] = (kr * jnp.exp((c - 1.0 - idx) * _log_gamma(h))).astype(BF16)
        v_s[h] = vall[:, h * RET_DV:(h + 1) * RET_DV].astype(BF16)

    for n in range(nc):
        sl = slice(n * c, (n + 1) * c)
        v_c = v_s[:, sl, :]
        attn = _bdot('hid,hjd->hij', q_s[:, sl, :], k_s[:, sl, :]) * dec_s[...]
        s = s_ref[...]
        o_s[:, sl, :] = (_bdot('hij,hje->hie', attn.astype(BF16), v_c)
                         + _bdot('hcd,hde->hce', qx_s[:, sl, :], s.astype(BF16)))
        upd = _bdot('hcd,hce->hde', kz_s[:, sl, :], v_c)
        for h in range(HEADS):
            s_ref[h] = math.exp(c * _log_gamma(h)) * s[h] + upd[h]

    gate = _dot(xb, win_ref[:, 2 * hk + HEADS * RET_DV:])
    o_ref[...] = _ret_out([o_s[h] for h in range(HEADS)], gate, x, wo_ref[...], g_ref, b_ref)

    @pl.when(ti == pl.num_programs(1) - 1)
    def _():
        sout_ref[...] = s_ref[...]


def _ret_prompt(x, w, g, b, tm):
    bsz, t, _ = x.shape
    c = min(RET_CHUNK, tm)
    assert t % tm == 0 and tm % c == 0
    blk = pl.BlockSpec((None, tm, D_MODEL), lambda bi, ti: (bi, ti, 0))
    consts = (w["win"], w["freq"], w["wo"], g, b)
    head_k = pltpu.VMEM((HEADS, tm, DK), BF16)
    return pl.pallas_call(
        functools.partial(_ret_prompt_kernel, tm=tm, c=c, pos0=0),
        grid=(bsz, t // tm),
        in_specs=[blk] + [_resident(a.shape) for a in consts],
        out_specs=[blk, pl.BlockSpec((None, HEADS, DK, RET_DV), lambda bi, ti: (bi, 0, 0, 0))],
        out_shape=[_f32(*x.shape), _f32(bsz, HEADS, DK, RET_DV)],
        scratch_shapes=[pltpu.VMEM((HEADS, DK, RET_DV), F32), head_k, head_k, head_k, head_k,
                        pltpu.VMEM((HEADS, tm, RET_DV), BF16),
                        pltpu.VMEM((HEADS, c, c), F32),
                        pltpu.VMEM((HEADS, tm, RET_DV), F32)],
        compiler_params=_params(2),
        name="ret_prompt",
    )(x, *consts)


def _ret_sample_a_kernel(x_ref, freq_ref, win_ref, kq_ref, vrows_ref, gate_ref, wb_ref, proj_s, *, pos0):
    xb = x_ref[...].astype(BF16)
    _stream_projection(xb, win_ref, wb_ref, proj_s)

    @pl.when(pl.program_id(0) == pl.num_programs(0) - 1)
    def _():
        rows = x_ref.shape[0]
        hk = HEADS * DK
        cos2, sin2 = _rope_tables(jnp.full((1, 1), pos0, F32), freq_ref)
        qall, kall = proj_s[0], proj_s[1]
        vrows_ref[0] = jnp.concatenate([proj_s[2], proj_s[3]], axis=-1)
        gate_ref[...] = jnp.concatenate([proj_s[4], proj_s[5]], axis=-1)
        for h in range(HEADS):
            hs = slice(h * DK, (h + 1) * DK)
            vs = slice(h * RET_DV, (h + 1) * RET_DV)
            qr = _rope(qall[:, hs], cos2, sin2)
            kr = _rope(kall[:, hs], cos2, sin2) * DK ** -0.5
            kq_ref[0, :, hs] = kr
            kq_ref[1, :, hs] = qr
            vrows_ref[1, :, vs] = jnp.broadcast_to(jnp.sum(qr * kr, -1, keepdims=True), (rows, RET_DV))
            vrows_ref[2, :, vs] = jnp.full((rows, RET_DV), math.exp(_log_gamma(h)), F32)
        kq_ref[2:8] = jnp.zeros((6, rows, hk), F32)
        vrows_ref[3:8] = jnp.zeros((5, rows, HEADS * RET_DV), F32)


def _ret_sample_b_kernel(kq_ref, v_ref, s_ref, o_ref, sout_ref):
    bb = kq_ref.shape[0]
    n = bb * HEADS
    kqb = kq_ref[...].reshape(n, SUBLANES, DK).astype(BF16)
    v8 = v_ref[...].reshape(n, SUBLANES, RET_DV)
    s = s_ref[...].reshape(n, DK, RET_DV)
    r = _bdot('nqd,nde->nqe', kqb, s.astype(BF16))
    gamma = v8[:, 2:3, :]
    o = v8[:, 1:2, :] * v8[:, 0:1, :] + gamma * r[:, 1:2, :]
    row0 = lax.broadcasted_iota(jnp.int32, (n, SUBLANES, RET_DV), 1) == 0
    s_new = s * gamma + _bdot('nqd,nqe->nde', kqb, jnp.where(row0, v8, 0.0).astype(BF16))
    o_ref[...] = jnp.broadcast_to(o, (n, SUBLANES, RET_DV)).reshape(bb, HEADS, SUBLANES, RET_DV)
    sout_ref[...] = s_new.reshape(bb, HEADS, DK, RET_DV)


def _ret_sample_c_kernel(o_ref, gate_ref, x_ref, wo_ref, g_ref, b_ref, out_ref, wob_ref):
    o = o_ref[...]
    wob = wo_ref[...].astype(BF16)
    wob_ref[...] = wob
    out_ref[...] = _ret_out([o[:, h * RET_DV:(h + 1) * RET_DV] for h in range(HEADS)],
                            gate_ref[...], x_ref[...], wob, g_ref, b_ref)


def _ret_sample(x2d, s0, w_in_all, w_out, layer, freq, g, b):
    bsz = x2d.shape[0]
    kq, vrows, gate, winb = _projection_call(
        functools.partial(_ret_sample_a_kernel, pos0=float(PAST_LEN)), (x2d, freq), w_in_all, layer,
        RET_IN // W_COLS, RET_IN,
        [_f32(SUBLANES, bsz, HEADS * DK), _f32(SUBLANES, bsz, HEADS * RET_DV), _f32(bsz, HEADS * RET_DV)],
        bsz, "ret_sample_a")
    o8, s_new = _state_step(_ret_sample_b_kernel, _head_rows(kq, DK), (_head_rows(vrows, RET_DV),), s0, 4,
                            "ret_sample_b")
    o = o8[:, :, 0, :].reshape(bsz, HEADS * RET_DV)
    x1, wob = _whole_call(_ret_sample_c_kernel, (o, gate, x2d, w_out, g, b),
                          [_f32(bsz, D_MODEL), _bf16(*w_out.shape)], "ret_sample_c")
    return x1, s_new, winb, wob


def _lane_row(values, offset):
    return jnp.zeros((1, LANES), F32).at[0, offset:offset + HEADS].set(values.astype(F32))


def _gdn_small_weights(w_in, conv_w, a_log, dt_bias, norm_g):
    wba = jnp.zeros((D_MODEL, LANES), F32).at[:, :2 * HEADS].set(w_in[:, GDN_QKVZ:])
    return {"wba": wba.astype(BF16), "conv_w": conv_w,
            "alog": _lane_row(a_log, HEADS), "dtb": _lane_row(dt_bias, HEADS), "ng": norm_g.reshape(1, GDN_DV)}


def _rope_freqs():
    half = DK // 2
    freqs = ROPE_BASE ** (-jnp.arange(half, dtype=F32) / half)
    return jnp.concatenate([freqs, freqs]).reshape(1, DK)


def kernel(x_prompt, x_sample, state_pool, state_gdn_conv, state_gdn, state_ret, pool_w, pool_scale, gdn_w_in,
           gdn_conv_w, gdn_a_log, gdn_dt_bias, gdn_norm_g, gdn_w_out, ret_w_in, ret_w_out, ffn_w13, ffn_w2,
           ln_g, ln_b):
    bsz, seq, _ = x_prompt.shape
    dec = x_sample.shape[0]
    tm = min(512, seq)
    xp = x_prompt
    xs = x_sample.reshape(dec, D_MODEL)
    pool_p, pool_s, conv_p, conv_s, gdn_p, gdn_s, ret_p, ret_s = [], [], [], [], [], [], [], []
    for i in range(DEPTH):
        kind, j = i % N_MIXERS, i // N_MIXERS
        g1, b1 = ln_g[i, 0].reshape(1, D_MODEL), ln_b[i, 0].reshape(1, D_MODEL)
        g2, b2 = ln_g[i, 1].reshape(1, D_MODEL), ln_b[i, 1].reshape(1, D_MODEL)
        if kind == 0:
            sc = pool_scale[j].reshape(1, D_MODEL)
            pool_p.append(xp[:, seq - POOL_BUF:])
            pool_s.append(jnp.concatenate([state_pool[j][:, 1:], xs[:, None, :]], axis=1))
            xs, pwb = _pool_sample(xs, state_pool[j].transpose(1, 0, 2), pool_w[j], sc, g1, b1)
            xs, *fw = _ffn_cast(xs, ffn_w13, ffn_w2, i, g2, b2)
            xp = _pool_ffn_prompt(xp, pwb, sc, g1, b1, fw, g2, b2, tm)
            continue
        if kind == 1:
            w = _gdn_small_weights(gdn_w_in[j], gdn_conv_w[j], gdn_a_log[j], gdn_dt_bias[j], gdn_norm_g[j])
            xs, cs, ss, w["wqkvz"], w["wo"] = _gdn_sample(xs, state_gdn_conv[j], state_gdn[j], gdn_w_in,
                                                         gdn_w_out[j], j, w, g1, b1)
            xp, cp, sp = _gdn_prompt(xp, w, g1, b1, tm)
            conv_p.append(cp), conv_s.append(cs), gdn_p.append(sp), gdn_s.append(ss)
        else:
            w = {"freq": _rope_freqs()}
            xs, ss, w["win"], w["wo"] = _ret_sample(xs, state_ret[j], ret_w_in, ret_w_out[j], j, w["freq"], g1, b1)
            xp, sp = _ret_prompt(xp, w, g1, b1, tm)
            ret_p.append(sp), ret_s.append(ss)
        xs, *fw = _ffn_cast(xs, ffn_w13, ffn_w2, i, g2, b2)
        xp = _ffn(xp.reshape(bsz * seq, D_MODEL), fw, g2, b2, tm).reshape(bsz, seq, D_MODEL)
    return (xp, xs.reshape(dec, 1, D_MODEL), jnp.stack(pool_p), jnp.stack(pool_s), jnp.stack(conv_p),
            jnp.stack(conv_s), jnp.stack(gdn_p), jnp.stack(gdn_s), jnp.stack(ret_p), jnp.stack(ret_s))
```

```python
import functools
import math

import jax
import jax.numpy as jnp
from jax import lax
from jax.experimental import pallas as pl
from jax.experimental.pallas import tpu as pltpu

F32 = jnp.float32
BF16 = jnp.bfloat16

D_MODEL = 1024
DEPTH = 4
PAST_LEN = 16384
N_MIXERS = 3

POOL_WINDOWS = (2, 4, 8, 16)
POOL_GROUP = D_MODEL // len(POOL_WINDOWS)
POOL_BUF = max(POOL_WINDOWS) - 1
POOL_HALO = 16

HEADS = 8
DK = D_MODEL // HEADS
GDN_DV = DK
GDN_CONV = 4
GDN_CONV_CH = HEADS * (2 * DK + GDN_DV)
GDN_QKVZ = GDN_CONV_CH + HEADS * GDN_DV
CONV_HALO = 8
GDN_CHUNK = 64
HEAD_GROUP = 8
RET_DV = 2 * DK
RET_IN = 2 * HEADS * DK + 2 * HEADS * RET_DV
RET_CHUNK = 256
ROPE_BASE = 10000.0
W_COLS = 1024

D_FF = -(-8 * D_MODEL // (3 * 256)) * 256
FF_CHUNK = 256
DN_ALPHA = (2 * DEPTH) ** 0.25
LN_EPS = 1e-5
RMS_EPS = 1e-6

SUBLANES = 8
LANES = 128
VMEM_LIMIT = 56 * 1024 * 1024


def _params(n_grid):
    return pltpu.CompilerParams(dimension_semantics=("arbitrary",) * n_grid, vmem_limit_bytes=VMEM_LIMIT)


def _resident(shape):
    nd = len(shape)
    return pl.BlockSpec(shape, lambda *_: (0,) * nd, pipeline_mode=pl.Buffered(1))


def _whole(shape):
    nd = len(shape)
    return pl.BlockSpec(shape, lambda *_: (0,) * nd)


def _dot(a, b):
    return jnp.dot(a, b, preferred_element_type=F32)


def _bdot(spec, a, b):
    return jnp.einsum(spec, a, b, preferred_element_type=F32)


def _silu(x):
    return x * jax.nn.sigmoid(x)


def _layer_norm(y, g, b):
    mu = jnp.mean(y, -1, keepdims=True)
    yc = y - mu
    var = jnp.mean(yc * yc, -1, keepdims=True)
    return yc * lax.rsqrt(var + LN_EPS) * g + b


def _f32(*shape):
    return jax.ShapeDtypeStruct(shape, F32)


def _bf16(*shape):
    return jax.ShapeDtypeStruct(shape, BF16)


def _ffn_tail(x, w1_ref, w3_ref, w2_ref, g_ref, b_ref):
    xb = x.astype(BF16)
    acc = None
    for c in range(D_FF // FF_CHUNK):
        lo = c * FF_CHUNK
        a = _dot(xb, w1_ref[:, lo:lo + FF_CHUNK])
        b = _dot(xb, w3_ref[:, lo:lo + FF_CHUNK])
        h = (_silu(a) * b).astype(BF16)
        p = _dot(h, w2_ref[lo:lo + FF_CHUNK, :])
        acc = p if acc is None else acc + p
    return _layer_norm(DN_ALPHA * x + acc, g_ref[...], b_ref[...])


def _ffn_kernel(x_ref, w1_ref, w3_ref, w2_ref, g_ref, b_ref, o_ref):
    o_ref[...] = _ffn_tail(x_ref[...], w1_ref, w3_ref, w2_ref, g_ref, b_ref)


def _ffn(x2d, fw, g, b, tm):
    n = x2d.shape[0]
    assert n % tm == 0
    row = pl.BlockSpec((tm, D_MODEL), lambda i: (i, 0))
    consts = tuple(fw) + (g, b)
    return pl.pallas_call(
        _ffn_kernel,
        grid=(n // tm,),
        in_specs=[row] + [_resident(a.shape) for a in consts],
        out_specs=row,
        out_shape=_f32(n, D_MODEL),
        compiler_params=_params(1),
        name="ffn",
    )(x2d, *consts)


def _ffn_cast_kernel(x_ref, w1_ref, w3_ref, w2_ref, g_ref, b_ref, o_ref, w1b_ref, w3b_ref, w2b_ref, acc_ref):
    c = pl.program_id(0)

    @pl.when(c == 0)
    def _():
        acc_ref[...] = jnp.zeros(acc_ref.shape, F32)

    xb = x_ref[...].astype(BF16)
    w1b, w3b, w2b = w1_ref[...].astype(BF16), w3_ref[...].astype(BF16), w2_ref[...].astype(BF16)
    w1b_ref[...], w3b_ref[...], w2b_ref[...] = w1b, w3b, w2b
    h = (_silu(_dot(xb, w1b)) * _dot(xb, w3b)).astype(BF16)
    acc_ref[...] += _dot(h, w2b)

    @pl.when(c == pl.num_programs(0) - 1)
    def _():
        o_ref[...] = _layer_norm(DN_ALPHA * x_ref[...] + acc_ref[...], g_ref[...], b_ref[...])


def _ffn_cast(x2d, w13_all, w2_all, layer, g, b):
    n = x2d.shape[0]
    nch = D_FF // FF_CHUNK
    return pl.pallas_call(
        _ffn_cast_kernel,
        grid=(nch,),
        in_specs=[_resident(x2d.shape),
                  pl.BlockSpec((None, D_MODEL, FF_CHUNK), lambda c: (layer, 0, c)),
                  pl.BlockSpec((None, D_MODEL, FF_CHUNK), lambda c: (layer, 0, nch + c)),
                  pl.BlockSpec((None, FF_CHUNK, D_MODEL), lambda c: (layer, c, 0)),
                  _resident(g.shape), _resident(b.shape)],
        out_specs=[_whole(x2d.shape),
                   pl.BlockSpec((D_MODEL, FF_CHUNK), lambda c: (0, c)),
                   pl.BlockSpec((D_MODEL, FF_CHUNK), lambda c: (0, c)),
                   pl.BlockSpec((FF_CHUNK, D_MODEL), lambda c: (c, 0))],
        out_shape=[_f32(n, D_MODEL), _bf16(D_MODEL, D_FF), _bf16(D_MODEL, D_FF), _bf16(D_FF, D_MODEL)],
        scratch_shapes=[pltpu.VMEM((n, D_MODEL), F32)],
        compiler_params=_params(1),
        name="ffn_cast",
    )(x2d, w13_all, w13_all, w2_all, g, b)


def _pool_mix(x, shifted, cnt_of, pw, sc_ref):
    outs = []
    for gi, win in enumerate(POOL_WINDOWS):
        lo, hi = gi * POOL_GROUP, (gi + 1) * POOL_GROUP
        xs = x[:, lo:hi]
        s = xs
        for d in range(1, win):
            s = s + shifted(d, lo, hi)
        pooled = s / cnt_of(win) - xs
        outs.append(_dot(pooled.astype(BF16), pw[gi]))
    return jnp.concatenate(outs, axis=-1) * sc_ref[...]


def _pool_ffn_prompt_kernel(x_ref, pw_ref, sc_ref, g1_ref, b1_ref, w1_ref, w3_ref, w2_ref, g2_ref, b2_ref,
                            o_ref, ext_ref, *, tm, n_valid):
    ti = pl.program_id(1)

    @pl.when(ti == 0)
    def _():
        ext_ref[0:POOL_HALO, :] = jnp.zeros((POOL_HALO, D_MODEL), F32)

    x = x_ref[...]
    ext_ref[POOL_HALO:POOL_HALO + tm, :] = x
    t = ti * tm + lax.broadcasted_iota(jnp.int32, (tm, 1), 0)
    mix = _pool_mix(
        x,
        lambda d, lo, hi: ext_ref[pl.ds(POOL_HALO - d, tm), lo:hi],
        lambda win: jnp.minimum(t + 1 + n_valid, win).astype(F32),
        pw_ref, sc_ref)
    x1 = _layer_norm(DN_ALPHA * x + mix, g1_ref[...], b1_ref[...])
    ext_ref[0:POOL_HALO, :] = ext_ref[tm:tm + POOL_HALO, :]
    o_ref[...] = _ffn_tail(x1, w1_ref, w3_ref, w2_ref, g2_ref, b2_ref)


def _pool_ffn_prompt(x, pwb, sc, g1, b1, fw, g2, b2, tm):
    bsz, t, _ = x.shape
    assert t % tm == 0
    blk = pl.BlockSpec((None, tm, D_MODEL), lambda bi, ti: (bi, ti, 0))
    consts = (pwb, sc, g1, b1) + tuple(fw) + (g2, b2)
    return pl.pallas_call(
        functools.partial(_pool_ffn_prompt_kernel, tm=tm, n_valid=0),
        grid=(bsz, t // tm),
        in_specs=[blk] + [_resident(a.shape) for a in consts],
        out_specs=blk,
        out_shape=_f32(*x.shape),
        scratch_shapes=[pltpu.VMEM((tm + POOL_HALO, D_MODEL), F32)],
        compiler_params=_params(2),
        name="pool_ffn_prompt",
    )(x, *consts)


def _pool_sample_kernel(x_ref, buf_ref, pw_ref, sc_ref, g_ref, b_ref, o_ref, pwb_ref, *, n_valid):
    x = x_ref[...]
    pwb = pw_ref[...].astype(BF16)
    pwb_ref[...] = pwb
    mix = _pool_mix(
        x,
        lambda d, lo, hi: buf_ref[POOL_BUF - d, :, lo:hi],
        lambda win: float(min(1 + n_valid, win)),
        pwb, sc_ref)
    o_ref[...] = _layer_norm(DN_ALPHA * x + mix, g_ref[...], b_ref[...])


def _whole_call(kernel, args, outs, name):
    return pl.pallas_call(
        kernel,
        grid=(1,),
        in_specs=[_resident(a.shape) for a in args],
        out_specs=[_whole(o.shape) for o in outs],
        out_shape=list(outs),
        compiler_params=_params(1),
        name=name,
    )(*args)


def _pool_sample(x2d, buf_t, pw, sc, g, b):
    return _whole_call(functools.partial(_pool_sample_kernel, n_valid=min(PAST_LEN, POOL_BUF)),
                       (x2d, buf_t, pw, sc, g, b), [_f32(*x2d.shape), _bf16(*pw.shape)], "pool_sample")


def _softplus(x):
    return jnp.maximum(x, 0.0) + jnp.log1p(jnp.exp(-jnp.abs(x)))


def _gdn_gates(ba, alog_ref, dtb_ref):
    return jax.nn.sigmoid(ba), -jnp.exp(alog_ref[...]) * _softplus(ba + dtb_ref[...])


def _l2norm(v, scale=1.0):
    return v * (lax.rsqrt(jnp.sum(v * v, -1, keepdims=True) + RMS_EPS) * scale)


def _gdn_out(o_heads, z, x, ng_ref, wo, g_ref, b_ref):
    gated = []
    for h, oh in enumerate(o_heads):
        on = oh * lax.rsqrt(jnp.mean(oh * oh, -1, keepdims=True) + RMS_EPS) * ng_ref[...]
        gated.append((on * _silu(z[:, h * GDN_DV:(h + 1) * GDN_DV])).astype(BF16))
    mix = _dot(jnp.concatenate(gated, axis=-1), wo)
    return _layer_norm(DN_ALPHA * x + mix, g_ref[...], b_ref[...])


def _pair_index(c):
    ii = lax.broadcasted_iota(jnp.int32, (c, 2 * c), 0)
    lane = lax.broadcasted_iota(jnp.int32, (c, 2 * c), 1)
    return ii, lane % c, lane < c


def _block_diag2(p):
    _, _, left = _pair_index(p.shape[1])
    return jnp.concatenate([jnp.where(left, p, 0.0).astype(BF16), jnp.where(left, 0.0, p).astype(BF16)], axis=1)


def _unit_lower_inverse_pairs(a):
    c = a.shape[1]
    ii, jj, _ = _pair_index(c)
    x = jnp.where(ii == jj, 1.0, 0.0) - jnp.where((ii // 2 == jj // 2) & (ii > jj), a, 0.0)
    s = 4
    while s <= c:
        off = jnp.where((ii // s == jj // s) & (ii // (s // 2) != jj // (s // 2)) & (ii > jj), a, 0.0)
        y = _bdot('bik,bkj->bij', off.astype(BF16), _block_diag2(x))
        x = x - _bdot('bik,bkj->bij', x.astype(BF16), _block_diag2(y))
        s *= 2
    return x


def _gdn_prompt_kernel(x_ref, wqkvz_ref, wba_ref, cw_ref, alog_ref, dtb_ref, ng_ref, wo_ref, g_ref, b_ref,
                       o_ref, conv_ref, sout_ref,
                       ext0, ext1, ext2, ext3, s_ref, q_s, k_s, kb_s, vbk_s, kg_s, wkqg_s, u_s, at_s, dec_s, gl_s,
                       o_s, *, tm):
    c = GDN_CHUNK
    nc = tm // c
    ti = pl.program_id(1)
    exts = (ext0, ext1, ext2, ext3)
    pw = 2 * DK

    @pl.when(ti == 0)
    def _():
        for e in exts:
            e[0:CONV_HALO, :] = jnp.zeros((CONV_HALO, 3 * pw), F32)
        s_ref[...] = jnp.zeros(s_ref.shape, F32)

    x = x_ref[...]
    xb = x.astype(BF16)
    sub = lax.broadcasted_iota(jnp.int32, (c, LANES), 0) % SUBLANES

    def project(p):
        for part in range(3):
            col0 = (part * HEADS + 2 * p) * DK
            exts[p][CONV_HALO:CONV_HALO + tm, part * pw:(part + 1) * pw] = _dot(xb, wqkvz_ref[:, col0:col0 + pw])

    def conv_silu(part, h, n):
        lo = part * pw + (h % 2) * DK
        wl = (part * HEADS + h) * DK
        e = exts[h // 2][n * c:(n + 1) * c + CONV_HALO, lo:lo + DK]
        y = e[CONV_HALO:] * cw_ref[GDN_CONV - 1:GDN_CONV, wl:wl + DK]
        e3 = e.reshape((c + CONV_HALO) // SUBLANES, SUBLANES, LANES)
        for j in range(GDN_CONV - 1):
            back = GDN_CONV - 1 - j
            r = pltpu.roll(e3, back, 1).reshape(c + CONV_HALO, LANES)
            y = y + jnp.where(sub >= back, r[CONV_HALO:], r[0:c]) * cw_ref[j:j + 1, wl:wl + DK]
        return _silu(y)

    beta_all, gl = _gdn_gates(_dot(xb, wba_ref[...]), alog_ref, dtb_ref)
    pos = lax.broadcasted_iota(jnp.int32, (tm, LANES), 0) % c
    gc = gl
    step = 1
    while step < c:
        gc = gc + jnp.where(pos >= step, pltpu.roll(gc, step, 0), 0.0)
        step *= 2
    gcl = jnp.broadcast_to(gc.reshape(nc, c, LANES)[:, c - 1:c, :], (nc, c, LANES)).reshape(tm, LANES)
    eg = jnp.exp(gc)
    kdec = jnp.exp(gcl - gc)
    glast = jnp.exp(gcl)
    gct = gc.T

    ii, jj, left = _pair_index(c)
    lower = ii >= jj
    for h in range(HEADS):
        if h % 2 == 0:
            project(h // 2)
        for n in range(nc):
            rs = slice(n * c, (n + 1) * c)
            qn = _l2norm(conv_silu(0, h, n), DK ** -0.5)
            kn = _l2norm(conv_silu(1, h, n))
            vh = conv_silu(2, h, n)
            bh = beta_all[rs, h:h + 1]
            egh = eg[rs, HEADS + h:HEADS + h + 1]
            kb = kn * bh
            q_s[h, rs] = qn.astype(BF16)
            k_s[h, rs] = kn.astype(BF16)
            kb_s[h, rs] = kb.astype(BF16)
            vbk_s[h, rs, 0:GDN_DV] = (vh * bh).astype(BF16)
            vbk_s[h, rs, GDN_DV:] = (kb * egh).astype(BF16)
            wkqg_s[h, n, c:, :] = (qn * egh).astype(BF16)
            kg_s[h, rs] = (kn * kdec[rs, HEADS + h:HEADS + h + 1]).astype(BF16)
            gl_s[h, n] = jnp.broadcast_to(glast[n * c:n * c + SUBLANES, HEADS + h:HEADS + h + 1], (SUBLANES, LANES))
        for m in range(nc // 2):
            col = jnp.where(left, gc[2 * m * c:(2 * m + 1) * c, HEADS + h:HEADS + h + 1],
                            gc[(2 * m + 1) * c:(2 * m + 2) * c, HEADS + h:HEADS + h + 1])
            row = gct[HEADS + h:HEADS + h + 1, 2 * m * c:(2 * m + 2) * c]
            dec_s[h, m] = jnp.where(lower, jnp.exp(jnp.where(lower, col - row, 0.0)), 0.0)

    for p, e in enumerate(exts):
        for part in range(3):
            col0 = (part * HEADS + 2 * p) * DK
            conv_ref[:, col0:col0 + pw] = e[tm + CONV_HALO - (GDN_CONV - 1):tm + CONV_HALO, part * pw:(part + 1) * pw]
        e[0:CONV_HALO, :] = e[tm:tm + CONV_HALO, :]

    nb = HEAD_GROUP * nc // 2

    def diag_pair(g):
        return jnp.where(left, g[:, 0:c, :], g[:, c:, :])

    def precompute(gi, carry):
        hs = pl.ds(gi * HEAD_GROUP, HEAD_GROUP)
        dec = dec_s[hs].reshape(nb, c, 2 * c)
        kn2 = k_s[hs].reshape(nb, 2 * c, DK)
        a = diag_pair(_bdot('bid,bjd->bij', kb_s[hs].reshape(nb, 2 * c, DK), kn2))
        tinv = _unit_lower_inverse_pairs(jnp.where(ii > jj, a * dec, 0.0))
        uw = _bdot('bik,bkd->bid', _block_diag2(tinv), vbk_s[hs].reshape(nb, 2 * c, 2 * GDN_DV))
        u_s[hs] = uw[:, :, 0:GDN_DV].reshape(HEAD_GROUP, nc, c, GDN_DV)
        wkqg_s[hs, :, 0:c, :] = uw[:, :, GDN_DV:].astype(BF16).reshape(HEAD_GROUP, nc, c, DK)
        attn = diag_pair(_bdot('bid,bjd->bij', q_s[hs].reshape(nb, 2 * c, DK), kn2)) * dec
        at_s[hs] = attn.astype(BF16).reshape(HEAD_GROUP, nc // 2, c, 2 * c)
        return carry

    lax.fori_loop(0, HEADS // HEAD_GROUP, precompute, 0)

    def chunk_pair(m, carry):
        at2 = at_s[:, m]
        for half in range(2):
            n = 2 * m + half
            sl = pl.ds(pl.multiple_of(n * c, c), c)
            s = s_ref[...]
            r = _bdot('hid,hde->hie', wkqg_s[:, n], s.astype(BF16))
            vnb = (u_s[:, n] - r[:, 0:c, :]).astype(BF16)
            o_s[:, sl, :] = r[:, c:, :] + _bdot('hij,hje->hie', at2[:, :, half * c:(half + 1) * c], vnb)
            s_ref[...] = s * gl_s[:, n][:, 0:1, :] + _bdot('hcd,hce->hde', kg_s[:, sl, :], vnb)
        return carry

    lax.fori_loop(0, nc // 2, chunk_pair, 0)

    z = _dot(xb, wqkvz_ref[:, GDN_CONV_CH:])
    o_ref[...] = _gdn_out([o_s[h] for h in range(HEADS)], z, x, ng_ref, wo_ref[...], g_ref, b_ref)

    @pl.when(ti == pl.num_programs(1) - 1)
    def _():
        sout_ref[...] = s_ref[...]


def _gdn_prompt(x, w, g, b, tm):
    bsz, t, _ = x.shape
    assert t % tm == 0 and tm % (2 * GDN_CHUNK) == 0
    c = GDN_CHUNK
    nc = tm // c
    blk = pl.BlockSpec((None, tm, D_MODEL), lambda bi, ti: (bi, ti, 0))
    consts = (w["wqkvz"], w["wba"], w["conv_w"], w["alog"], w["dtb"], w["ng"], w["wo"], g, b)
    head_bf16 = pltpu.VMEM((HEADS, tm, DK), BF16)
    return pl.pallas_call(
        functools.partial(_gdn_prompt_kernel, tm=tm),
        grid=(bsz, t // tm),
        in_specs=[blk] + [_resident(a.shape) for a in consts],
        out_specs=[blk,
                   pl.BlockSpec((None, GDN_CONV - 1, GDN_CONV_CH), lambda bi, ti: (bi, 0, 0)),
                   pl.BlockSpec((None, HEADS, DK, GDN_DV), lambda bi, ti: (bi, 0, 0, 0))],
        out_shape=[_f32(*x.shape), _f32(bsz, GDN_CONV - 1, GDN_CONV_CH), _f32(bsz, HEADS, DK, GDN_DV)],
        scratch_shapes=[pltpu.VMEM((tm + CONV_HALO, 3 * 2 * DK), F32)] * (HEADS // 2)
        + [pltpu.VMEM((HEADS, DK, GDN_DV), F32),
                        head_bf16, head_bf16, head_bf16,
                        pltpu.VMEM((HEADS, tm, 2 * GDN_DV), BF16),
                        head_bf16,
                        pltpu.VMEM((HEADS, nc, 2 * c, DK), BF16),
                        pltpu.VMEM((HEADS, nc, c, GDN_DV), F32),
                        pltpu.VMEM((HEADS, nc // 2, c, 2 * c), BF16),
                        pltpu.VMEM((HEADS, nc // 2, c, 2 * c), F32),
                        pltpu.VMEM((HEADS, nc, SUBLANES, LANES), F32),
                        pltpu.VMEM((HEADS, tm, GDN_DV), F32)],
        compiler_params=_params(2),
        name="gdn_prompt",
    )(x, *consts)


def _stream_projection(xb, win_ref, wb_ref, proj_s):
    wb = win_ref[...].astype(BF16)
    wb_ref[...] = wb
    proj_s[pl.program_id(0)] = _dot(xb, wb)


def _gdn_sample_a_kernel(x_ref, cb_ref, win_ref, wba_ref, cw_ref, alog_ref, dtb_ref,
                         rows_ref, pre_ref, z_ref, wb_ref, proj_s):
    xb = x_ref[...].astype(BF16)
    _stream_projection(xb, win_ref, wb_ref, proj_s)

    @pl.when(pl.program_id(0) == pl.num_programs(0) - 1)
    def _():
        nq = GDN_CONV_CH // W_COLS
        pre = jnp.concatenate([proj_s[i] for i in range(nq)], axis=-1)
        pre_ref[...] = pre
        z_ref[...] = proj_s[nq]
        y = pre * cw_ref[GDN_CONV - 1:GDN_CONV, :]
        for j in range(GDN_CONV - 1):
            y = y + cb_ref[j] * cw_ref[j:j + 1, :]
        qkv = _silu(y)
        beta_all, gl = _gdn_gates(_dot(xb, wba_ref[...]), alog_ref, dtb_ref)
        eg = jnp.exp(gl)
        rows = x_ref.shape[0]
        for h in range(HEADS):
            hs = slice(h * DK, (h + 1) * DK)
            qn = _l2norm(qkv[:, hs], DK ** -0.5)
            kn = _l2norm(qkv[:, HEADS * DK + h * DK:HEADS * DK + (h + 1) * DK])
            vh = qkv[:, 2 * HEADS * DK + h * GDN_DV:2 * HEADS * DK + (h + 1) * GDN_DV]
            bh = beta_all[:, h:h + 1]
            egh = eg[:, HEADS + h:HEADS + h + 1]
            rows_ref[0, :, hs] = kn
            rows_ref[1, :, hs] = qn
            rows_ref[2, :, hs] = vh * bh
            rows_ref[3, :, hs] = jnp.broadcast_to(bh * egh, (rows, DK))
            rows_ref[4, :, hs] = jnp.broadcast_to(egh, (rows, DK))
            rows_ref[5, :, hs] = jnp.broadcast_to(jnp.sum(qn * kn, -1, keepdims=True), (rows, DK))
        rows_ref[6:8] = jnp.zeros((2, rows, D_MODEL), F32)


def _gdn_sample_b_kernel(p_ref, s_ref, o_ref, sout_ref):
    bb = p_ref.shape[0]
    n = bb * HEADS
    p = p_ref[...].reshape(n, SUBLANES, DK)
    s = s_ref[...].reshape(n, DK, GDN_DV)
    pb = p.astype(BF16)
    r = _bdot('nqd,nde->nqe', pb, s.astype(BF16))
    v_new = p[:, 2:3, :] - p[:, 3:4, :] * r[:, 0:1, :]
    o = p[:, 4:5, :] * r[:, 1:2, :] + p[:, 5:6, :] * v_new
    row0 = lax.broadcasted_iota(jnp.int32, (n, SUBLANES, GDN_DV), 1) == 0
    vn8 = jnp.where(row0, jnp.broadcast_to(v_new, (n, SUBLANES, GDN_DV)), 0.0)
    s_new = s * p[:, 4:5, :] + _bdot('nqd,nqe->nde', pb, vn8.astype(BF16))
    o_ref[...] = jnp.broadcast_to(o, (n, SUBLANES, GDN_DV)).reshape(bb, HEADS, SUBLANES, GDN_DV)
    sout_ref[...] = s_new.reshape(bb, HEADS, DK, GDN_DV)


def _gdn_sample_c_kernel(o_ref, z_ref, x_ref, ng_ref, wo_ref, g_ref, b_ref, out_ref, wob_ref):
    o = o_ref[...]
    wob = wo_ref[...].astype(BF16)
    wob_ref[...] = wob
    out_ref[...] = _gdn_out([o[:, h * GDN_DV:(h + 1) * GDN_DV] for h in range(HEADS)],
                            z_ref[...], x_ref[...], ng_ref, wob, g_ref, b_ref)


def _projection_call(kernel, resident_args, w_in_all, layer, n_steps, n_cols, outs, rows, name):
    return pl.pallas_call(
        kernel,
        grid=(n_steps,),
        in_specs=[_resident(a.shape) for a in resident_args[:2]]
        + [pl.BlockSpec((None, D_MODEL, W_COLS), lambda c: (layer, 0, c))]
        + [_resident(a.shape) for a in resident_args[2:]],
        out_specs=[_whole(o.shape) for o in outs] + [pl.BlockSpec((D_MODEL, W_COLS), lambda c: (0, c))],
        out_shape=list(outs) + [_bf16(D_MODEL, n_cols)],
        scratch_shapes=[pltpu.VMEM((n_steps, rows, W_COLS), F32)],
        compiler_params=_params(1),
        name=name,
    )(*resident_args[:2], w_in_all, *resident_args[2:])


def _state_step(kernel, p, extra, s0, bb, name):
    bsz = s0.shape[0]
    assert bsz % bb == 0

    def blk(a):
        return pl.BlockSpec((bb,) + a.shape[1:], lambda i: (i,) + (0,) * (a.ndim - 1))

    ins = (p,) + tuple(extra) + (s0,)
    o_shape = (bsz, HEADS, SUBLANES, s0.shape[-1])
    return pl.pallas_call(
        kernel,
        grid=(bsz // bb,),
        in_specs=[blk(a) for a in ins],
        out_specs=[pl.BlockSpec((bb,) + o_shape[1:], lambda i: (i, 0, 0, 0)), blk(s0)],
        out_shape=[_f32(*o_shape), _f32(*s0.shape)],
        compiler_params=_params(1),
        name=name,
    )(*ins)


def _head_rows(rows, width):
    r, bsz, _ = rows.shape
    return rows.reshape(r, bsz, HEADS, width).transpose(1, 2, 0, 3)


def _gdn_sample(x2d, conv_buf, s0, w_in_all, w_out, layer, w, g, b):
    bsz = x2d.shape[0]
    cb_t = conv_buf.transpose(1, 0, 2)
    rows, pre, z, wqkvz = _projection_call(
        _gdn_sample_a_kernel, (x2d, cb_t, w["wba"], w["conv_w"], w["alog"], w["dtb"]), w_in_all, layer,
        GDN_QKVZ // W_COLS, GDN_QKVZ, [_f32(SUBLANES, bsz, D_MODEL), _f32(bsz, GDN_CONV_CH), _f32(bsz, D_MODEL)],
        bsz, "gdn_sample_a")
    o8, s_new = _state_step(_gdn_sample_b_kernel, _head_rows(rows, DK), (), s0, 8, "gdn_sample_b")
    o = o8[:, :, 0, :].reshape(bsz, HEADS * GDN_DV)
    x1, wob = _whole_call(_gdn_sample_c_kernel, (o, z, x2d, w["ng"], w_out, g, b),
                          [_f32(bsz, D_MODEL), _bf16(*w_out.shape)], "gdn_sample_c")
    new_conv = jnp.concatenate([conv_buf[:, 1:], pre[:, None, :]], axis=1)
    return x1, new_conv, s_new, wqkvz, wob


def _log_gamma(h):
    return math.log(1.0 - 2.0 ** (-5.0 - h))


def _rope_tables(pos, freq_ref):
    ang = pos * freq_ref[...]
    lane = lax.broadcasted_iota(jnp.int32, ang.shape, 1)
    return jnp.cos(ang), jnp.where(lane < DK // 2, -1.0, 1.0) * jnp.sin(ang)


def _rope(v, cos2, sin2):
    return v * cos2 + pltpu.roll(v, DK // 2, 1) * sin2


def _ret_out(o_heads, gate, x, wo, g_ref, b_ref):
    gated = []
    for h, oh in enumerate(o_heads):
        mu = jnp.mean(oh, -1, keepdims=True)
        oc = oh - mu
        on = oc * lax.rsqrt(jnp.mean(oc * oc, -1, keepdims=True) + LN_EPS)
        gated.append((_silu(gate[:, h * RET_DV:(h + 1) * RET_DV]) * on).astype(BF16))
    mix = _dot(jnp.concatenate(gated, axis=-1), wo)
    return _layer_norm(DN_ALPHA * x + mix, g_ref[...], b_ref[...])


def _ret_prompt_kernel(x_ref, win_ref, freq_ref, wo_ref, g_ref, b_ref, o_ref, sout_ref,
                       s_ref, q_s, k_s, qx_s, kz_s, v_s, dec_s, o_s, *, tm, c, pos0):
    nc = tm // c
    ti = pl.program_id(1)
    hk = HEADS * DK

    @pl.when(ti == 0)
    def _():
        s_ref[...] = jnp.zeros(s_ref.shape, F32)
        diff = (lax.broadcasted_iota(jnp.int32, (c, c), 0) - lax.broadcasted_iota(jnp.int32, (c, c), 1)).astype(F32)
        for h in range(HEADS):
            dec_s[h] = jnp.where(diff >= 0, jnp.exp(jnp.maximum(diff, 0.0) * _log_gamma(h)), 0.0)

    x = x_ref[...]
    xb = x.astype(BF16)
    row = lax.broadcasted_iota(jnp.int32, (tm, 1), 0)
    cos2, sin2 = _rope_tables((pos0 + ti * tm + row).astype(F32), freq_ref)
    idx = (row % c).astype(F32)
    qall = _dot(xb, win_ref[:, 0:hk])
    kall = _dot(xb, win_ref[:, hk:2 * hk])
    vall = _dot(xb, win_ref[:, 2 * hk:2 * hk + HEADS * RET_DV])
    for h in range(HEADS):
        hs = slice(h * DK, (h + 1) * DK)
        qr = _rope(qall[:, hs], cos2, sin2)
        kr = _rope(kall[:, hs], cos2, sin2) * DK ** -0.5
        q_s[h] = qr.astype(BF16)
        k_s[h] = kr.astype(BF16)
        qx_s[h] = (qr * jnp.exp((idx + 1.0) * _log_gamma(h))).astype(BF16)
        kz_s[h] = (kr * jnp.exp((c - 1.0 - idx) * _log_gamma(h))).astype(BF16)
        v_s[h] = vall[:, h * RET_DV:(h + 1) * RET_DV].astype(BF16)

    for n in range(nc):
        sl = slice(n * c, (n + 1) * c)
        v_c = v_s[:, sl, :]
        attn = _bdot('hid,hjd->hij', q_s[:, sl, :], k_s[:, sl, :]) * dec_s[...]
        s = s_ref[...]
        o_s[:, sl, :] = (_bdot('hij,hje->hie', attn.astype(BF16), v_c)
                         + _bdot('hcd,hde->hce', qx_s[:, sl, :], s.astype(BF16)))
        upd = _bdot('hcd,hce->hde', kz_s[:, sl, :], v_c)
        for h in range(HEADS):
            s_ref[h] = math.exp(c * _log_gamma(h)) * s[h] + upd[h]

    gate = _dot(xb, win_ref[:, 2 * hk + HEADS * RET_DV:])
    o_ref[...] = _ret_out([o_s[h] for h in range(HEADS)], gate, x, wo_ref[...], g_ref, b_ref)

    @pl.when(ti == pl.num_programs(1) - 1)
    def _():
        sout_ref[...] = s_ref[...]


def _ret_prompt(x, w, g, b, tm):
    bsz, t, _ = x.shape
    c = min(RET_CHUNK, tm)
    assert t % tm == 0 and tm % c == 0
    blk = pl.BlockSpec((None, tm, D_MODEL), lambda bi, ti: (bi, ti, 0))
    consts = (w["win"], w["freq"], w["wo"], g, b)
    head_k = pltpu.VMEM((HEADS, tm, DK), BF16)
    return pl.pallas_call(
        functools.partial(_ret_prompt_kernel, tm=tm, c=c, pos0=0),
        grid=(bsz, t // tm),
        in_specs=[blk] + [_resident(a.shape) for a in consts],
        out_specs=[blk, pl.BlockSpec((None, HEADS, DK, RET_DV), lambda bi, ti: (bi, 0, 0, 0))],
        out_shape=[_f32(*x.shape), _f32(bsz, HEADS, DK, RET_DV)],
        scratch_shapes=[pltpu.VMEM((HEADS, DK, RET_DV), F32), head_k, head_k, head_k, head_k,
                        pltpu.VMEM((HEADS, tm, RET_DV), BF16),
                        pltpu.VMEM((HEADS, c, c), F32),
                        pltpu.VMEM((HEADS, tm, RET_DV), F32)],
        compiler_params=_params(2),
        name="ret_prompt",
    )(x, *consts)


def _ret_sample_a_kernel(x_ref, freq_ref, win_ref, kq_ref, vrows_ref, gate_ref, wb_ref, proj_s, *, pos0):
    xb = x_ref[...].astype(BF16)
    _stream_projection(xb, win_ref, wb_ref, proj_s)

    @pl.when(pl.program_id(0) == pl.num_programs(0) - 1)
    def _():
        rows = x_ref.shape[0]
        hk = HEADS * DK
        cos2, sin2 = _rope_tables(jnp.full((1, 1), pos0, F32), freq_ref)
        qall, kall = proj_s[0], proj_s[1]
        vrows_ref[0] = jnp.concatenate([proj_s[2], proj_s[3]], axis=-1)
        gate_ref[...] = jnp.concatenate([proj_s[4], proj_s[5]], axis=-1)
        for h in range(HEADS):
            hs = slice(h * DK, (h + 1) * DK)
            vs = slice(h * RET_DV, (h + 1) * RET_DV)
            qr = _rope(qall[:, hs], cos2, sin2)
            kr = _rope(kall[:, hs], cos2, sin2) * DK ** -0.5
            kq_ref[0, :, hs] = kr
            kq_ref[1, :, hs] = qr
            vrows_ref[1, :, vs] = jnp.broadcast_to(jnp.sum(qr * kr, -1, keepdims=True), (rows, RET_DV))
            vrows_ref[2, :, vs] = jnp.full((rows, RET_DV), math.exp(_log_gamma(h)), F32)
        kq_ref[2:8] = jnp.zeros((6, rows, hk), F32)
        vrows_ref[3:8] = jnp.zeros((5, rows, HEADS * RET_DV), F32)


def _ret_sample_b_kernel(kq_ref, v_ref, s_ref, o_ref, sout_ref):
    bb = kq_ref.shape[0]
    n = bb * HEADS
    kqb = kq_ref[...].reshape(n, SUBLANES, DK).astype(BF16)
    v8 = v_ref[...].reshape(n, SUBLANES, RET_DV)
    s = s_ref[...].reshape(n, DK, RET_DV)
    r = _bdot('nqd,nde->nqe', kqb, s.astype(BF16))
    gamma = v8[:, 2:3, :]
    o = v8[:, 1:2, :] * v8[:, 0:1, :] + gamma * r[:, 1:2, :]
    row0 = lax.broadcasted_iota(jnp.int32, (n, SUBLANES, RET_DV), 1) == 0
    s_new = s * gamma + _bdot('nqd,nqe->nde', kqb, jnp.where(row0, v8, 0.0).astype(BF16))
    o_ref[...] = jnp.broadcast_to(o, (n, SUBLANES, RET_DV)).reshape(bb, HEADS, SUBLANES, RET_DV)
    sout_ref[...] = s_new.reshape(bb, HEADS, DK, RET_DV)


def _ret_sample_c_kernel(o_ref, gate_ref, x_ref, wo_ref, g_ref, b_ref, out_ref, wob_ref):
    o = o_ref[...]
    wob = wo_ref[...].astype(BF16)
    wob_ref[...] = wob
    out_ref[...] = _ret_out([o[:, h * RET_DV:(h + 1) * RET_DV] for h in range(HEADS)],
                            gate_ref[...], x_ref[...], wob, g_ref, b_ref)


def _ret_sample(x2d, s0, w_in_all, w_out, layer, freq, g, b):
    bsz = x2d.shape[0]
    kq, vrows, gate, winb = _projection_call(
        functools.partial(_ret_sample_a_kernel, pos0=float(PAST_LEN)), (x2d, freq), w_in_all, layer,
        RET_IN // W_COLS, RET_IN,
        [_f32(SUBLANES, bsz, HEADS * DK), _f32(SUBLANES, bsz, HEADS * RET_DV), _f32(bsz, HEADS * RET_DV)],
        bsz, "ret_sample_a")
    o8, s_new = _state_step(_ret_sample_b_kernel, _head_rows(kq, DK), (_head_rows(vrows, RET_DV),), s0, 4,
                            "ret_sample_b")
    o = o8[:, :, 0, :].reshape(bsz, HEADS * RET_DV)
    x1, wob = _whole_call(_ret_sample_c_kernel, (o, gate, x2d, w_out, g, b),
                          [_f32(bsz, D_MODEL), _bf16(*w_out.shape)], "ret_sample_c")
    return x1, s_new, winb, wob


def _lane_row(values, offset):
    return jnp.zeros((1, LANES), F32).at[0, offset:offset + HEADS].set(values.astype(F32))


def _gdn_small_weights(w_in, conv_w, a_log, dt_bias, norm_g):
    wba = jnp.zeros((D_MODEL, LANES), F32).at[:, :2 * HEADS].set(w_in[:, GDN_QKVZ:])
    return {"wba": wba.astype(BF16), "conv_w": conv_w,
            "alog": _lane_row(a_log, HEADS), "dtb": _lane_row(dt_bias, HEADS), "ng": norm_g.reshape(1, GDN_DV)}


def _rope_freqs():
    half = DK // 2
    freqs = ROPE_BASE ** (-jnp.arange(half, dtype=F32) / half)
    return jnp.concatenate([freqs, freqs]).reshape(1, DK)


def kernel(x_prompt, x_sample, state_pool, state_gdn_conv, state_gdn, state_ret, pool_w, pool_scale, gdn_w_in,
           gdn_conv_w, gdn_a_log, gdn_dt_bias, gdn_norm_g, gdn_w_out, ret_w_in, ret_w_out, ffn_w13, ffn_w2,
           ln_g, ln_b):
    bsz, seq, _ = x_prompt.shape
    dec = x_sample.shape[0]
    tm = min(512, seq)
    xp = x_prompt
    xs = x_sample.reshape(dec, D_MODEL)
    pool_p, pool_s, conv_p, conv_s, gdn_p, gdn_s, ret_p, ret_s = [], [], [], [], [], [], [], []
    for i in range(DEPTH):
        kind, j = i % N_MIXERS, i // N_MIXERS
        g1, b1 = ln_g[i, 0].reshape(1, D_MODEL), ln_b[i, 0].reshape(1, D_MODEL)
        g2, b2 = ln_g[i, 1].reshape(1, D_MODEL), ln_b[i, 1].reshape(1, D_MODEL)
        if kind == 0:
            sc = pool_scale[j].reshape(1, D_MODEL)
            pool_p.append(xp[:, seq - POOL_BUF:])
            pool_s.append(jnp.concatenate([state_pool[j][:, 1:], xs[:, None, :]], axis=1))
            xs, pwb = _pool_sample(xs, state_pool[j].transpose(1, 0, 2), pool_w[j], sc, g1, b1)
            xs, *fw = _ffn_cast(xs, ffn_w13, ffn_w2, i, g2, b2)
            xp = _pool_ffn_prompt(xp, pwb, sc, g1, b1, fw, g2, b2, tm)
            continue
        if kind == 1:
            w = _gdn_small_weights(gdn_w_in[j], gdn_conv_w[j], gdn_a_log[j], gdn_dt_bias[j], gdn_norm_g[j])
            xs, cs, ss, w["wqkvz"], w["wo"] = _gdn_sample(xs, state_gdn_conv[j], state_gdn[j], gdn_w_in,
                                                         gdn_w_out[j], j, w, g1, b1)
            xp, cp, sp = _gdn_prompt(xp, w, g1, b1, tm)
            conv_p.append(cp), conv_s.append(cs), gdn_p.append(sp), gdn_s.append(ss)
        else:
            w = {"freq": _rope_freqs()}
            xs, ss, w["win"], w["wo"] = _ret_sample(xs, state_ret[j], ret_w_in, ret_w_out[j], j, w["freq"], g1, b1)
            xp, sp = _ret_prompt(xp, w, g1, b1, tm)
            ret_p.append(sp), ret_s.append(ss)
        xs, *fw = _ffn_cast(xs, ffn_w13, ffn_w2, i, g2, b2)
        xp = _ffn(xp.reshape(bsz * seq, D_MODEL), fw, g2, b2, tm).reshape(bsz, seq, D_MODEL)
    return (xp, xs.reshape(dec, 1, D_MODEL), jnp.stack(pool_p), jnp.stack(pool_s), jnp.stack(conv_p),
            jnp.stack(conv_s), jnp.stack(gdn_p), jnp.stack(gdn_s), jnp.stack(ret_p), jnp.stack(ret_s))
```

```python
import functools
import math

import jax
import jax.numpy as jnp
from jax import lax
from jax.experimental import pallas as pl
from jax.experimental.pallas import tpu as pltpu

F32 = jnp.float32
BF16 = jnp.bfloat16

D_MODEL = 1024
DEPTH = 4
PAST_LEN = 16384
N_MIXERS = 3

POOL_WINDOWS = (2, 4, 8, 16)
POOL_GROUP = D_MODEL // len(POOL_WINDOWS)
POOL_BUF = max(POOL_WINDOWS) - 1
POOL_HALO = 16
POOL_ROWS = 64

HEADS = 8
DK = D_MODEL // HEADS
GDN_DV = DK
GDN_CONV = 4
GDN_CONV_CH = HEADS * (2 * DK + GDN_DV)
GDN_QKVZ = GDN_CONV_CH + HEADS * GDN_DV
CONV_HALO = 8
GDN_CHUNK = 64
HEAD_GROUP = 8
RET_DV = 2 * DK
RET_IN = 2 * HEADS * DK + 2 * HEADS * RET_DV
RET_CHUNK = 256
ROPE_BASE = 10000.0
W_COLS = 1024

D_FF = -(-8 * D_MODEL // (3 * 256)) * 256
FF_CHUNK = 256
DN_ALPHA = (2 * DEPTH) ** 0.25
LN_EPS = 1e-5
RMS_EPS = 1e-6

SUBLANES = 8
LANES = 128
VMEM_LIMIT = 56 * 1024 * 1024


def _params(n_grid):
    return pltpu.CompilerParams(dimension_semantics=("arbitrary",) * n_grid, vmem_limit_bytes=VMEM_LIMIT)


def _resident(shape):
    nd = len(shape)
    return pl.BlockSpec(shape, lambda *_: (0,) * nd, pipeline_mode=pl.Buffered(1))


def _whole(shape):
    nd = len(shape)
    return pl.BlockSpec(shape, lambda *_: (0,) * nd)


def _dot(a, b):
    return jnp.dot(a, b, preferred_element_type=F32)


def _bdot(spec, a, b):
    return jnp.einsum(spec, a, b, preferred_element_type=F32)


def _silu(x):
    return x * jax.nn.sigmoid(x)


def _layer_norm(y, g, b):
    mu = jnp.mean(y, -1, keepdims=True)
    yc = y - mu
    var = jnp.mean(yc * yc, -1, keepdims=True)
    return yc * lax.rsqrt(var + LN_EPS) * g + b


def _f32(*shape):
    return jax.ShapeDtypeStruct(shape, F32)


def _bf16(*shape):
    return jax.ShapeDtypeStruct(shape, BF16)


def _ffn_tail(x, w1_ref, w3_ref, w2_ref, g_ref, b_ref):
    xb = x.astype(BF16)
    acc = None
    for c in range(D_FF // FF_CHUNK):
        lo = c * FF_CHUNK
        a = _dot(xb, w1_ref[:, lo:lo + FF_CHUNK])
        b = _dot(xb, w3_ref[:, lo:lo + FF_CHUNK])
        h = (_silu(a) * b).astype(BF16)
        p = _dot(h, w2_ref[lo:lo + FF_CHUNK, :])
        acc = p if acc is None else acc + p
    return _layer_norm(DN_ALPHA * x + acc, g_ref[...], b_ref[...])


def _ffn_kernel(x_ref, w1_ref, w3_ref, w2_ref, g_ref, b_ref, o_ref):
    o_ref[...] = _ffn_tail(x_ref[...], w1_ref, w3_ref, w2_ref, g_ref, b_ref)


def _ffn(x2d, fw, g, b, tm):
    n = x2d.shape[0]
    assert n % tm == 0
    row = pl.BlockSpec((tm, D_MODEL), lambda i: (i, 0))
    consts = tuple(fw) + (g, b)
    return pl.pallas_call(
        _ffn_kernel,
        grid=(n // tm,),
        in_specs=[row] + [_resident(a.shape) for a in consts],
        out_specs=row,
        out_shape=_f32(n, D_MODEL),
        compiler_params=_params(1),
        name="ffn",
    )(x2d, *consts)


def _ffn_cast_kernel(x_ref, w1_ref, w3_ref, w2_ref, g_ref, b_ref, o_ref, w1b_ref, w3b_ref, w2b_ref, acc_ref):
    c = pl.program_id(0)

    @pl.when(c == 0)
    def _():
        acc_ref[...] = jnp.zeros(acc_ref.shape, F32)

    xb = x_ref[...].astype(BF16)
    w1b, w3b, w2b = w1_ref[...].astype(BF16), w3_ref[...].astype(BF16), w2_ref[...].astype(BF16)
    w1b_ref[...], w3b_ref[...], w2b_ref[...] = w1b, w3b, w2b
    h = (_silu(_dot(xb, w1b)) * _dot(xb, w3b)).astype(BF16)
    acc_ref[...] += _dot(h, w2b)

    @pl.when(c == pl.num_programs(0) - 1)
    def _():
        o_ref[...] = _layer_norm(DN_ALPHA * x_ref[...] + acc_ref[...], g_ref[...], b_ref[...])


def _ffn_cast(x2d, w13_all, w2_all, layer, g, b):
    n = x2d.shape[0]
    nch = D_FF // FF_CHUNK
    return pl.pallas_call(
        _ffn_cast_kernel,
        grid=(nch,),
        in_specs=[_resident(x2d.shape),
                  pl.BlockSpec((None, D_MODEL, FF_CHUNK), lambda c: (layer, 0, c)),
                  pl.BlockSpec((None, D_MODEL, FF_CHUNK), lambda c: (layer, 0, nch + c)),
                  pl.BlockSpec((None, FF_CHUNK, D_MODEL), lambda c: (layer, c, 0)),
                  _resident(g.shape), _resident(b.shape)],
        out_specs=[_whole(x2d.shape),
                   pl.BlockSpec((D_MODEL, FF_CHUNK), lambda c: (0, c)),
                   pl.BlockSpec((D_MODEL, FF_CHUNK), lambda c: (0, c)),
                   pl.BlockSpec((FF_CHUNK, D_MODEL), lambda c: (c, 0))],
        out_shape=[_f32(n, D_MODEL), _bf16(D_MODEL, D_FF), _bf16(D_MODEL, D_FF), _bf16(D_FF, D_MODEL)],
        scratch_shapes=[pltpu.VMEM((n, D_MODEL), F32)],
        compiler_params=_params(1),
        name="ffn_cast",
    )(x2d, w13_all, w13_all, w2_all, g, b)


def _rows_up(v, s):
    rows, w = v.shape
    if s % SUBLANES:
        sub = lax.broadcasted_iota(jnp.int32, v.shape, 0) % SUBLANES
        v = pltpu.roll(v.reshape(rows // SUBLANES, SUBLANES, w), s % SUBLANES, 1).reshape(rows, w)
        v = jnp.where(sub >= s % SUBLANES, v, jnp.concatenate([v[0:SUBLANES], v[0:rows - SUBLANES]], axis=0))
    whole = s - s % SUBLANES
    return jnp.concatenate([v[0:whole], v[0:rows - whole]], axis=0) if whole else v


def _trailing_sums(e, win):
    s, w = e, 1
    while w < win:
        s = s + _rows_up(s, w)
        w *= 2
    return s


def _pool_ffn_prompt_kernel(x_ref, pw_ref, sc_ref, g1_ref, b1_ref, w1_ref, w3_ref, w2_ref, g2_ref, b2_ref,
                            o_ref, ext_ref, pooled_ref, *, tm, n_valid):
    ti = pl.program_id(1)
    rb = POOL_ROWS

    @pl.when(ti == 0)
    def _():
        ext_ref[0:POOL_HALO, :] = jnp.zeros((POOL_HALO, D_MODEL), F32)

    x = x_ref[...]
    ext_ref[POOL_HALO:POOL_HALO + tm, :] = x
    for r0 in range(0, tm, rb):
        t = ti * tm + r0 + lax.broadcasted_iota(jnp.int32, (rb, 1), 0)
        for gi, win in enumerate(POOL_WINDOWS):
            lo, hi = gi * POOL_GROUP, (gi + 1) * POOL_GROUP
            e = ext_ref[r0:r0 + rb + POOL_HALO, lo:hi]
            cnt = jnp.minimum(t + 1 + n_valid, win).astype(F32)
            pooled = _trailing_sums(e, win)[POOL_HALO:] / cnt - e[POOL_HALO:]
            pooled_ref[r0:r0 + rb, lo:hi] = pooled.astype(BF16)
    ext_ref[0:POOL_HALO, :] = ext_ref[tm:tm + POOL_HALO, :]
    mix = jnp.concatenate(
        [_dot(pooled_ref[:, gi * POOL_GROUP:(gi + 1) * POOL_GROUP], pw_ref[gi]) for gi in range(len(POOL_WINDOWS))],
        axis=-1) * sc_ref[...]
    x1 = _layer_norm(DN_ALPHA * x + mix, g1_ref[...], b1_ref[...])
    o_ref[...] = _ffn_tail(x1, w1_ref, w3_ref, w2_ref, g2_ref, b2_ref)


def _pool_ffn_prompt(x, pwb, sc, g1, b1, fw, g2, b2, tm):
    bsz, t, _ = x.shape
    assert t % tm == 0
    blk = pl.BlockSpec((None, tm, D_MODEL), lambda bi, ti: (bi, ti, 0))
    consts = (pwb, sc, g1, b1) + tuple(fw) + (g2, b2)
    return pl.pallas_call(
        functools.partial(_pool_ffn_prompt_kernel, tm=tm, n_valid=0),
        grid=(bsz, t // tm),
        in_specs=[blk] + [_resident(a.shape) for a in consts],
        out_specs=blk,
        out_shape=_f32(*x.shape),
        scratch_shapes=[pltpu.VMEM((tm + POOL_HALO, D_MODEL), F32), pltpu.VMEM((tm, D_MODEL), BF16)],
        compiler_params=_params(2),
        name="pool_ffn_prompt",
    )(x, *consts)


def _pool_sample_kernel(x_ref, buf_ref, pw_ref, sc_ref, g_ref, b_ref, o_ref, pwb_ref, nbuf_ref, *, n_valid):
    x = x_ref[...]
    pwb = pw_ref[...].astype(BF16)
    pwb_ref[...] = pwb
    outs = []
    for gi, win in enumerate(POOL_WINDOWS):
        lo, hi = gi * POOL_GROUP, (gi + 1) * POOL_GROUP
        xs = x[:, lo:hi]
        s = xs
        for d in range(1, win):
            s = s + buf_ref[:, POOL_BUF - d, lo:hi]
        pooled = s / float(min(1 + n_valid, win)) - xs
        outs.append(_dot(pooled.astype(BF16), pwb[gi]))
    mix = jnp.concatenate(outs, axis=-1) * sc_ref[...]
    o_ref[...] = _layer_norm(DN_ALPHA * x + mix, g_ref[...], b_ref[...])
    for r in range(POOL_BUF - 1):
        nbuf_ref[:, r, :] = buf_ref[:, r + 1, :]
    nbuf_ref[:, POOL_BUF - 1, :] = x


def _whole_call(kernel, args, outs, name):
    return pl.pallas_call(
        kernel,
        grid=(1,),
        in_specs=[_resident(a.shape) for a in args],
        out_specs=[_whole(o.shape) for o in outs],
        out_shape=list(outs),
        compiler_params=_params(1),
        name=name,
    )(*args)


def _pool_sample(x2d, buf, pw, sc, g, b):
    return _whole_call(functools.partial(_pool_sample_kernel, n_valid=min(PAST_LEN, POOL_BUF)),
                       (x2d, buf, pw, sc, g, b), [_f32(*x2d.shape), _bf16(*pw.shape), _f32(*buf.shape)],
                       "pool_sample")


def _softplus(x):
    return jnp.maximum(x, 0.0) + jnp.log1p(jnp.exp(-jnp.abs(x)))


def _gdn_gates(ba, alog_ref, dtb_ref):
    return jax.nn.sigmoid(ba), -jnp.exp(alog_ref[...]) * _softplus(ba + dtb_ref[...])


def _l2norm(v, scale=1.0):
    return v * (lax.rsqrt(jnp.sum(v * v, -1, keepdims=True) + RMS_EPS) * scale)


def _gdn_out(o_heads, z, x, ng_ref, wo, g_ref, b_ref):
    gated = []
    for h, oh in enumerate(o_heads):
        on = oh * lax.rsqrt(jnp.mean(oh * oh, -1, keepdims=True) + RMS_EPS) * ng_ref[...]
        gated.append((on * _silu(z[:, h * GDN_DV:(h + 1) * GDN_DV])).astype(BF16))
    mix = _dot(jnp.concatenate(gated, axis=-1), wo)
    return _layer_norm(DN_ALPHA * x + mix, g_ref[...], b_ref[...])


def _pair_index(c):
    ii = lax.broadcasted_iota(jnp.int32, (c, 2 * c), 0)
    lane = lax.broadcasted_iota(jnp.int32, (c, 2 * c), 1)
    return ii, lane % c, lane < c


def _block_diag2(p):
    _, _, left = _pair_index(p.shape[1])
    return jnp.concatenate([jnp.where(left, p, 0.0).astype(BF16), jnp.where(left, 0.0, p).astype(BF16)], axis=1)


def _unit_lower_inverse_pairs(a):
    c = a.shape[1]
    ii, jj, _ = _pair_index(c)
    x = jnp.where(ii == jj, 1.0, 0.0) - jnp.where((ii // 2 == jj // 2) & (ii > jj), a, 0.0)
    s = 4
    while s <= c:
        off = jnp.where((ii // s == jj // s) & (ii // (s // 2) != jj // (s // 2)) & (ii > jj), a, 0.0)
        y = _bdot('bik,bkj->bij', off.astype(BF16), _block_diag2(x))
        x = x - _bdot('bik,bkj->bij', x.astype(BF16), _block_diag2(y))
        s *= 2
    return x


def _gdn_prompt_kernel(x_ref, wqkvz_ref, wba_ref, cw_ref, alog_ref, dtb_ref, ng_ref, wo_ref, g_ref, b_ref,
                       o_ref, conv_ref, sout_ref,
                       ext0, ext1, ext2, ext3, s_ref, q_s, k_s, kb_s, vbk_s, kg_s, wkqg_s, u_s, at_s, dec_s, gl_s,
                       o_s, *, tm):
    c = GDN_CHUNK
    nc = tm // c
    ti = pl.program_id(1)
    exts = (ext0, ext1, ext2, ext3)
    pw = 2 * DK

    @pl.when(ti == 0)
    def _():
        for e in exts:
            e[0:CONV_HALO, :] = jnp.zeros((CONV_HALO, 3 * pw), F32)
        s_ref[...] = jnp.zeros(s_ref.shape, F32)

    x = x_ref[...]
    xb = x.astype(BF16)
    sub = lax.broadcasted_iota(jnp.int32, (c, LANES), 0) % SUBLANES

    def project(p):
        for part in range(3):
            col0 = (part * HEADS + 2 * p) * DK
            exts[p][CONV_HALO:CONV_HALO + tm, part * pw:(part + 1) * pw] = _dot(xb, wqkvz_ref[:, col0:col0 + pw])

    def conv_silu(part, h, n):
        lo = part * pw + (h % 2) * DK
        wl = (part * HEADS + h) * DK
        e = exts[h // 2][n * c:(n + 1) * c + CONV_HALO, lo:lo + DK]
        y = e[CONV_HALO:] * cw_ref[GDN_CONV - 1:GDN_CONV, wl:wl + DK]
        e3 = e.reshape((c + CONV_HALO) // SUBLANES, SUBLANES, LANES)
        for j in range(GDN_CONV - 1):
            back = GDN_CONV - 1 - j
            r = pltpu.roll(e3, back, 1).reshape(c + CONV_HALO, LANES)
            y = y + jnp.where(sub >= back, r[CONV_HALO:], r[0:c]) * cw_ref[j:j + 1, wl:wl + DK]
        return _silu(y)

    beta_all, gl = _gdn_gates(_dot(xb, wba_ref[...]), alog_ref, dtb_ref)
    pos = lax.broadcasted_iota(jnp.int32, (tm, LANES), 0) % c
    gc = gl
    step = 1
    while step < c:
        gc = gc + jnp.where(pos >= step, pltpu.roll(gc, step, 0), 0.0)
        step *= 2
    gcl = jnp.broadcast_to(gc.reshape(nc, c, LANES)[:, c - 1:c, :], (nc, c, LANES)).reshape(tm, LANES)
    eg = jnp.exp(gc)
    kdec = jnp.exp(gcl - gc)
    glast = jnp.exp(gcl)
    gct = gc.T

    ii, jj, left = _pair_index(c)
    lower = ii >= jj
    for h in range(HEADS):
        if h % 2 == 0:
            project(h // 2)
        for n in range(nc):
            rs = slice(n * c, (n + 1) * c)
            qn = _l2norm(conv_silu(0, h, n), DK ** -0.5)
            kn = _l2norm(conv_silu(1, h, n))
            vh = conv_silu(2, h, n)
            bh = beta_all[rs, h:h + 1]
            egh = eg[rs, HEADS + h:HEADS + h + 1]
            kb = kn * bh
            q_s[h, rs] = qn.astype(BF16)
            k_s[h, rs] = kn.astype(BF16)
            kb_s[h, rs] = kb.astype(BF16)
            vbk_s[h, rs, 0:GDN_DV] = (vh * bh).astype(BF16)
            vbk_s[h, rs, GDN_DV:] = (kb * egh).astype(BF16)
            wkqg_s[h, n, c:, :] = (qn * egh).astype(BF16)
            kg_s[h, rs] = (kn * kdec[rs, HEADS + h:HEADS + h + 1]).astype(BF16)
            gl_s[h, n] = jnp.broadcast_to(glast[n * c:n * c + SUBLANES, HEADS + h:HEADS + h + 1], (SUBLANES, LANES))
        for m in range(nc // 2):
            col = jnp.where(left, gc[2 * m * c:(2 * m + 1) * c, HEADS + h:HEADS + h + 1],
                            gc[(2 * m + 1) * c:(2 * m + 2) * c, HEADS + h:HEADS + h + 1])
            row = gct[HEADS + h:HEADS + h + 1, 2 * m * c:(2 * m + 2) * c]
            dec_s[h, m] = jnp.where(lower, jnp.exp(jnp.where(lower, col - row, 0.0)), 0.0)

    for p, e in enumerate(exts):
        for part in range(3):
            col0 = (part * HEADS + 2 * p) * DK
            conv_ref[:, col0:col0 + pw] = e[tm + CONV_HALO - (GDN_CONV - 1):tm + CONV_HALO, part * pw:(part + 1) * pw]
        e[0:CONV_HALO, :] = e[tm:tm + CONV_HALO, :]

    nb = HEAD_GROUP * nc // 2

    def diag_pair(g):
        return jnp.where(left, g[:, 0:c, :], g[:, c:, :])

    def precompute(gi, carry):
        hs = pl.ds(gi * HEAD_GROUP, HEAD_GROUP)
        dec = dec_s[hs].reshape(nb, c, 2 * c)
        kn2 = k_s[hs].reshape(nb, 2 * c, DK)
        a = diag_pair(_bdot('bid,bjd->bij', kb_s[hs].reshape(nb, 2 * c, DK), kn2))
        tinv = _unit_lower_inverse_pairs(jnp.where(ii > jj, a * dec, 0.0))
        uw = _bdot('bik,bkd->bid', _block_diag2(tinv), vbk_s[hs].reshape(nb, 2 * c, 2 * GDN_DV))
        u_s[hs] = uw[:, :, 0:GDN_DV].reshape(HEAD_GROUP, nc, c, GDN_DV)
        wkqg_s[hs, :, 0:c, :] = uw[:, :, GDN_DV:].astype(BF16).reshape(HEAD_GROUP, nc, c, DK)
        attn = diag_pair(_bdot('bid,bjd->bij', q_s[hs].reshape(nb, 2 * c, DK), kn2)) * dec
        at_s[hs] = attn.astype(BF16).reshape(HEAD_GROUP, nc // 2, c, 2 * c)
        return carry

    lax.fori_loop(0, HEADS // HEAD_GROUP, precompute, 0)

    def chunk_pair(m, carry):
        at2 = at_s[:, m]
        for half in range(2):
            n = 2 * m + half
            sl = pl.ds(pl.multiple_of(n * c, c), c)
            s = s_ref[...]
            r = _bdot('hid,hde->hie', wkqg_s[:, n], s.astype(BF16))
            vnb = (u_s[:, n] - r[:, 0:c, :]).astype(BF16)
            o_s[:, sl, :] = r[:, c:, :] + _bdot('hij,hje->hie', at2[:, :, half * c:(half + 1) * c], vnb)
            s_ref[...] = s * gl_s[:, n][:, 0:1, :] + _bdot('hcd,hce->hde', kg_s[:, sl, :], vnb)
        return carry

    lax.fori_loop(0, nc // 2, chunk_pair, 0)

    z = _dot(xb, wqkvz_ref[:, GDN_CONV_CH:])
    o_ref[...] = _gdn_out([o_s[h] for h in range(HEADS)], z, x, ng_ref, wo_ref[...], g_ref, b_ref)

    @pl.when(ti == pl.num_programs(1) - 1)
    def _():
        sout_ref[...] = s_ref[...]


def _gdn_prompt(x, w, g, b, tm):
    bsz, t, _ = x.shape
    assert t % tm == 0 and tm % (2 * GDN_CHUNK) == 0
    c = GDN_CHUNK
    nc = tm // c
    blk = pl.BlockSpec((None, tm, D_MODEL), lambda bi, ti: (bi, ti, 0))
    consts = (w["wqkvz"], w["wba"], w["conv_w"], w["alog"], w["dtb"], w["ng"], w["wo"], g, b)
    head_bf16 = pltpu.VMEM((HEADS, tm, DK), BF16)
    return pl.pallas_call(
        functools.partial(_gdn_prompt_kernel, tm=tm),
        grid=(bsz, t // tm),
        in_specs=[blk] + [_resident(a.shape) for a in consts],
        out_specs=[blk,
                   pl.BlockSpec((None, GDN_CONV - 1, GDN_CONV_CH), lambda bi, ti: (bi, 0, 0)),
                   pl.BlockSpec((None, HEADS, DK, GDN_DV), lambda bi, ti: (bi, 0, 0, 0))],
        out_shape=[_f32(*x.shape), _f32(bsz, GDN_CONV - 1, GDN_CONV_CH), _f32(bsz, HEADS, DK, GDN_DV)],
        scratch_shapes=[pltpu.VMEM((tm + CONV_HALO, 3 * 2 * DK), F32)] * (HEADS // 2)
        + [pltpu.VMEM((HEADS, DK, GDN_DV), F32),
                        head_bf16, head_bf16, head_bf16,
                        pltpu.VMEM((HEADS, tm, 2 * GDN_DV), BF16),
                        head_bf16,
                        pltpu.VMEM((HEADS, nc, 2 * c, DK), BF16),
                        pltpu.VMEM((HEADS, nc, c, GDN_DV), F32),
                        pltpu.VMEM((HEADS, nc // 2, c, 2 * c), BF16),
                        pltpu.VMEM((HEADS, nc // 2, c, 2 * c), F32),
                        pltpu.VMEM((HEADS, nc, SUBLANES, LANES), F32),
                        pltpu.VMEM((HEADS, tm, GDN_DV), F32)],
        compiler_params=_params(2),
        name="gdn_prompt",
    )(x, *consts)


def _stream_projection(xb, win_ref, wb_ref, proj_s):
    wb = win_ref[...].astype(BF16)
    wb_ref[...] = wb
    proj_s[pl.program_id(0)] = _dot(xb, wb)


def _gdn_sample_a_kernel(x_ref, cb_ref, win_ref, wba_ref, cw_ref, alog_ref, dtb_ref,
                         rows_ref, pre_ref, z_ref, wb_ref, proj_s):
    xb = x_ref[...].astype(BF16)
    _stream_projection(xb, win_ref, wb_ref, proj_s)

    @pl.when(pl.program_id(0) == pl.num_programs(0) - 1)
    def _():
        nq = GDN_CONV_CH // W_COLS
        pre = jnp.concatenate([proj_s[i] for i in range(nq)], axis=-1)
        pre_ref[...] = pre
        z_ref[...] = proj_s[nq]
        y = pre * cw_ref[GDN_CONV - 1:GDN_CONV, :]
        for j in range(GDN_CONV - 1):
            y = y + cb_ref[j] * cw_ref[j:j + 1, :]
        qkv = _silu(y)
        beta_all, gl = _gdn_gates(_dot(xb, wba_ref[...]), alog_ref, dtb_ref)
        eg = jnp.exp(gl)
        rows = x_ref.shape[0]
        rows_ref[...] = jnp.zeros(rows_ref.shape, F32)
        for h in range(HEADS):
            hs = slice(h * DK, (h + 1) * DK)
            qn = _l2norm(qkv[:, hs], DK ** -0.5)
            kn = _l2norm(qkv[:, HEADS * DK + h * DK:HEADS * DK + (h + 1) * DK])
            vh = qkv[:, 2 * HEADS * DK + h * GDN_DV:2 * HEADS * DK + (h + 1) * GDN_DV]
            bh = beta_all[:, h:h + 1]
            egh = eg[:, HEADS + h:HEADS + h + 1]
            rows_ref[:, h, 0, :] = kn
            rows_ref[:, h, 1, :] = qn
            rows_ref[:, h, 2, :] = vh * bh
            rows_ref[:, h, 3, :] = jnp.broadcast_to(bh * egh, (rows, DK))
            rows_ref[:, h, 4, :] = jnp.broadcast_to(egh, (rows, DK))
            rows_ref[:, h, 5, :] = jnp.broadcast_to(jnp.sum(qn * kn, -1, keepdims=True), (rows, DK))


def _gdn_sample_b_kernel(p_ref, s_ref, o_ref, sout_ref):
    bb = p_ref.shape[0]
    n = bb * HEADS
    p = p_ref[...].reshape(n, SUBLANES, DK)
    s = s_ref[...].reshape(n, DK, GDN_DV)
    pb = p.astype(BF16)
    r = _bdot('nqd,nde->nqe', pb, s.astype(BF16))
    v_new = p[:, 2:3, :] - p[:, 3:4, :] * r[:, 0:1, :]
    o = p[:, 4:5, :] * r[:, 1:2, :] + p[:, 5:6, :] * v_new
    row0 = lax.broadcasted_iota(jnp.int32, (n, SUBLANES, GDN_DV), 1) == 0
    vn8 = jnp.where(row0, jnp.broadcast_to(v_new, (n, SUBLANES, GDN_DV)), 0.0)
    s_new = s * p[:, 4:5, :] + _bdot('nqd,nqe->nde', pb, vn8.astype(BF16))
    o_ref[...] = jnp.broadcast_to(o, (n, SUBLANES, GDN_DV)).reshape(bb, HEADS, SUBLANES, GDN_DV)
    sout_ref[...] = s_new.reshape(bb, HEADS, DK, GDN_DV)


def _gdn_sample_c_kernel(o_ref, z_ref, x_ref, ng_ref, wo_ref, g_ref, b_ref, out_ref, wob_ref):
    wob = wo_ref[...].astype(BF16)
    wob_ref[...] = wob
    out_ref[...] = _gdn_out([o_ref[:, h, 0, :] for h in range(HEADS)],
                            z_ref[...], x_ref[...], ng_ref, wob, g_ref, b_ref)


def _projection_call(kernel, resident_args, w_in_all, layer, n_steps, n_cols, outs, rows, name):
    return pl.pallas_call(
        kernel,
        grid=(n_steps,),
        in_specs=[_resident(a.shape) for a in resident_args[:2]]
        + [pl.BlockSpec((None, D_MODEL, W_COLS), lambda c: (layer, 0, c))]
        + [_resident(a.shape) for a in resident_args[2:]],
        out_specs=[_whole(o.shape) for o in outs] + [pl.BlockSpec((D_MODEL, W_COLS), lambda c: (0, c))],
        out_shape=list(outs) + [_bf16(D_MODEL, n_cols)],
        scratch_shapes=[pltpu.VMEM((n_steps, rows, W_COLS), F32)],
        compiler_params=_params(1),
        name=name,
    )(*resident_args[:2], w_in_all, *resident_args[2:])


def _state_step(kernel, p, extra, s0, bb, name):
    bsz = s0.shape[0]
    assert bsz % bb == 0

    def blk(a):
        return pl.BlockSpec((bb,) + a.shape[1:], lambda i: (i,) + (0,) * (a.ndim - 1))

    ins = (p,) + tuple(extra) + (s0,)
    o_shape = (bsz, HEADS, SUBLANES, s0.shape[-1])
    return pl.pallas_call(
        kernel,
        grid=(bsz // bb,),
        in_specs=[blk(a) for a in ins],
        out_specs=[pl.BlockSpec((bb,) + o_shape[1:], lambda i: (i, 0, 0, 0)), blk(s0)],
        out_shape=[_f32(*o_shape), _f32(*s0.shape)],
        compiler_params=_params(1),
        name=name,
    )(*ins)


def _gdn_sample(x2d, conv_buf, s0, w_in_all, w_out, layer, w, g, b):
    bsz = x2d.shape[0]
    cb_t = conv_buf.transpose(1, 0, 2)
    rows, pre, z, wqkvz = _projection_call(
        _gdn_sample_a_kernel, (x2d, cb_t, w["wba"], w["conv_w"], w["alog"], w["dtb"]), w_in_all, layer,
        GDN_QKVZ // W_COLS, GDN_QKVZ,
        [_f32(bsz, HEADS, SUBLANES, DK), _f32(bsz, GDN_CONV_CH), _f32(bsz, D_MODEL)], bsz, "gdn_sample_a")
    o8, s_new = _state_step(_gdn_sample_b_kernel, rows, (), s0, 8, "gdn_sample_b")
    x1, wob = _whole_call(_gdn_sample_c_kernel, (o8, z, x2d, w["ng"], w_out, g, b),
                          [_f32(bsz, D_MODEL), _bf16(*w_out.shape)], "gdn_sample_c")
    new_conv = jnp.concatenate([conv_buf[:, 1:], pre[:, None, :]], axis=1)
    return x1, new_conv, s_new, wqkvz, wob


def _log_gamma(h):
    return math.log(1.0 - 2.0 ** (-5.0 - h))


def _rope_tables(pos, freq_ref):
    ang = pos * freq_ref[...]
    lane = lax.broadcasted_iota(jnp.int32, ang.shape, 1)
    return jnp.cos(ang), jnp.where(lane < DK // 2, -1.0, 1.0) * jnp.sin(ang)


def _rope(v, cos2, sin2):
    return v * cos2 + pltpu.roll(v, DK // 2, 1) * sin2


def _ret_out(o_heads, gate, x, wo, g_ref, b_ref):
    gated = []
    for h, oh in enumerate(o_heads):
        mu = jnp.mean(oh, -1, keepdims=True)
        oc = oh - mu
        on = oc * lax.rsqrt(jnp.mean(oc * oc, -1, keepdims=True) + LN_EPS)
        gated.append((_silu(gate[:, h * RET_DV:(h + 1) * RET_DV]) * on).astype(BF16))
    mix = _dot(jnp.concatenate(gated, axis=-1), wo)
    return _layer_norm(DN_ALPHA * x + mix, g_ref[...], b_ref[...])


def _ret_prompt_kernel(x_ref, win_ref, freq_ref, wo_ref, g_ref, b_ref, o_ref, sout_ref,
                       s_ref, q_s, k_s, qx_s, kz_s, v_s, dec_s, o_s, *, tm, c, pos0):
    nc = tm // c
    ti = pl.program_id(1)
    hk = HEADS * DK

    @pl.when(ti == 0)
    def _():
        s_ref[...] = jnp.zeros(s_ref.shape, F32)
        diff = (lax.broadcasted_iota(jnp.int32, (c, c), 0) - lax.broadcasted_iota(jnp.int32, (c, c), 1)).astype(F32)
        for h in range(HEADS):
            dec_s[h] = jnp.where(diff >= 0, jnp.exp(jnp.maximum(diff, 0.0) * _log_gamma(h)), 0.0)

    x = x_ref[...]
    xb = x.astype(BF16)
    row = lax.broadcasted_iota(jnp.int32, (tm, 1), 0)
    cos2, sin2 = _rope_tables((pos0 + ti * tm + row).astype(F32), freq_ref)
    idx = (row % c).astype(F32)
    qall = _dot(xb, win_ref[:, 0:hk])
    kall = _dot(xb, win_ref[:, hk:2 * hk])
    vall = _dot(xb, win_ref[:, 2 * hk:2 * hk + HEADS * RET_DV])
    for h in range(HEADS):
        hs = slice(h * DK, (h + 1) * DK)
        qr = _rope(qall[:, hs], cos2, sin2)
        kr = _rope(kall[:, hs], cos2, sin2) * DK ** -0.5
        q_s[h] = qr.astype(BF16)
        k_s[h] = kr.astype(BF16)
        qx_s[h] = (qr * jnp.exp((idx + 1.0) * _log_gamma(h))).astype(BF16)
        kz_s[h] = (kr * jnp.exp((c - 1.0 - idx) * _log_gamma(h))).astype(BF16)
        v_s[h] = vall[:, h * RET_DV:(h + 1) * RET_DV].astype(BF16)

    for n in range(nc):
        sl = slice(n * c, (n + 1) * c)
        v_c = v_s[:, sl, :]
        attn = _bdot('hid,hjd->hij', q_s[:, sl, :], k_s[:, sl, :]) * dec_s[...]
        s = s_ref[...]
        o_s[:, sl, :] = (_bdot('hij,hje->hie', attn.astype(BF16), v_c)
                         + _bdot('hcd,hde->hce', qx_s[:, sl, :], s.astype(BF16)))
        upd = _bdot('hcd,hce->hde', kz_s[:, sl, :], v_c)
        for h in range(HEADS):
            s_ref[h] = math.exp(c * _log_gamma(h)) * s[h] + upd[h]

    gate = _dot(xb, win_ref[:, 2 * hk + HEADS * RET_DV:])
    o_ref[...] = _ret_out([o_s[h] for h in range(HEADS)], gate, x, wo_ref[...], g_ref, b_ref)

    @pl.when(ti == pl.num_programs(1) - 1)
    def _():
        sout_ref[...] = s_ref[...]


def _ret_prompt(x, w, g, b, tm):
    bsz, t, _ = x.shape
    c = min(RET_CHUNK, tm)
    assert t % tm == 0 and tm % c == 0
    blk = pl.BlockSpec((None, tm, D_MODEL), lambda bi, ti: (bi, ti, 0))
    consts = (w["win"], w["freq"], w["wo"], g, b)
    head_k = pltpu.VMEM((HEADS, tm, DK), BF16)
    return pl.pallas_call(
        functools.partial(_ret_prompt_kernel, tm=tm, c=c, pos0=0),
        grid=(bsz, t // tm),
        in_specs=[blk] + [_resident(a.shape) for a in consts],
        out_specs=[blk, pl.BlockSpec((None, HEADS, DK, RET_DV), lambda bi, ti: (bi, 0, 0, 0))],
        out_shape=[_f32(*x.shape), _f32(bsz, HEADS, DK, RET_DV)],
        scratch_shapes=[pltpu.VMEM((HEADS, DK, RET_DV), F32), head_k, head_k, head_k, head_k,
                        pltpu.VMEM((HEADS, tm, RET_DV), BF16),
                        pltpu.VMEM((HEADS, c, c), F32),
                        pltpu.VMEM((HEADS, tm, RET_DV), F32)],
        compiler_params=_params(2),
        name="ret_prompt",
    )(x, *consts)


def _ret_sample_a_kernel(x_ref, freq_ref, win_ref, kq_ref, vrows_ref, gate_ref, wb_ref, proj_s, *, pos0):
    xb = x_ref[...].astype(BF16)
    _stream_projection(xb, win_ref, wb_ref, proj_s)

    @pl.when(pl.program_id(0) == pl.num_programs(0) - 1)
    def _():
        rows = x_ref.shape[0]
        cos2, sin2 = _rope_tables(jnp.full((1, 1), pos0, F32), freq_ref)
        qall, kall = proj_s[0], proj_s[1]
        gate_ref[...] = jnp.concatenate([proj_s[4], proj_s[5]], axis=-1)
        kq_ref[...] = jnp.zeros(kq_ref.shape, F32)
        vrows_ref[...] = jnp.zeros(vrows_ref.shape, F32)
        for h in range(HEADS):
            hs = slice(h * DK, (h + 1) * DK)
            qr = _rope(qall[:, hs], cos2, sin2)
            kr = _rope(kall[:, hs], cos2, sin2) * DK ** -0.5
            kq_ref[:, h, 0, :] = kr
            kq_ref[:, h, 1, :] = qr
            vh = proj_s[2 + h // 4][:, (h % 4) * RET_DV:(h % 4 + 1) * RET_DV]
            vrows_ref[:, h, 0, :] = vh
            vrows_ref[:, h, 1, :] = jnp.broadcast_to(jnp.sum(qr * kr, -1, keepdims=True), (rows, RET_DV))
            vrows_ref[:, h, 2, :] = jnp.full((rows, RET_DV), math.exp(_log_gamma(h)), F32)


def _ret_sample_b_kernel(kq_ref, v_ref, s_ref, o_ref, sout_ref):
    bb = kq_ref.shape[0]
    n = bb * HEADS
    kqb = kq_ref[...].reshape(n, SUBLANES, DK).astype(BF16)
    v8 = v_ref[...].reshape(n, SUBLANES, RET_DV)
    s = s_ref[...].reshape(n, DK, RET_DV)
    r = _bdot('nqd,nde->nqe', kqb, s.astype(BF16))
    gamma = v8[:, 2:3, :]
    o = v8[:, 1:2, :] * v8[:, 0:1, :] + gamma * r[:, 1:2, :]
    row0 = lax.broadcasted_iota(jnp.int32, (n, SUBLANES, RET_DV), 1) == 0
    s_new = s * gamma + _bdot('nqd,nqe->nde', kqb, jnp.where(row0, v8, 0.0).astype(BF16))
    o_ref[...] = jnp.broadcast_to(o, (n, SUBLANES, RET_DV)).reshape(bb, HEADS, SUBLANES, RET_DV)
    sout_ref[...] = s_new.reshape(bb, HEADS, DK, RET_DV)


def _ret_sample_c_kernel(o_ref, gate_ref, x_ref, wo_ref, g_ref, b_ref, out_ref, wob_ref):
    wob = wo_ref[...].astype(BF16)
    wob_ref[...] = wob
    out_ref[...] = _ret_out([o_ref[:, h, 0, :] for h in range(HEADS)],
                            gate_ref[...], x_ref[...], wob, g_ref, b_ref)


def _ret_sample(x2d, s0, w_in_all, w_out, layer, freq, g, b):
    bsz = x2d.shape[0]
    kq, vrows, gate, winb = _projection_call(
        functools.partial(_ret_sample_a_kernel, pos0=float(PAST_LEN)), (x2d, freq), w_in_all, layer,
        RET_IN // W_COLS, RET_IN,
        [_f32(bsz, HEADS, SUBLANES, DK), _f32(bsz, HEADS, SUBLANES, RET_DV), _f32(bsz, HEADS * RET_DV)],
        bsz, "ret_sample_a")
    o8, s_new = _state_step(_ret_sample_b_kernel, kq, (vrows,), s0, 4, "ret_sample_b")
    x1, wob = _whole_call(_ret_sample_c_kernel, (o8, gate, x2d, w_out, g, b),
                          [_f32(bsz, D_MODEL), _bf16(*w_out.shape)], "ret_sample_c")
    return x1, s_new, winb, wob


def _lane_row(values, offset):
    return jnp.zeros((1, LANES), F32).at[0, offset:offset + HEADS].set(values.astype(F32))


def _gdn_small_weights(w_in, conv_w, a_log, dt_bias, norm_g):
    wba = jnp.zeros((D_MODEL, LANES), F32).at[:, :2 * HEADS].set(w_in[:, GDN_QKVZ:])
    return {"wba": wba.astype(BF16), "conv_w": conv_w,
            "alog": _lane_row(a_log, HEADS), "dtb": _lane_row(dt_bias, HEADS), "ng": norm_g.reshape(1, GDN_DV)}


def _rope_freqs():
    half = DK // 2
    freqs = ROPE_BASE ** (-jnp.arange(half, dtype=F32) / half)
    return jnp.concatenate([freqs, freqs]).reshape(1, DK)


def kernel(x_prompt, x_sample, state_pool, state_gdn_conv, state_gdn, state_ret, pool_w, pool_scale, gdn_w_in,
           gdn_conv_w, gdn_a_log, gdn_dt_bias, gdn_norm_g, gdn_w_out, ret_w_in, ret_w_out, ffn_w13, ffn_w2,
           ln_g, ln_b):
    bsz, seq, _ = x_prompt.shape
    dec = x_sample.shape[0]
    tm = min(512, seq)
    xp = x_prompt
    xs = x_sample.reshape(dec, D_MODEL)
    pool_p, pool_s, conv_p, conv_s, gdn_p, gdn_s, ret_p, ret_s = [], [], [], [], [], [], [], []
    for i in range(DEPTH):
        kind, j = i % N_MIXERS, i // N_MIXERS
        g1, b1 = ln_g[i, 0].reshape(1, D_MODEL), ln_b[i, 0].reshape(1, D_MODEL)
        g2, b2 = ln_g[i, 1].reshape(1, D_MODEL), ln_b[i, 1].reshape(1, D_MODEL)
        if kind == 0:
            sc = pool_scale[j].reshape(1, D_MODEL)
            pool_p.append(xp[:, seq - POOL_BUF:])
            xs, pwb, nbuf = _pool_sample(xs, state_pool[j], pool_w[j], sc, g1, b1)
            pool_s.append(nbuf)
            xs, *fw = _ffn_cast(xs, ffn_w13, ffn_w2, i, g2, b2)
            xp = _pool_ffn_prompt(xp, pwb, sc, g1, b1, fw, g2, b2, tm)
            continue
        if kind == 1:
            w = _gdn_small_weights(gdn_w_in[j], gdn_conv_w[j], gdn_a_log[j], gdn_dt_bias[j], gdn_norm_g[j])
            xs, cs, ss, w["wqkvz"], w["wo"] = _gdn_sample(xs, state_gdn_conv[j], state_gdn[j], gdn_w_in,
                                                         gdn_w_out[j], j, w, g1, b1)
            xp, cp, sp = _gdn_prompt(xp, w, g1, b1, tm)
            conv_p.append(cp), conv_s.append(cs), gdn_p.append(sp), gdn_s.append(ss)
        else:
            w = {"freq": _rope_freqs()}
            xs, ss, w["win"], w["wo"] = _ret_sample(xs, state_ret[j], ret_w_in, ret_w_out[j], j, w["freq"], g1, b1)
            xp, sp = _ret_prompt(xp, w, g1, b1, tm)
            ret_p.append(sp), ret_s.append(ss)
        xs, *fw = _ffn_cast(xs, ffn_w13, ffn_w2, i, g2, b2)
        xp = _ffn(xp.reshape(bsz * seq, D_MODEL), fw, g2, b2, tm).reshape(bsz, seq, D_MODEL)
    return (xp, xs.reshape(dec, 1, D_MODEL), jnp.stack(pool_p), jnp.stack(pool_s), jnp.stack(conv_p),
            jnp.stack(conv_s), jnp.stack(gdn_p), jnp.stack(gdn_s), jnp.stack(ret_p), jnp.stack(ret_s))
```

```python
import functools
import math

import jax
import jax.numpy as jnp
from jax import lax
from jax.experimental import pallas as pl
from jax.experimental.pallas import tpu as pltpu

F32 = jnp.float32
BF16 = jnp.bfloat16

D_MODEL = 1024
DEPTH = 4
PAST_LEN = 16384
N_MIXERS = 3

POOL_WINDOWS = (2, 4, 8, 16)
POOL_GROUP = D_MODEL // len(POOL_WINDOWS)
POOL_BUF = max(POOL_WINDOWS) - 1
POOL_HALO = 16
POOL_ROWS = 64

HEADS = 8
DK = D_MODEL // HEADS
GDN_DV = DK
GDN_CONV = 4
GDN_CONV_CH = HEADS * (2 * DK + GDN_DV)
GDN_QKVZ = GDN_CONV_CH + HEADS * GDN_DV
CONV_HALO = 8
GDN_CHUNK = 64
HEAD_GROUP = 8
RET_DV = 2 * DK
RET_IN = 2 * HEADS * DK + 2 * HEADS * RET_DV
RET_CHUNK = 256
ROPE_BASE = 10000.0
W_COLS = 1024

D_FF = -(-8 * D_MODEL // (3 * 256)) * 256
FF_CHUNK = 256
CAST_CHUNK = 256
DN_ALPHA = (2 * DEPTH) ** 0.25
LN_EPS = 1e-5
RMS_EPS = 1e-6

SUBLANES = 8
LANES = 128
VMEM_LIMIT = 56 * 1024 * 1024


def _params(n_grid):
    return pltpu.CompilerParams(dimension_semantics=("arbitrary",) * n_grid, vmem_limit_bytes=VMEM_LIMIT)


def _resident(shape):
    nd = len(shape)
    return pl.BlockSpec(shape, lambda *_: (0,) * nd, pipeline_mode=pl.Buffered(1))


def _whole(shape):
    nd = len(shape)
    return pl.BlockSpec(shape, lambda *_: (0,) * nd)


def _dot(a, b):
    return jnp.dot(a, b, preferred_element_type=F32)


def _bdot(spec, a, b):
    return jnp.einsum(spec, a, b, preferred_element_type=F32)


def _silu(x):
    return x * jax.nn.sigmoid(x)


def _layer_norm(y, g, b):
    mu = jnp.mean(y, -1, keepdims=True)
    yc = y - mu
    var = jnp.mean(yc * yc, -1, keepdims=True)
    return yc * lax.rsqrt(var + LN_EPS) * g + b


def _f32(*shape):
    return jax.ShapeDtypeStruct(shape, F32)


def _bf16(*shape):
    return jax.ShapeDtypeStruct(shape, BF16)


def _ffn_tail(x, w1_ref, w3_ref, w2_ref, g_ref, b_ref):
    xb = x.astype(BF16)
    acc = None
    for c in range(D_FF // FF_CHUNK):
        lo = c * FF_CHUNK
        a = _dot(xb, w1_ref[:, lo:lo + FF_CHUNK])
        b = _dot(xb, w3_ref[:, lo:lo + FF_CHUNK])
        h = (_silu(a) * b).astype(BF16)
        p = _dot(h, w2_ref[lo:lo + FF_CHUNK, :])
        acc = p if acc is None else acc + p
    return _layer_norm(DN_ALPHA * x + acc, g_ref[...], b_ref[...])


def _ffn_kernel(x_ref, w1_ref, w3_ref, w2_ref, g_ref, b_ref, o_ref, *, sub):
    for r0 in range(0, x_ref.shape[0], sub):
        o_ref[r0:r0 + sub, :] = _ffn_tail(x_ref[r0:r0 + sub, :], w1_ref, w3_ref, w2_ref, g_ref, b_ref)


def _ffn(x2d, fw, g, b, tm, sub):
    n = x2d.shape[0]
    assert n % tm == 0 and tm % sub == 0
    row = pl.BlockSpec((tm, D_MODEL), lambda i: (i, 0))
    consts = tuple(fw) + (g, b)
    return pl.pallas_call(
        functools.partial(_ffn_kernel, sub=sub),
        grid=(n // tm,),
        in_specs=[row] + [_resident(a.shape) for a in consts],
        out_specs=row,
        out_shape=_f32(n, D_MODEL),
        compiler_params=_params(1),
        name="ffn",
    )(x2d, *consts)


def _ffn_cast_kernel(x_ref, w1_ref, w3_ref, w2_ref, g_ref, b_ref, o_ref, w1b_ref, w3b_ref, w2b_ref, acc_ref):
    c = pl.program_id(0)

    @pl.when(c == 0)
    def _():
        acc_ref[...] = jnp.zeros(acc_ref.shape, F32)

    xb = x_ref[...].astype(BF16)
    w1b, w3b, w2b = w1_ref[...].astype(BF16), w3_ref[...].astype(BF16), w2_ref[...].astype(BF16)
    w1b_ref[...], w3b_ref[...], w2b_ref[...] = w1b, w3b, w2b
    h = (_silu(_dot(xb, w1b)) * _dot(xb, w3b)).astype(BF16)
    acc_ref[...] += _dot(h, w2b)

    @pl.when(c == pl.num_programs(0) - 1)
    def _():
        o_ref[...] = _layer_norm(DN_ALPHA * x_ref[...] + acc_ref[...], g_ref[...], b_ref[...])


def _ffn_cast(x2d, w13_all, w2_all, layer, g, b):
    n = x2d.shape[0]
    cc = CAST_CHUNK
    nch = D_FF // cc
    return pl.pallas_call(
        _ffn_cast_kernel,
        grid=(nch,),
        in_specs=[_resident(x2d.shape),
                  pl.BlockSpec((None, D_MODEL, cc), lambda c: (layer, 0, c)),
                  pl.BlockSpec((None, D_MODEL, cc), lambda c: (layer, 0, nch + c)),
                  pl.BlockSpec((None, cc, D_MODEL), lambda c: (layer, c, 0)),
                  _resident(g.shape), _resident(b.shape)],
        out_specs=[_whole(x2d.shape),
                   pl.BlockSpec((D_MODEL, cc), lambda c: (0, c)),
                   pl.BlockSpec((D_MODEL, cc), lambda c: (0, c)),
                   pl.BlockSpec((cc, D_MODEL), lambda c: (c, 0))],
        out_shape=[_f32(n, D_MODEL), _bf16(D_MODEL, D_FF), _bf16(D_MODEL, D_FF), _bf16(D_FF, D_MODEL)],
        scratch_shapes=[pltpu.VMEM((n, D_MODEL), F32)],
        compiler_params=_params(1),
        name="ffn_cast",
    )(x2d, w13_all, w13_all, w2_all, g, b)


def _rows_up(v, s):
    rows, w = v.shape
    if s % SUBLANES:
        sub = lax.broadcasted_iota(jnp.int32, v.shape, 0) % SUBLANES
        v = pltpu.roll(v.reshape(rows // SUBLANES, SUBLANES, w), s % SUBLANES, 1).reshape(rows, w)
        v = jnp.where(sub >= s % SUBLANES, v, jnp.concatenate([v[0:SUBLANES], v[0:rows - SUBLANES]], axis=0))
    whole = s - s % SUBLANES
    return jnp.concatenate([v[0:whole], v[0:rows - whole]], axis=0) if whole else v


def _trailing_sums(e, win):
    s, w = e, 1
    while w < win:
        s = s + _rows_up(s, w)
        w *= 2
    return s


def _pool_ffn_prompt_kernel(x_ref, pw_ref, sc_ref, g1_ref, b1_ref, w1_ref, w3_ref, w2_ref, g2_ref, b2_ref,
                            o_ref, ext_ref, pooled_ref, *, tm, sub, n_valid):
    ti = pl.program_id(1)
    rb = POOL_ROWS

    @pl.when(ti == 0)
    def _():
        ext_ref[0:POOL_HALO, :] = jnp.zeros((POOL_HALO, D_MODEL), F32)

    ext_ref[POOL_HALO:POOL_HALO + tm, :] = x_ref[...]

    def pool_rows(r0):
        t = ti * tm + r0 + lax.broadcasted_iota(jnp.int32, (rb, 1), 0)
        for gi, win in enumerate(POOL_WINDOWS):
            lo, hi = gi * POOL_GROUP, (gi + 1) * POOL_GROUP
            e = ext_ref[r0:r0 + rb + POOL_HALO, lo:hi]
            cnt = jnp.minimum(t + 1 + n_valid, win).astype(F32)
            pooled = _trailing_sums(e, win)[POOL_HALO:] / cnt - e[POOL_HALO:]
            pooled_ref[r0:r0 + rb, lo:hi] = pooled.astype(BF16)

    for r0 in range(0, tm, rb):
        pool_rows(r0)
    ext_ref[0:POOL_HALO, :] = ext_ref[tm:tm + POOL_HALO, :]
    for s0 in range(0, tm, sub):
        mix = jnp.concatenate(
            [_dot(pooled_ref[s0:s0 + sub, gi * POOL_GROUP:(gi + 1) * POOL_GROUP], pw_ref[gi])
             for gi in range(len(POOL_WINDOWS))], axis=-1) * sc_ref[...]
        x1 = _layer_norm(DN_ALPHA * x_ref[s0:s0 + sub, :] + mix, g1_ref[...], b1_ref[...])
        o_ref[s0:s0 + sub, :] = _ffn_tail(x1, w1_ref, w3_ref, w2_ref, g2_ref, b2_ref)


def _pool_ffn_prompt(x, pwb, sc, g1, b1, fw, g2, b2, tm, sub):
    bsz, t, _ = x.shape
    assert t % tm == 0 and tm % sub == 0 and sub % POOL_ROWS == 0
    blk = pl.BlockSpec((None, tm, D_MODEL), lambda bi, ti: (bi, ti, 0))
    consts = (pwb, sc, g1, b1) + tuple(fw) + (g2, b2)
    return pl.pallas_call(
        functools.partial(_pool_ffn_prompt_kernel, tm=tm, sub=sub, n_valid=0),
        grid=(bsz, t // tm),
        in_specs=[blk] + [_resident(a.shape) for a in consts],
        out_specs=blk,
        out_shape=_f32(*x.shape),
        scratch_shapes=[pltpu.VMEM((tm + POOL_HALO, D_MODEL), F32), pltpu.VMEM((tm, D_MODEL), BF16)],
        compiler_params=_params(2),
        name="pool_ffn_prompt",
    )(x, *consts)


def _pool_sample_kernel(x_ref, buf_ref, pw_ref, sc_ref, g_ref, b_ref, o_ref, pwb_ref, nbuf_ref, *, n_valid):
    x = x_ref[...]
    pwb = pw_ref[...].astype(BF16)
    pwb_ref[...] = pwb
    outs = []
    for gi, win in enumerate(POOL_WINDOWS):
        lo, hi = gi * POOL_GROUP, (gi + 1) * POOL_GROUP
        xs = x[:, lo:hi]
        s = xs
        for d in range(1, win):
            s = s + buf_ref[POOL_BUF - d, :, lo:hi]
        pooled = s / float(min(1 + n_valid, win)) - xs
        outs.append(_dot(pooled.astype(BF16), pwb[gi]))
    mix = jnp.concatenate(outs, axis=-1) * sc_ref[...]
    o_ref[...] = _layer_norm(DN_ALPHA * x + mix, g_ref[...], b_ref[...])
    nbuf_ref[0:POOL_BUF - 1] = buf_ref[1:POOL_BUF]
    nbuf_ref[POOL_BUF - 1] = x


def _whole_call(kernel, args, outs, name):
    return pl.pallas_call(
        kernel,
        grid=(1,),
        in_specs=[_resident(a.shape) for a in args],
        out_specs=[_whole(o.shape) for o in outs],
        out_shape=list(outs),
        compiler_params=_params(1),
        name=name,
    )(*args)


def _pool_sample(x2d, buf, pw, sc, g, b):
    return _whole_call(functools.partial(_pool_sample_kernel, n_valid=min(PAST_LEN, POOL_BUF)),
                       (x2d, buf, pw, sc, g, b), [_f32(*x2d.shape), _bf16(*pw.shape), _f32(*buf.shape)],
                       "pool_sample")


def _softplus(x):
    return jnp.maximum(x, 0.0) + jnp.log1p(jnp.exp(-jnp.abs(x)))


def _gdn_gates(ba, alog_ref, dtb_ref):
    return jax.nn.sigmoid(ba), -jnp.exp(alog_ref[...]) * _softplus(ba + dtb_ref[...])


def _l2norm(v, scale=1.0):
    return v * (lax.rsqrt(jnp.sum(v * v, -1, keepdims=True) + RMS_EPS) * scale)


def _gdn_out(o_heads, z, x, ng_ref, wo, g_ref, b_ref):
    gated = []
    for h, oh in enumerate(o_heads):
        on = oh * lax.rsqrt(jnp.mean(oh * oh, -1, keepdims=True) + RMS_EPS) * ng_ref[...]
        gated.append((on * _silu(z[:, h * GDN_DV:(h + 1) * GDN_DV])).astype(BF16))
    mix = _dot(jnp.concatenate(gated, axis=-1), wo)
    return _layer_norm(DN_ALPHA * x + mix, g_ref[...], b_ref[...])


def _pair_index(c):
    ii = lax.broadcasted_iota(jnp.int32, (c, 2 * c), 0)
    lane = lax.broadcasted_iota(jnp.int32, (c, 2 * c), 1)
    return ii, lane % c, lane < c


def _block_diag2(p):
    _, _, left = _pair_index(p.shape[1])
    return jnp.concatenate([jnp.where(left, p, 0.0).astype(BF16), jnp.where(left, 0.0, p).astype(BF16)], axis=1)


def _unit_lower_inverse_pairs(a):
    c = a.shape[1]
    ii, jj, _ = _pair_index(c)
    x = jnp.where(ii == jj, 1.0, 0.0) - jnp.where((ii // 2 == jj // 2) & (ii > jj), a, 0.0)
    s = 4
    while s <= c:
        off = jnp.where((ii // s == jj // s) & (ii // (s // 2) != jj // (s // 2)) & (ii > jj), a, 0.0)
        y = _bdot('bik,bkj->bij', off.astype(BF16), _block_diag2(x))
        x = x - _bdot('bik,bkj->bij', x.astype(BF16), _block_diag2(y))
        s *= 2
    return x


def _gdn_prompt_kernel(x_ref, wqkvz_ref, wba_ref, cw_ref, alog_ref, dtb_ref, ng_ref, wo_ref, g_ref, b_ref,
                       o_ref, conv_ref, sout_ref,
                       ext0, ext1, ext2, ext3, s_ref, q_s, k_s, kb_s, vbk_s, kg_s, wkqg_s, u_s, at_s, dec_s, gl_s,
                       o_s, *, tm):
    c = GDN_CHUNK
    nc = tm // c
    ti = pl.program_id(1)
    exts = (ext0, ext1, ext2, ext3)
    pw = 2 * DK

    @pl.when(ti == 0)
    def _():
        for e in exts:
            e[0:CONV_HALO, :] = jnp.zeros((CONV_HALO, 3 * pw), F32)
        s_ref[...] = jnp.zeros(s_ref.shape, F32)

    x = x_ref[...]
    xb = x.astype(BF16)
    sub = lax.broadcasted_iota(jnp.int32, (c, LANES), 0) % SUBLANES
    keep = {back: sub >= back for back in range(1, GDN_CONV)}

    def project(p):
        for part in range(3):
            col0 = (part * HEADS + 2 * p) * DK
            exts[p][CONV_HALO:CONV_HALO + tm, part * pw:(part + 1) * pw] = _dot(xb, wqkvz_ref[:, col0:col0 + pw])

    def conv_silu(part, h, n):
        lo = part * pw + (h % 2) * DK
        wl = (part * HEADS + h) * DK
        e = exts[h // 2][n * c:(n + 1) * c + CONV_HALO, lo:lo + DK]
        y = e[CONV_HALO:] * cw_ref[GDN_CONV - 1:GDN_CONV, wl:wl + DK]
        e3 = e.reshape((c + CONV_HALO) // SUBLANES, SUBLANES, LANES)
        for j in range(GDN_CONV - 1):
            back = GDN_CONV - 1 - j
            r = pltpu.roll(e3, back, 1).reshape(c + CONV_HALO, LANES)
            y = y + jnp.where(keep[back], r[CONV_HALO:], r[0:c]) * cw_ref[j:j + 1, wl:wl + DK]
        return _silu(y)

    beta_all, gl = _gdn_gates(_dot(xb, wba_ref[...]), alog_ref, dtb_ref)
    pos = lax.broadcasted_iota(jnp.int32, (tm, LANES), 0) % c
    gc = gl
    step = 1
    while step < c:
        gc = gc + jnp.where(pos >= step, pltpu.roll(gc, step, 0), 0.0)
        step *= 2
    gcl = jnp.broadcast_to(gc.reshape(nc, c, LANES)[:, c - 1:c, :], (nc, c, LANES)).reshape(tm, LANES)
    eg = jnp.exp(gc)
    kdec = jnp.exp(gcl - gc)
    glast = jnp.exp(gcl)
    gct = gc.T

    ii, jj, left = _pair_index(c)
    lower = ii >= jj
    for h in range(HEADS):
        if h % 2 == 0:
            project(h // 2)
        for n in range(nc):
            rs = slice(n * c, (n + 1) * c)
            qn = _l2norm(conv_silu(0, h, n), DK ** -0.5)
            kn = _l2norm(conv_silu(1, h, n))
            vh = conv_silu(2, h, n)
            bh = beta_all[rs, h:h + 1]
            egh = eg[rs, HEADS + h:HEADS + h + 1]
            kb = kn * bh
            q_s[h, rs] = qn.astype(BF16)
            k_s[h, rs] = kn.astype(BF16)
            kb_s[h, rs] = kb.astype(BF16)
            vbk_s[h, rs, 0:GDN_DV] = (vh * bh).astype(BF16)
            vbk_s[h, rs, GDN_DV:] = (kb * egh).astype(BF16)
            wkqg_s[h, n, c:, :] = (qn * egh).astype(BF16)
            kg_s[h, rs] = (kn * kdec[rs, HEADS + h:HEADS + h + 1]).astype(BF16)
            gl_s[h, n] = jnp.broadcast_to(glast[n * c:n * c + SUBLANES, HEADS + h:HEADS + h + 1], (SUBLANES, LANES))
        for m in range(nc // 2):
            col = jnp.where(left, gc[2 * m * c:(2 * m + 1) * c, HEADS + h:HEADS + h + 1],
                            gc[(2 * m + 1) * c:(2 * m + 2) * c, HEADS + h:HEADS + h + 1])
            row = gct[HEADS + h:HEADS + h + 1, 2 * m * c:(2 * m + 2) * c]
            dec_s[h, m] = jnp.where(lower, jnp.exp(jnp.where(lower, col - row, 0.0)), 0.0)

    for p, e in enumerate(exts):
        for part in range(3):
            col0 = (part * HEADS + 2 * p) * DK
            conv_ref[:, col0:col0 + pw] = e[tm + CONV_HALO - (GDN_CONV - 1):tm + CONV_HALO, part * pw:(part + 1) * pw]
        e[0:CONV_HALO, :] = e[tm:tm + CONV_HALO, :]

    nb = HEAD_GROUP * nc // 2

    def diag_pair(g):
        return jnp.where(left, g[:, 0:c, :], g[:, c:, :])

    def precompute(gi, carry):
        hs = pl.ds(gi * HEAD_GROUP, HEAD_GROUP)
        dec = dec_s[hs].reshape(nb, c, 2 * c)
        kn2 = k_s[hs].reshape(nb, 2 * c, DK)
        a = diag_pair(_bdot('bid,bjd->bij', kb_s[hs].reshape(nb, 2 * c, DK), kn2))
        tinv = _unit_lower_inverse_pairs(jnp.where(ii > jj, a * dec, 0.0))
        uw = _bdot('bik,bkd->bid', _block_diag2(tinv), vbk_s[hs].reshape(nb, 2 * c, 2 * GDN_DV))
        u_s[hs] = uw[:, :, 0:GDN_DV].reshape(HEAD_GROUP, nc, c, GDN_DV)
        wkqg_s[hs, :, 0:c, :] = uw[:, :, GDN_DV:].astype(BF16).reshape(HEAD_GROUP, nc, c, DK)
        attn = diag_pair(_bdot('bid,bjd->bij', q_s[hs].reshape(nb, 2 * c, DK), kn2)) * dec
        at_s[hs] = attn.astype(BF16).reshape(HEAD_GROUP, nc // 2, c, 2 * c)
        return carry

    lax.fori_loop(0, HEADS // HEAD_GROUP, precompute, 0)

    def chunk_pair(m, carry):
        at2 = at_s[:, m]
        for half in range(2):
            n = 2 * m + half
            sl = pl.ds(pl.multiple_of(n * c, c), c)
            s = s_ref[...]
            r = _bdot('hid,hde->hie', wkqg_s[:, n], s.astype(BF16))
            vnb = (u_s[:, n] - r[:, 0:c, :]).astype(BF16)
            o_s[:, sl, :] = r[:, c:, :] + _bdot('hij,hje->hie', at2[:, :, half * c:(half + 1) * c], vnb)
            s_ref[...] = s * gl_s[:, n][:, 0:1, :] + _bdot('hcd,hce->hde', kg_s[:, sl, :], vnb)
        return carry

    lax.fori_loop(0, nc // 2, chunk_pair, 0)

    z = _dot(xb, wqkvz_ref[:, GDN_CONV_CH:])
    o_ref[...] = _gdn_out([o_s[h] for h in range(HEADS)], z, x, ng_ref, wo_ref[...], g_ref, b_ref)

    @pl.when(ti == pl.num_programs(1) - 1)
    def _():
        sout_ref[...] = s_ref[...]


def _gdn_prompt(x, w, g, b, tm):
    bsz, t, _ = x.shape
    assert t % tm == 0 and tm % (2 * GDN_CHUNK) == 0
    c = GDN_CHUNK
    nc = tm // c
    blk = pl.BlockSpec((None, tm, D_MODEL), lambda bi, ti: (bi, ti, 0))
    consts = (w["wqkvz"], w["wba"], w["conv_w"], w["alog"], w["dtb"], w["ng"], w["wo"], g, b)
    head_bf16 = pltpu.VMEM((HEADS, tm, DK), BF16)
    return pl.pallas_call(
        functools.partial(_gdn_prompt_kernel, tm=tm),
        grid=(bsz, t // tm),
        in_specs=[blk] + [_resident(a.shape) for a in consts],
        out_specs=[blk,
                   pl.BlockSpec((None, GDN_CONV - 1, GDN_CONV_CH), lambda bi, ti: (bi, 0, 0)),
                   pl.BlockSpec((None, HEADS, DK, GDN_DV), lambda bi, ti: (bi, 0, 0, 0))],
        out_shape=[_f32(*x.shape), _f32(bsz, GDN_CONV - 1, GDN_CONV_CH), _f32(bsz, HEADS, DK, GDN_DV)],
        scratch_shapes=[pltpu.VMEM((tm + CONV_HALO, 3 * 2 * DK), F32)] * (HEADS // 2)
        + [pltpu.VMEM((HEADS, DK, GDN_DV), F32),
                        head_bf16, head_bf16, head_bf16,
                        pltpu.VMEM((HEADS, tm, 2 * GDN_DV), BF16),
                        head_bf16,
                        pltpu.VMEM((HEADS, nc, 2 * c, DK), BF16),
                        pltpu.VMEM((HEADS, nc, c, GDN_DV), F32),
                        pltpu.VMEM((HEADS, nc // 2, c, 2 * c), BF16),
                        pltpu.VMEM((HEADS, nc // 2, c, 2 * c), F32),
                        pltpu.VMEM((HEADS, nc, SUBLANES, LANES), F32),
                        pltpu.VMEM((HEADS, tm, GDN_DV), F32)],
        compiler_params=_params(2),
        name="gdn_prompt",
    )(x, *consts)


def _stream_projection(xb, win_ref, wb_ref, proj_s):
    wb = win_ref[...].astype(BF16)
    wb_ref[...] = wb
    proj_s[pl.program_id(0)] = _dot(xb, wb)


def _gdn_sample_a_kernel(x_ref, cb_ref, win_ref, wba_ref, cw_ref, alog_ref, dtb_ref,
                         rows_ref, pre_ref, z_ref, wb_ref, proj_s):
    xb = x_ref[...].astype(BF16)
    _stream_projection(xb, win_ref, wb_ref, proj_s)

    @pl.when(pl.program_id(0) == pl.num_programs(0) - 1)
    def _():
        nq = GDN_CONV_CH // W_COLS
        pre = jnp.concatenate([proj_s[i] for i in range(nq)], axis=-1)
        pre_ref[...] = pre
        z_ref[...] = proj_s[nq]
        y = pre * cw_ref[GDN_CONV - 1:GDN_CONV, :]
        for j in range(GDN_CONV - 1):
            y = y + cb_ref[j] * cw_ref[j:j + 1, :]
        qkv = _silu(y)
        beta_all, gl = _gdn_gates(_dot(xb, wba_ref[...]), alog_ref, dtb_ref)
        eg = jnp.exp(gl)
        rows = x_ref.shape[0]
        rows_ref[...] = jnp.zeros(rows_ref.shape, F32)
        for h in range(HEADS):
            hs = slice(h * DK, (h + 1) * DK)
            qn = _l2norm(qkv[:, hs], DK ** -0.5)
            kn = _l2norm(qkv[:, HEADS * DK + h * DK:HEADS * DK + (h + 1) * DK])
            vh = qkv[:, 2 * HEADS * DK + h * GDN_DV:2 * HEADS * DK + (h + 1) * GDN_DV]
            bh = beta_all[:, h:h + 1]
            egh = eg[:, HEADS + h:HEADS + h + 1]
            rows_ref[:, h, 0, :] = kn
            rows_ref[:, h, 1, :] = qn
            rows_ref[:, h, 2, :] = vh * bh
            rows_ref[:, h, 3, :] = jnp.broadcast_to(bh * egh, (rows, DK))
            rows_ref[:, h, 4, :] = jnp.broadcast_to(egh, (rows, DK))
            rows_ref[:, h, 5, :] = jnp.broadcast_to(jnp.sum(qn * kn, -1, keepdims=True), (rows, DK))


def _gdn_sample_b_kernel(p_ref, s_ref, o_ref, sout_ref):
    bb = p_ref.shape[0]
    n = bb * HEADS
    p = p_ref[...].reshape(n, SUBLANES, DK)
    s = s_ref[...].reshape(n, DK, GDN_DV)
    pb = p.astype(BF16)
    r = _bdot('nqd,nde->nqe', pb, s.astype(BF16))
    v_new = p[:, 2:3, :] - p[:, 3:4, :] * r[:, 0:1, :]
    o = p[:, 4:5, :] * r[:, 1:2, :] + p[:, 5:6, :] * v_new
    row0 = lax.broadcasted_iota(jnp.int32, (n, SUBLANES, GDN_DV), 1) == 0
    vn8 = jnp.where(row0, jnp.broadcast_to(v_new, (n, SUBLANES, GDN_DV)), 0.0)
    s_new = s * p[:, 4:5, :] + _bdot('nqd,nqe->nde', pb, vn8.astype(BF16))
    o_ref[...] = jnp.broadcast_to(o, (n, SUBLANES, GDN_DV)).reshape(bb, HEADS, SUBLANES, GDN_DV)
    sout_ref[...] = s_new.reshape(bb, HEADS, DK, GDN_DV)


def _gdn_sample_c_kernel(o_ref, z_ref, x_ref, ng_ref, wo_ref, g_ref, b_ref, out_ref, wob_ref):
    wob = wo_ref[...].astype(BF16)
    wob_ref[...] = wob
    out_ref[...] = _gdn_out([o_ref[:, h, 0, :] for h in range(HEADS)],
                            z_ref[...], x_ref[...], ng_ref, wob, g_ref, b_ref)


def _projection_call(kernel, resident_args, w_in_all, layer, n_steps, n_cols, outs, rows, name):
    return pl.pallas_call(
        kernel,
        grid=(n_steps,),
        in_specs=[_resident(a.shape) for a in resident_args[:2]]
        + [pl.BlockSpec((None, D_MODEL, W_COLS), lambda c: (layer, 0, c))]
        + [_resident(a.shape) for a in resident_args[2:]],
        out_specs=[_whole(o.shape) for o in outs] + [pl.BlockSpec((D_MODEL, W_COLS), lambda c: (0, c))],
        out_shape=list(outs) + [_bf16(D_MODEL, n_cols)],
        scratch_shapes=[pltpu.VMEM((n_steps, rows, W_COLS), F32)],
        compiler_params=_params(1),
        name=name,
    )(*resident_args[:2], w_in_all, *resident_args[2:])


def _state_step(kernel, p, extra, s0, bb, name):
    bsz = s0.shape[0]
    assert bsz % bb == 0

    def blk(a):
        return pl.BlockSpec((bb,) + a.shape[1:], lambda i: (i,) + (0,) * (a.ndim - 1))

    ins = (p,) + tuple(extra) + (s0,)
    o_shape = (bsz, HEADS, SUBLANES, s0.shape[-1])
    return pl.pallas_call(
        kernel,
        grid=(bsz // bb,),
        in_specs=[blk(a) for a in ins],
        out_specs=[pl.BlockSpec((bb,) + o_shape[1:], lambda i: (i, 0, 0, 0)), blk(s0)],
        out_shape=[_f32(*o_shape), _f32(*s0.shape)],
        compiler_params=_params(1),
        name=name,
    )(*ins)


def _gdn_sample(x2d, conv_buf, s0, w_in_all, w_out, layer, w, g, b):
    bsz = x2d.shape[0]
    cb_t = conv_buf.transpose(1, 0, 2)
    rows, pre, z, wqkvz = _projection_call(
        _gdn_sample_a_kernel, (x2d, cb_t, w["wba"], w["conv_w"], w["alog"], w["dtb"]), w_in_all, layer,
        GDN_QKVZ // W_COLS, GDN_QKVZ,
        [_f32(bsz, HEADS, SUBLANES, DK), _f32(bsz, GDN_CONV_CH), _f32(bsz, D_MODEL)], bsz, "gdn_sample_a")
    o8, s_new = _state_step(_gdn_sample_b_kernel, rows, (), s0, 8, "gdn_sample_b")
    x1, wob = _whole_call(_gdn_sample_c_kernel, (o8, z, x2d, w["ng"], w_out, g, b),
                          [_f32(bsz, D_MODEL), _bf16(*w_out.shape)], "gdn_sample_c")
    new_conv = jnp.concatenate([conv_buf[:, 1:], pre[:, None, :]], axis=1)
    return x1, new_conv, s_new, wqkvz, wob


def _log_gamma(h):
    return math.log(1.0 - 2.0 ** (-5.0 - h))


def _rope_tables(pos, freq_ref):
    ang = pos * freq_ref[...]
    lane = lax.broadcasted_iota(jnp.int32, ang.shape, 1)
    return jnp.cos(ang), jnp.where(lane < DK // 2, -1.0, 1.0) * jnp.sin(ang)


def _rope(v, cos2, sin2):
    return v * cos2 + pltpu.roll(v, DK // 2, 1) * sin2


def _ret_out(o_heads, gate, x, wo, g_ref, b_ref):
    gated = []
    for h, oh in enumerate(o_heads):
        mu = jnp.mean(oh, -1, keepdims=True)
        oc = oh - mu
        on = oc * lax.rsqrt(jnp.mean(oc * oc, -1, keepdims=True) + LN_EPS)
        gated.append((_silu(gate[:, h * RET_DV:(h + 1) * RET_DV]) * on).astype(BF16))
    mix = _dot(jnp.concatenate(gated, axis=-1), wo)
    return _layer_norm(DN_ALPHA * x + mix, g_ref[...], b_ref[...])


def _ret_prompt_kernel(x_ref, win_ref, freq_ref, wo_ref, g_ref, b_ref, o_ref, sout_ref,
                       s_ref, q_s, k_s, qx_s, kz_s, v_s, dec_s, o_s, *, tm, c, pos0):
    nc = tm // c
    ti = pl.program_id(1)
    hk = HEADS * DK

    @pl.when(ti == 0)
    def _():
        s_ref[...] = jnp.zeros(s_ref.shape, F32)
        diff = (lax.broadcasted_iota(jnp.int32, (c, c), 0) - lax.broadcasted_iota(jnp.int32, (c, c), 1)).astype(F32)
        for h in range(HEADS):
            dec_s[h] = jnp.where(diff >= 0, jnp.exp(jnp.maximum(diff, 0.0) * _log_gamma(h)), 0.0)

    x = x_ref[...]
    xb = x.astype(BF16)
    row = lax.broadcasted_iota(jnp.int32, (tm, 1), 0)
    cos2, sin2 = _rope_tables((pos0 + ti * tm + row).astype(F32), freq_ref)
    idx = (row % c).astype(F32)
    qall = _dot(xb, win_ref[:, 0:hk])
    kall = _dot(xb, win_ref[:, hk:2 * hk])
    vall = _dot(xb, win_ref[:, 2 * hk:2 * hk + HEADS * RET_DV])
    for h in range(HEADS):
        hs = slice(h * DK, (h + 1) * DK)
        qr = _rope(qall[:, hs], cos2, sin2)
        kr = _rope(kall[:, hs], cos2, sin2) * DK ** -0.5
        q_s[h] = qr.astype(BF16)
        k_s[h] = kr.astype(BF16)
        qx_s[h] = (qr * jnp.exp((idx + 1.0) * _log_gamma(h))).astype(BF16)
        kz_s[h] = (kr * jnp.exp((c - 1.0 - idx) * _log_gamma(h))).astype(BF16)
        v_s[h] = vall[:, h * RET_DV:(h + 1) * RET_DV].astype(BF16)

    for n in range(nc):
        sl = slice(n * c, (n + 1) * c)
        v_c = v_s[:, sl, :]
        attn = _bdot('hid,hjd->hij', q_s[:, sl, :], k_s[:, sl, :]) * dec_s[...]
        s = s_ref[...]
        o_s[:, sl, :] = (_bdot('hij,hje->hie', attn.astype(BF16), v_c)
                         + _bdot('hcd,hde->hce', qx_s[:, sl, :], s.astype(BF16)))
        upd = _bdot('hcd,hce->hde', kz_s[:, sl, :], v_c)
        for h in range(HEADS):
            s_ref[h] = math.exp(c * _log_gamma(h)) * s[h] + upd[h]

    gate = _dot(xb, win_ref[:, 2 * hk + HEADS * RET_DV:])
    o_ref[...] = _ret_out([o_s[h] for h in range(HEADS)], gate, x, wo_ref[...], g_ref, b_ref)

    @pl.when(ti == pl.num_programs(1) - 1)
    def _():
        sout_ref[...] = s_ref[...]


def _ret_prompt(x, w, g, b, tm):
    bsz, t, _ = x.shape
    c = min(RET_CHUNK, tm)
    assert t % tm == 0 and tm % c == 0
    blk = pl.BlockSpec((None, tm, D_MODEL), lambda bi, ti: (bi, ti, 0))
    consts = (w["win"], w["freq"], w["wo"], g, b)
    head_k = pltpu.VMEM((HEADS, tm, DK), BF16)
    return pl.pallas_call(
        functools.partial(_ret_prompt_kernel, tm=tm, c=c, pos0=0),
        grid=(bsz, t // tm),
        in_specs=[blk] + [_resident(a.shape) for a in consts],
        out_specs=[blk, pl.BlockSpec((None, HEADS, DK, RET_DV), lambda bi, ti: (bi, 0, 0, 0))],
        out_shape=[_f32(*x.shape), _f32(bsz, HEADS, DK, RET_DV)],
        scratch_shapes=[pltpu.VMEM((HEADS, DK, RET_DV), F32), head_k, head_k, head_k, head_k,
                        pltpu.VMEM((HEADS, tm, RET_DV), BF16),
                        pltpu.VMEM((HEADS, c, c), F32),
                        pltpu.VMEM((HEADS, tm, RET_DV), F32)],
        compiler_params=_params(2),
        name="ret_prompt",
    )(x, *consts)


def _ret_sample_a_kernel(x_ref, freq_ref, win_ref, kq_ref, vrows_ref, gate_ref, wb_ref, proj_s, *, pos0):
    xb = x_ref[...].astype(BF16)
    _stream_projection(xb, win_ref, wb_ref, proj_s)

    @pl.when(pl.program_id(0) == pl.num_programs(0) - 1)
    def _():
        rows = x_ref.shape[0]
        cos2, sin2 = _rope_tables(jnp.full((1, 1), pos0, F32), freq_ref)
        qall, kall = proj_s[0], proj_s[1]
        gate_ref[...] = jnp.concatenate([proj_s[4], proj_s[5]], axis=-1)
        kq_ref[...] = jnp.zeros(kq_ref.shape, F32)
        vrows_ref[...] = jnp.zeros(vrows_ref.shape, F32)
        for h in range(HEADS):
            hs = slice(h * DK, (h + 1) * DK)
            qr = _rope(qall[:, hs], cos2, sin2)
            kr = _rope(kall[:, hs], cos2, sin2) * DK ** -0.5
            kq_ref[:, h, 0, :] = kr
            kq_ref[:, h, 1, :] = qr
            vh = proj_s[2 + h // 4][:, (h % 4) * RET_DV:(h % 4 + 1) * RET_DV]
            vrows_ref[:, h, 0, :] = vh
            vrows_ref[:, h, 1, :] = jnp.broadcast_to(jnp.sum(qr * kr, -1, keepdims=True), (rows, RET_DV))
            vrows_ref[:, h, 2, :] = jnp.full((rows, RET_DV), math.exp(_log_gamma(h)), F32)


def _ret_sample_b_kernel(kq_ref, v_ref, s_ref, o_ref, sout_ref):
    bb = kq_ref.shape[0]
    n = bb * HEADS
    kqb = kq_ref[...].reshape(n, SUBLANES, DK).astype(BF16)
    v8 = v_ref[...].reshape(n, SUBLANES, RET_DV)
    s = s_ref[...].reshape(n, DK, RET_DV)
    r = _bdot('nqd,nde->nqe', kqb, s.astype(BF16))
    gamma = v8[:, 2:3, :]
    o = v8[:, 1:2, :] * v8[:, 0:1, :] + gamma * r[:, 1:2, :]
    row0 = lax.broadcasted_iota(jnp.int32, (n, SUBLANES, RET_DV), 1) == 0
    s_new = s * gamma + _bdot('nqd,nqe->nde', kqb, jnp.where(row0, v8, 0.0).astype(BF16))
    o_ref[...] = jnp.broadcast_to(o, (n, SUBLANES, RET_DV)).reshape(bb, HEADS, SUBLANES, RET_DV)
    sout_ref[...] = s_new.reshape(bb, HEADS, DK, RET_DV)


def _ret_sample_c_kernel(o_ref, gate_ref, x_ref, wo_ref, g_ref, b_ref, out_ref, wob_ref):
    wob = wo_ref[...].astype(BF16)
    wob_ref[...] = wob
    out_ref[...] = _ret_out([o_ref[:, h, 0, :] for h in range(HEADS)],
                            gate_ref[...], x_ref[...], wob, g_ref, b_ref)


def _ret_sample(x2d, s0, w_in_all, w_out, layer, freq, g, b):
    bsz = x2d.shape[0]
    kq, vrows, gate, winb = _projection_call(
        functools.partial(_ret_sample_a_kernel, pos0=float(PAST_LEN)), (x2d, freq), w_in_all, layer,
        RET_IN // W_COLS, RET_IN,
        [_f32(bsz, HEADS, SUBLANES, DK), _f32(bsz, HEADS, SUBLANES, RET_DV), _f32(bsz, HEADS * RET_DV)],
        bsz, "ret_sample_a")
    o8, s_new = _state_step(_ret_sample_b_kernel, kq, (vrows,), s0, 4, "ret_sample_b")
    x1, wob = _whole_call(_ret_sample_c_kernel, (o8, gate, x2d, w_out, g, b),
                          [_f32(bsz, D_MODEL), _bf16(*w_out.shape)], "ret_sample_c")
    return x1, s_new, winb, wob


def _lane_row(values, offset):
    return jnp.zeros((1, LANES), F32).at[0, offset:offset + HEADS].set(values.astype(F32))


def _gdn_small_weights(w_in, conv_w, a_log, dt_bias, norm_g):
    wba = jnp.zeros((D_MODEL, LANES), F32).at[:, :2 * HEADS].set(w_in[:, GDN_QKVZ:])
    return {"wba": wba.astype(BF16), "conv_w": conv_w,
            "alog": _lane_row(a_log, HEADS), "dtb": _lane_row(dt_bias, HEADS), "ng": norm_g.reshape(1, GDN_DV)}


def _rope_freqs():
    half = DK // 2
    freqs = ROPE_BASE ** (-jnp.arange(half, dtype=F32) / half)
    return jnp.concatenate([freqs, freqs]).reshape(1, DK)


def kernel(x_prompt, x_sample, state_pool, state_gdn_conv, state_gdn, state_ret, pool_w, pool_scale, gdn_w_in,
           gdn_conv_w, gdn_a_log, gdn_dt_bias, gdn_norm_g, gdn_w_out, ret_w_in, ret_w_out, ffn_w13, ffn_w2,
           ln_g, ln_b):
    bsz, seq, _ = x_prompt.shape
    dec = x_sample.shape[0]
    tm = min(512, seq)
    xp = x_prompt
    xs = x_sample.reshape(dec, D_MODEL)
    pool_p, pool_s, conv_p, conv_s, gdn_p, gdn_s, ret_p, ret_s = [], [], [], [], [], [], [], []
    for i in range(DEPTH):
        kind, j = i % N_MIXERS, i // N_MIXERS
        g1, b1 = ln_g[i, 0].reshape(1, D_MODEL), ln_b[i, 0].reshape(1, D_MODEL)
        g2, b2 = ln_g[i, 1].reshape(1, D_MODEL), ln_b[i, 1].reshape(1, D_MODEL)
        if kind == 0:
            sc = pool_scale[j].reshape(1, D_MODEL)
            pool_p.append(xp[:, seq - POOL_BUF:])
            xs, pwb, nbuf = _pool_sample(xs, state_pool[j].transpose(1, 0, 2), pool_w[j], sc, g1, b1)
            pool_s.append(nbuf)
            xs, *fw = _ffn_cast(xs, ffn_w13, ffn_w2, i, g2, b2)
            xp = _pool_ffn_prompt(xp, pwb, sc, g1, b1, fw, g2, b2, min(2 * tm, seq), tm)
            continue
        if kind == 1:
            w = _gdn_small_weights(gdn_w_in[j], gdn_conv_w[j], gdn_a_log[j], gdn_dt_bias[j], gdn_norm_g[j])
            xs, cs, ss, w["wqkvz"], w["wo"] = _gdn_sample(xs, state_gdn_conv[j], state_gdn[j], gdn_w_in,
                                                         gdn_w_out[j], j, w, g1, b1)
            xp, cp, sp = _gdn_prompt(xp, w, g1, b1, tm)
            conv_p.append(cp), conv_s.append(cs), gdn_p.append(sp), gdn_s.append(ss)
        else:
            w = {"freq": _rope_freqs()}
            xs, ss, w["win"], w["wo"] = _ret_sample(xs, state_ret[j], ret_w_in, ret_w_out[j], j, w["freq"], g1, b1)
            xp, sp = _ret_prompt(xp, w, g1, b1, tm)
            ret_p.append(sp), ret_s.append(ss)
        xs, *fw = _ffn_cast(xs, ffn_w13, ffn_w2, i, g2, b2)
        xp = _ffn(xp.reshape(bsz * seq, D_MODEL), fw, g2, b2, min(2 * tm, seq), tm).reshape(bsz, seq, D_MODEL)
    return (xp, xs.reshape(dec, 1, D_MODEL), jnp.stack(pool_p), jnp.stack(pool_s).transpose(0, 2, 1, 3), jnp.stack(conv_p),
            jnp.stack(conv_s), jnp.stack(gdn_p), jnp.stack(gdn_s), jnp.stack(ret_p), jnp.stack(ret_s))
```

```python
import functools
import math

import jax
import jax.numpy as jnp
from jax import lax
from jax.experimental import pallas as pl
from jax.experimental.pallas import tpu as pltpu

F32 = jnp.float32
BF16 = jnp.bfloat16

D_MODEL = 1024
DEPTH = 4
PAST_LEN = 16384
N_MIXERS = 3

POOL_WINDOWS = (2, 4, 8, 16)
POOL_GROUP = D_MODEL // len(POOL_WINDOWS)
POOL_BUF = max(POOL_WINDOWS) - 1
POOL_HALO = 16
POOL_ROWS = 64

HEADS = 8
DK = D_MODEL // HEADS
GDN_DV = DK
GDN_CONV = 4
GDN_CONV_CH = HEADS * (2 * DK + GDN_DV)
GDN_QKVZ = GDN_CONV_CH + HEADS * GDN_DV
CONV_HALO = 8
CONV_PHASES = 4
GDN_CHUNK = 64
HEAD_GROUP = 8
RET_DV = 2 * DK
RET_IN = 2 * HEADS * DK + 2 * HEADS * RET_DV
RET_CHUNK = 256
ROPE_BASE = 10000.0
W_COLS = 1024

D_FF = -(-8 * D_MODEL // (3 * 256)) * 256
FF_CHUNK = 256
CAST_CHUNK = 256
DN_ALPHA = (2 * DEPTH) ** 0.25
LN_EPS = 1e-5
RMS_EPS = 1e-6

SUBLANES = 8
LANES = 128
VMEM_LIMIT = 56 * 1024 * 1024


def _params(n_grid):
    return pltpu.CompilerParams(dimension_semantics=("arbitrary",) * n_grid, vmem_limit_bytes=VMEM_LIMIT)


def _resident(shape):
    nd = len(shape)
    return pl.BlockSpec(shape, lambda *_: (0,) * nd, pipeline_mode=pl.Buffered(1))


def _whole(shape):
    nd = len(shape)
    return pl.BlockSpec(shape, lambda *_: (0,) * nd)


def _dot(a, b):
    return jnp.dot(a, b, preferred_element_type=F32)


def _bdot(spec, a, b):
    return jnp.einsum(spec, a, b, preferred_element_type=F32)


def _silu(x):
    return x * jax.nn.sigmoid(x)


def _layer_norm(y, g, b):
    mu = jnp.mean(y, -1, keepdims=True)
    yc = y - mu
    var = jnp.mean(yc * yc, -1, keepdims=True)
    return yc * lax.rsqrt(var + LN_EPS) * g + b


def _f32(*shape):
    return jax.ShapeDtypeStruct(shape, F32)


def _bf16(*shape):
    return jax.ShapeDtypeStruct(shape, BF16)


def _ffn_tail(x, w1_ref, w3_ref, w2_ref, g_ref, b_ref):
    xb = x.astype(BF16)
    acc = None
    for c in range(D_FF // FF_CHUNK):
        lo = c * FF_CHUNK
        a = _dot(xb, w1_ref[:, lo:lo + FF_CHUNK])
        b = _dot(xb, w3_ref[:, lo:lo + FF_CHUNK])
        h = (_silu(a) * b).astype(BF16)
        p = _dot(h, w2_ref[lo:lo + FF_CHUNK, :])
        acc = p if acc is None else acc + p
    return _layer_norm(DN_ALPHA * x + acc, g_ref[...], b_ref[...])


def _ffn_kernel(x_ref, w1_ref, w3_ref, w2_ref, g_ref, b_ref, o_ref, *, sub):
    for r0 in range(0, x_ref.shape[0], sub):
        o_ref[r0:r0 + sub, :] = _ffn_tail(x_ref[r0:r0 + sub, :], w1_ref, w3_ref, w2_ref, g_ref, b_ref)


def _ffn(x2d, fw, g, b, tm, sub):
    n = x2d.shape[0]
    assert n % tm == 0 and tm % sub == 0
    row = pl.BlockSpec((tm, D_MODEL), lambda i: (i, 0))
    consts = tuple(fw) + (g, b)
    return pl.pallas_call(
        functools.partial(_ffn_kernel, sub=sub),
        grid=(n // tm,),
        in_specs=[row] + [_resident(a.shape) for a in consts],
        out_specs=row,
        out_shape=_f32(n, D_MODEL),
        compiler_params=_params(1),
        name="ffn",
    )(x2d, *consts)


def _ffn_cast_kernel(x_ref, w1_ref, w3_ref, w2_ref, g_ref, b_ref, o_ref, w1b_ref, w3b_ref, w2b_ref, acc_ref):
    c = pl.program_id(0)

    @pl.when(c == 0)
    def _():
        acc_ref[...] = jnp.zeros(acc_ref.shape, F32)

    xb = x_ref[...].astype(BF16)
    w1b, w3b, w2b = w1_ref[...].astype(BF16), w3_ref[...].astype(BF16), w2_ref[...].astype(BF16)
    w1b_ref[...], w3b_ref[...], w2b_ref[...] = w1b, w3b, w2b
    h = (_silu(_dot(xb, w1b)) * _dot(xb, w3b)).astype(BF16)
    acc_ref[...] += _dot(h, w2b)

    @pl.when(c == pl.num_programs(0) - 1)
    def _():
        o_ref[...] = _layer_norm(DN_ALPHA * x_ref[...] + acc_ref[...], g_ref[...], b_ref[...])


def _ffn_cast(x2d, w13_all, w2_all, layer, g, b):
    n = x2d.shape[0]
    cc = CAST_CHUNK
    nch = D_FF // cc
    return pl.pallas_call(
        _ffn_cast_kernel,
        grid=(nch,),
        in_specs=[_resident(x2d.shape),
                  pl.BlockSpec((None, D_MODEL, cc), lambda c: (layer, 0, c)),
                  pl.BlockSpec((None, D_MODEL, cc), lambda c: (layer, 0, nch + c)),
                  pl.BlockSpec((None, cc, D_MODEL), lambda c: (layer, c, 0)),
                  _resident(g.shape), _resident(b.shape)],
        out_specs=[_whole(x2d.shape),
                   pl.BlockSpec((D_MODEL, cc), lambda c: (0, c)),
                   pl.BlockSpec((D_MODEL, cc), lambda c: (0, c)),
                   pl.BlockSpec((cc, D_MODEL), lambda c: (c, 0))],
        out_shape=[_f32(n, D_MODEL), _bf16(D_MODEL, D_FF), _bf16(D_MODEL, D_FF), _bf16(D_FF, D_MODEL)],
        scratch_shapes=[pltpu.VMEM((n, D_MODEL), F32)],
        compiler_params=_params(1),
        name="ffn_cast",
    )(x2d, w13_all, w13_all, w2_all, g, b)


def _rows_up(v, s):
    rows, w = v.shape
    if s % SUBLANES:
        sub = lax.broadcasted_iota(jnp.int32, v.shape, 0) % SUBLANES
        v = pltpu.roll(v.reshape(rows // SUBLANES, SUBLANES, w), s % SUBLANES, 1).reshape(rows, w)
        v = jnp.where(sub >= s % SUBLANES, v, jnp.concatenate([v[0:SUBLANES], v[0:rows - SUBLANES]], axis=0))
    whole = s - s % SUBLANES
    return jnp.concatenate([v[0:whole], v[0:rows - whole]], axis=0) if whole else v


def _trailing_sums(e, win):
    s, w = e, 1
    while w < win:
        s = s + _rows_up(s, w)
        w *= 2
    return s


def _pool_ffn_prompt_kernel(x_ref, pw_ref, sc_ref, g1_ref, b1_ref, w1_ref, w3_ref, w2_ref, g2_ref, b2_ref,
                            o_ref, ext_ref, pooled_ref, *, tm, sub, n_valid):
    ti = pl.program_id(1)
    rb = POOL_ROWS

    @pl.when(ti == 0)
    def _():
        ext_ref[0:POOL_HALO, :] = jnp.zeros((POOL_HALO, D_MODEL), F32)

    ext_ref[POOL_HALO:POOL_HALO + tm, :] = x_ref[...]

    def pool_rows(r0):
        t = ti * tm + r0 + lax.broadcasted_iota(jnp.int32, (rb, 1), 0)
        for gi, win in enumerate(POOL_WINDOWS):
            lo, hi = gi * POOL_GROUP, (gi + 1) * POOL_GROUP
            e = ext_ref[r0:r0 + rb + POOL_HALO, lo:hi]
            cnt = jnp.minimum(t + 1 + n_valid, win).astype(F32)
            pooled = _trailing_sums(e, win)[POOL_HALO:] / cnt - e[POOL_HALO:]
            pooled_ref[r0:r0 + rb, lo:hi] = pooled.astype(BF16)

    for r0 in range(0, tm, rb):
        pool_rows(r0)
    ext_ref[0:POOL_HALO, :] = ext_ref[tm:tm + POOL_HALO, :]
    for s0 in range(0, tm, sub):
        mix = jnp.concatenate(
            [_dot(pooled_ref[s0:s0 + sub, gi * POOL_GROUP:(gi + 1) * POOL_GROUP], pw_ref[gi])
             for gi in range(len(POOL_WINDOWS))], axis=-1) * sc_ref[...]
        x1 = _layer_norm(DN_ALPHA * x_ref[s0:s0 + sub, :] + mix, g1_ref[...], b1_ref[...])
        o_ref[s0:s0 + sub, :] = _ffn_tail(x1, w1_ref, w3_ref, w2_ref, g2_ref, b2_ref)


def _pool_ffn_prompt(x, pwb, sc, g1, b1, fw, g2, b2, tm, sub):
    bsz, t, _ = x.shape
    assert t % tm == 0 and tm % sub == 0 and sub % POOL_ROWS == 0
    blk = pl.BlockSpec((None, tm, D_MODEL), lambda bi, ti: (bi, ti, 0))
    consts = (pwb, sc, g1, b1) + tuple(fw) + (g2, b2)
    return pl.pallas_call(
        functools.partial(_pool_ffn_prompt_kernel, tm=tm, sub=sub, n_valid=0),
        grid=(bsz, t // tm),
        in_specs=[blk] + [_resident(a.shape) for a in consts],
        out_specs=blk,
        out_shape=_f32(*x.shape),
        scratch_shapes=[pltpu.VMEM((tm + POOL_HALO, D_MODEL), F32), pltpu.VMEM((tm, D_MODEL), BF16)],
        compiler_params=_params(2),
        name="pool_ffn_prompt",
    )(x, *consts)


def _pool_sample_kernel(x_ref, buf_ref, pw_ref, sc_ref, g_ref, b_ref, o_ref, pwb_ref, nbuf_ref, *, n_valid):
    x = x_ref[...]
    pwb = pw_ref[...].astype(BF16)
    pwb_ref[...] = pwb
    outs = []
    for gi, win in enumerate(POOL_WINDOWS):
        lo, hi = gi * POOL_GROUP, (gi + 1) * POOL_GROUP
        xs = x[:, lo:hi]
        s = xs
        for d in range(1, win):
            s = s + buf_ref[POOL_BUF - d, :, lo:hi]
        pooled = s / float(min(1 + n_valid, win)) - xs
        outs.append(_dot(pooled.astype(BF16), pwb[gi]))
    mix = jnp.concatenate(outs, axis=-1) * sc_ref[...]
    o_ref[...] = _layer_norm(DN_ALPHA * x + mix, g_ref[...], b_ref[...])
    nbuf_ref[0:POOL_BUF - 1] = buf_ref[1:POOL_BUF]
    nbuf_ref[POOL_BUF - 1] = x


def _whole_call(kernel, args, outs, name):
    return pl.pallas_call(
        kernel,
        grid=(1,),
        in_specs=[_resident(a.shape) for a in args],
        out_specs=[_whole(o.shape) for o in outs],
        out_shape=list(outs),
        compiler_params=_params(1),
        name=name,
    )(*args)


def _pool_sample(x2d, buf, pw, sc, g, b):
    return _whole_call(functools.partial(_pool_sample_kernel, n_valid=min(PAST_LEN, POOL_BUF)),
                       (x2d, buf, pw, sc, g, b), [_f32(*x2d.shape), _bf16(*pw.shape), _f32(*buf.shape)],
                       "pool_sample")


def _softplus(x):
    return jnp.maximum(x, 0.0) + jnp.log1p(jnp.exp(-jnp.abs(x)))


def _gdn_gates(ba, alog_ref, dtb_ref):
    return jax.nn.sigmoid(ba), -jnp.exp(alog_ref[...]) * _softplus(ba + dtb_ref[...])


def _l2norm(v, scale=1.0):
    return v * (lax.rsqrt(jnp.sum(v * v, -1, keepdims=True) + RMS_EPS) * scale)


def _gdn_out(o_heads, z, x, ng_ref, wo, g_ref, b_ref):
    gated = []
    for h, oh in enumerate(o_heads):
        on = oh * lax.rsqrt(jnp.mean(oh * oh, -1, keepdims=True) + RMS_EPS) * ng_ref[...]
        gated.append((on * _silu(z[:, h * GDN_DV:(h + 1) * GDN_DV])).astype(BF16))
    mix = _dot(jnp.concatenate(gated, axis=-1), wo)
    return _layer_norm(DN_ALPHA * x + mix, g_ref[...], b_ref[...])


def _pair_index(c):
    ii = lax.broadcasted_iota(jnp.int32, (c, 2 * c), 0)
    lane = lax.broadcasted_iota(jnp.int32, (c, 2 * c), 1)
    return ii, lane % c, lane < c


def _block_diag2(p):
    _, _, left = _pair_index(p.shape[1])
    return jnp.concatenate([jnp.where(left, p, 0.0).astype(BF16), jnp.where(left, 0.0, p).astype(BF16)], axis=1)


def _unit_lower_inverse_pairs(a):
    c = a.shape[1]
    ii, jj, _ = _pair_index(c)
    x = jnp.where(ii == jj, 1.0, 0.0) - jnp.where((ii // 2 == jj // 2) & (ii > jj), a, 0.0)
    s = 4
    while s <= c:
        off = jnp.where((ii // s == jj // s) & (ii // (s // 2) != jj // (s // 2)) & (ii > jj), a, 0.0)
        y = _bdot('bik,bkj->bij', off.astype(BF16), _block_diag2(x))
        x = x - _bdot('bik,bkj->bij', x.astype(BF16), _block_diag2(y))
        s *= 2
    return x


def _gdn_prompt_kernel(x_ref, wqkvz_ref, wba_ref, cw_ref, alog_ref, dtb_ref, ng_ref, wo_ref, g_ref, b_ref,
                       o_ref, conv_ref, sout_ref,
                       ext_s, act_s, s_ref, q_s, k_s, kb_s, vbk_s, kg_s, wkqg_s, u_s, at_s, dec_s, gl_s,
                       o_s, *, tm):
    c = GDN_CHUNK
    nc = tm // c
    ti = pl.program_id(1)
    nslab = GDN_CONV_CH // LANES

    @pl.when(ti == 0)
    def _():
        ext_s[:, 0:CONV_HALO, :] = jnp.zeros((nslab, CONV_HALO, LANES), F32)
        s_ref[...] = jnp.zeros(s_ref.shape, F32)

    x = x_ref[...]
    xb = x.astype(BF16)

    def project(p):
        for part in range(3):
            slab = part * HEADS + 2 * p
            pre = _dot(xb, wqkvz_ref[:, slab * DK:(slab + 2) * DK])
            ext_s[slab, CONV_HALO:CONV_HALO + tm, :] = pre[:, 0:DK]
            ext_s[slab + 1, CONV_HALO:CONV_HALO + tm, :] = pre[:, DK:]

    def conv_silu(part, h):
        slab = part * HEADS + h
        rows = tm // CONV_PHASES
        for ph in range(CONV_PHASES):
            y = None
            for j in range(GDN_CONV):
                start = CONV_HALO + ph - (GDN_CONV - 1 - j)
                term = ext_s[slab, pl.ds(start, rows, stride=CONV_PHASES), :] * cw_ref[j:j + 1, slab * DK:(slab + 1) * DK]
                y = term if y is None else y + term
            act_s[h % 2, part, pl.ds(ph, rows, stride=CONV_PHASES), :] = _silu(y)

    beta_all, gl = _gdn_gates(_dot(xb, wba_ref[...]), alog_ref, dtb_ref)
    pos = lax.broadcasted_iota(jnp.int32, (tm, LANES), 0) % c
    gc = gl
    step = 1
    while step < c:
        gc = gc + jnp.where(pos >= step, pltpu.roll(gc, step, 0), 0.0)
        step *= 2
    gcl = jnp.broadcast_to(gc.reshape(nc, c, LANES)[:, c - 1:c, :], (nc, c, LANES)).reshape(tm, LANES)
    eg = jnp.exp(gc)
    kdec = jnp.exp(gcl - gc)
    glast = jnp.exp(gcl)
    gct = gc.T

    ii, jj, left = _pair_index(c)
    lower = ii >= jj
    for h in range(HEADS):
        if h % 2 == 0:
            project(h // 2)
        for part in range(3):
            conv_silu(part, h)
        for n in range(nc):
            rs = slice(n * c, (n + 1) * c)
            qn = _l2norm(act_s[h % 2, 0, rs, :], DK ** -0.5)
            kn = _l2norm(act_s[h % 2, 1, rs, :])
            vh = act_s[h % 2, 2, rs, :]
            bh = beta_all[rs, h:h + 1]
            egh = eg[rs, HEADS + h:HEADS + h + 1]
            kb = kn * bh
            q_s[h, rs] = qn.astype(BF16)
            k_s[h, rs] = kn.astype(BF16)
            kb_s[h, rs] = kb.astype(BF16)
            vbk_s[h, rs, 0:GDN_DV] = (vh * bh).astype(BF16)
            vbk_s[h, rs, GDN_DV:] = (kb * egh).astype(BF16)
            wkqg_s[h, n, c:, :] = (qn * egh).astype(BF16)
            kg_s[h, rs] = (kn * kdec[rs, HEADS + h:HEADS + h + 1]).astype(BF16)
            gl_s[h, n] = jnp.broadcast_to(glast[n * c:n * c + SUBLANES, HEADS + h:HEADS + h + 1], (SUBLANES, LANES))
        for m in range(nc // 2):
            col = jnp.where(left, gc[2 * m * c:(2 * m + 1) * c, HEADS + h:HEADS + h + 1],
                            gc[(2 * m + 1) * c:(2 * m + 2) * c, HEADS + h:HEADS + h + 1])
            row = gct[HEADS + h:HEADS + h + 1, 2 * m * c:(2 * m + 2) * c]
            dec_s[h, m] = jnp.where(lower, jnp.exp(jnp.where(lower, col - row, 0.0)), 0.0)

    for slab in range(nslab):
        conv_ref[:, slab * DK:(slab + 1) * DK] = ext_s[slab, tm + CONV_HALO - (GDN_CONV - 1):tm + CONV_HALO, :]
    ext_s[:, 0:CONV_HALO, :] = ext_s[:, tm:tm + CONV_HALO, :]

    nb = HEAD_GROUP * nc // 2

    def diag_pair(g):
        return jnp.where(left, g[:, 0:c, :], g[:, c:, :])

    def precompute(gi, carry):
        hs = pl.ds(gi * HEAD_GROUP, HEAD_GROUP)
        dec = dec_s[hs].reshape(nb, c, 2 * c)
        kn2 = k_s[hs].reshape(nb, 2 * c, DK)
        a = diag_pair(_bdot('bid,bjd->bij', kb_s[hs].reshape(nb, 2 * c, DK), kn2))
        tinv = _unit_lower_inverse_pairs(jnp.where(ii > jj, a * dec, 0.0))
        uw = _bdot('bik,bkd->bid', _block_diag2(tinv), vbk_s[hs].reshape(nb, 2 * c, 2 * GDN_DV))
        u_s[hs] = uw[:, :, 0:GDN_DV].reshape(HEAD_GROUP, nc, c, GDN_DV)
        wkqg_s[hs, :, 0:c, :] = uw[:, :, GDN_DV:].astype(BF16).reshape(HEAD_GROUP, nc, c, DK)
        attn = diag_pair(_bdot('bid,bjd->bij', q_s[hs].reshape(nb, 2 * c, DK), kn2)) * dec
        at_s[hs] = attn.astype(BF16).reshape(HEAD_GROUP, nc // 2, c, 2 * c)
        return carry

    lax.fori_loop(0, HEADS // HEAD_GROUP, precompute, 0)

    def chunk_pair(m, carry):
        at2 = at_s[:, m]
        for half in range(2):
            n = 2 * m + half
            sl = pl.ds(pl.multiple_of(n * c, c), c)
            s = s_ref[...]
            r = _bdot('hid,hde->hie', wkqg_s[:, n], s.astype(BF16))
            vnb = (u_s[:, n] - r[:, 0:c, :]).astype(BF16)
            o_s[:, sl, :] = r[:, c:, :] + _bdot('hij,hje->hie', at2[:, :, half * c:(half + 1) * c], vnb)
            s_ref[...] = s * gl_s[:, n][:, 0:1, :] + _bdot('hcd,hce->hde', kg_s[:, sl, :], vnb)
        return carry

    lax.fori_loop(0, nc // 2, chunk_pair, 0)

    z = _dot(xb, wqkvz_ref[:, GDN_CONV_CH:])
    o_ref[...] = _gdn_out([o_s[h] for h in range(HEADS)], z, x, ng_ref, wo_ref[...], g_ref, b_ref)

    @pl.when(ti == pl.num_programs(1) - 1)
    def _():
        sout_ref[...] = s_ref[...]


def _gdn_prompt(x, w, g, b, tm):
    bsz, t, _ = x.shape
    assert t % tm == 0 and tm % (2 * GDN_CHUNK) == 0
    c = GDN_CHUNK
    nc = tm // c
    blk = pl.BlockSpec((None, tm, D_MODEL), lambda bi, ti: (bi, ti, 0))
    consts = (w["wqkvz"], w["wba"], w["conv_w"], w["alog"], w["dtb"], w["ng"], w["wo"], g, b)
    head_bf16 = pltpu.VMEM((HEADS, tm, DK), BF16)
    return pl.pallas_call(
        functools.partial(_gdn_prompt_kernel, tm=tm),
        grid=(bsz, t // tm),
        in_specs=[blk] + [_resident(a.shape) for a in consts],
        out_specs=[blk,
                   pl.BlockSpec((None, GDN_CONV - 1, GDN_CONV_CH), lambda bi, ti: (bi, 0, 0)),
                   pl.BlockSpec((None, HEADS, DK, GDN_DV), lambda bi, ti: (bi, 0, 0, 0))],
        out_shape=[_f32(*x.shape), _f32(bsz, GDN_CONV - 1, GDN_CONV_CH), _f32(bsz, HEADS, DK, GDN_DV)],
        scratch_shapes=[pltpu.VMEM((GDN_CONV_CH // LANES, tm + CONV_HALO, LANES), F32),
                        pltpu.VMEM((2, 3, tm, LANES), F32),
                        pltpu.VMEM((HEADS, DK, GDN_DV), F32),
                        head_bf16, head_bf16, head_bf16,
                        pltpu.VMEM((HEADS, tm, 2 * GDN_DV), BF16),
                        head_bf16,
                        pltpu.VMEM((HEADS, nc, 2 * c, DK), BF16),
                        pltpu.VMEM((HEADS, nc, c, GDN_DV), F32),
                        pltpu.VMEM((HEADS, nc // 2, c, 2 * c), BF16),
                        pltpu.VMEM((HEADS, nc // 2, c, 2 * c), F32),
                        pltpu.VMEM((HEADS, nc, SUBLANES, LANES), F32),
                        pltpu.VMEM((HEADS, tm, GDN_DV), F32)],
        compiler_params=_params(2),
        name="gdn_prompt",
    )(x, *consts)


def _stream_projection(xb, win_ref, wb_ref, proj_s):
    wb = win_ref[...].astype(BF16)
    wb_ref[...] = wb
    proj_s[pl.program_id(0)] = _dot(xb, wb)


def _gdn_sample_a_kernel(x_ref, cb_ref, win_ref, wba_ref, cw_ref, alog_ref, dtb_ref,
                         rows_ref, pre_ref, z_ref, wb_ref, proj_s):
    xb = x_ref[...].astype(BF16)
    _stream_projection(xb, win_ref, wb_ref, proj_s)

    @pl.when(pl.program_id(0) == pl.num_programs(0) - 1)
    def _():
        nq = GDN_CONV_CH // W_COLS
        pre = jnp.concatenate([proj_s[i] for i in range(nq)], axis=-1)
        pre_ref[...] = pre
        z_ref[...] = proj_s[nq]
        y = pre * cw_ref[GDN_CONV - 1:GDN_CONV, :]
        for j in range(GDN_CONV - 1):
            y = y + cb_ref[j] * cw_ref[j:j + 1, :]
        qkv = _silu(y)
        beta_all, gl = _gdn_gates(_dot(xb, wba_ref[...]), alog_ref, dtb_ref)
        eg = jnp.exp(gl)
        rows = x_ref.shape[0]
        rows_ref[...] = jnp.zeros(rows_ref.shape, F32)
        for h in range(HEADS):
            hs = slice(h * DK, (h + 1) * DK)
            qn = _l2norm(qkv[:, hs], DK ** -0.5)
            kn = _l2norm(qkv[:, HEADS * DK + h * DK:HEADS * DK + (h + 1) * DK])
            vh = qkv[:, 2 * HEADS * DK + h * GDN_DV:2 * HEADS * DK + (h + 1) * GDN_DV]
            bh = beta_all[:, h:h + 1]
            egh = eg[:, HEADS + h:HEADS + h + 1]
            rows_ref[:, h, 0, :] = kn
            rows_ref[:, h, 1, :] = qn
            rows_ref[:, h, 2, :] = vh * bh
            rows_ref[:, h, 3, :] = jnp.broadcast_to(bh * egh, (rows, DK))
            rows_ref[:, h, 4, :] = jnp.broadcast_to(egh, (rows, DK))
            rows_ref[:, h, 5, :] = jnp.broadcast_to(jnp.sum(qn * kn, -1, keepdims=True), (rows, DK))


def _gdn_sample_b_kernel(p_ref, s_ref, o_ref, sout_ref):
    bb = p_ref.shape[0]
    n = bb * HEADS
    p = p_ref[...].reshape(n, SUBLANES, DK)
    s = s_ref[...].reshape(n, DK, GDN_DV)
    pb = p.astype(BF16)
    r = _bdot('nqd,nde->nqe', pb, s.astype(BF16))
    v_new = p[:, 2:3, :] - p[:, 3:4, :] * r[:, 0:1, :]
    o = p[:, 4:5, :] * r[:, 1:2, :] + p[:, 5:6, :] * v_new
    row0 = lax.broadcasted_iota(jnp.int32, (n, SUBLANES, GDN_DV), 1) == 0
    vn8 = jnp.where(row0, jnp.broadcast_to(v_new, (n, SUBLANES, GDN_DV)), 0.0)
    s_new = s * p[:, 4:5, :] + _bdot('nqd,nqe->nde', pb, vn8.astype(BF16))
    o_ref[...] = jnp.broadcast_to(o, (n, SUBLANES, GDN_DV)).reshape(bb, HEADS, SUBLANES, GDN_DV)
    sout_ref[...] = s_new.reshape(bb, HEADS, DK, GDN_DV)


def _gdn_sample_c_kernel(o_ref, z_ref, x_ref, ng_ref, wo_ref, g_ref, b_ref, out_ref, wob_ref):
    wob = wo_ref[...].astype(BF16)
    wob_ref[...] = wob
    out_ref[...] = _gdn_out([o_ref[:, h, 0, :] for h in range(HEADS)],
                            z_ref[...], x_ref[...], ng_ref, wob, g_ref, b_ref)


def _projection_call(kernel, resident_args, w_in_all, layer, n_steps, n_cols, outs, rows, name):
    return pl.pallas_call(
        kernel,
        grid=(n_steps,),
        in_specs=[_resident(a.shape) for a in resident_args[:2]]
        + [pl.BlockSpec((None, D_MODEL, W_COLS), lambda c: (layer, 0, c))]
        + [_resident(a.shape) for a in resident_args[2:]],
        out_specs=[_whole(o.shape) for o in outs] + [pl.BlockSpec((D_MODEL, W_COLS), lambda c: (0, c))],
        out_shape=list(outs) + [_bf16(D_MODEL, n_cols)],
        scratch_shapes=[pltpu.VMEM((n_steps, rows, W_COLS), F32)],
        compiler_params=_params(1),
        name=name,
    )(*resident_args[:2], w_in_all, *resident_args[2:])


def _state_step(kernel, p, extra, s0, bb, name):
    bsz = s0.shape[0]
    assert bsz % bb == 0

    def blk(a):
        return pl.BlockSpec((bb,) + a.shape[1:], lambda i: (i,) + (0,) * (a.ndim - 1))

    ins = (p,) + tuple(extra) + (s0,)
    o_shape = (bsz, HEADS, SUBLANES, s0.shape[-1])
    return pl.pallas_call(
        kernel,
        grid=(bsz // bb,),
        in_specs=[blk(a) for a in ins],
        out_specs=[pl.BlockSpec((bb,) + o_shape[1:], lambda i: (i, 0, 0, 0)), blk(s0)],
        out_shape=[_f32(*o_shape), _f32(*s0.shape)],
        compiler_params=_params(1),
        name=name,
    )(*ins)


def _gdn_sample(x2d, conv_buf, s0, w_in_all, w_out, layer, w, g, b):
    bsz = x2d.shape[0]
    cb_t = conv_buf.transpose(1, 0, 2)
    rows, pre, z, wqkvz = _projection_call(
        _gdn_sample_a_kernel, (x2d, cb_t, w["wba"], w["conv_w"], w["alog"], w["dtb"]), w_in_all, layer,
        GDN_QKVZ // W_COLS, GDN_QKVZ,
        [_f32(bsz, HEADS, SUBLANES, DK), _f32(bsz, GDN_CONV_CH), _f32(bsz, D_MODEL)], bsz, "gdn_sample_a")
    o8, s_new = _state_step(_gdn_sample_b_kernel, rows, (), s0, 8, "gdn_sample_b")
    x1, wob = _whole_call(_gdn_sample_c_kernel, (o8, z, x2d, w["ng"], w_out, g, b),
                          [_f32(bsz, D_MODEL), _bf16(*w_out.shape)], "gdn_sample_c")
    new_conv = jnp.concatenate([conv_buf[:, 1:], pre[:, None, :]], axis=1)
    return x1, new_conv, s_new, wqkvz, wob


def _log_gamma(h):
    return math.log(1.0 - 2.0 ** (-5.0 - h))


def _rope_tables(pos, freq_ref):
    ang = pos * freq_ref[...]
    lane = lax.broadcasted_iota(jnp.int32, ang.shape, 1)
    return jnp.cos(ang), jnp.where(lane < DK // 2, -1.0, 1.0) * jnp.sin(ang)


def _rope(v, cos2, sin2):
    return v * cos2 + pltpu.roll(v, DK // 2, 1) * sin2


def _ret_out(o_heads, gate, x, wo, g_ref, b_ref):
    gated = []
    for h, oh in enumerate(o_heads):
        mu = jnp.mean(oh, -1, keepdims=True)
        oc = oh - mu
        on = oc * lax.rsqrt(jnp.mean(oc * oc, -1, keepdims=True) + LN_EPS)
        gated.append((_silu(gate[:, h * RET_DV:(h + 1) * RET_DV]) * on).astype(BF16))
    mix = _dot(jnp.concatenate(gated, axis=-1), wo)
    return _layer_norm(DN_ALPHA * x + mix, g_ref[...], b_ref[...])


def _ret_prompt_kernel(x_ref, win_ref, freq_ref, wo_ref, g_ref, b_ref, o_ref, sout_ref,
                       s_ref, q_s, k_s, qx_s, kz_s, v_s, dec_s, o_s, *, tm, c, pos0):
    nc = tm // c
    ti = pl.program_id(1)
    hk = HEADS * DK

    @pl.when(ti == 0)
    def _():
        s_ref[...] = jnp.zeros(s_ref.shape, F32)
        diff = (lax.broadcasted_iota(jnp.int32, (c, c), 0) - lax.broadcasted_iota(jnp.int32, (c, c), 1)).astype(F32)
        for h in range(HEADS):
            dec_s[h] = jnp.where(diff >= 0, jnp.exp(jnp.maximum(diff, 0.0) * _log_gamma(h)), 0.0)

    x = x_ref[...]
    xb = x.astype(BF16)
    row = lax.broadcasted_iota(jnp.int32, (tm, 1), 0)
    cos2, sin2 = _rope_tables((pos0 + ti * tm + row).astype(F32), freq_ref)
    idx = (row % c).astype(F32)
    qall = _dot(xb, win_ref[:, 0:hk])
    kall = _dot(xb, win_ref[:, hk:2 * hk])
    vall = _dot(xb, win_ref[:, 2 * hk:2 * hk + HEADS * RET_DV])
    for h in range(HEADS):
        hs = slice(h * DK, (h + 1) * DK)
        qr = _rope(qall[:, hs], cos2, sin2)
        kr = _rope(kall[:, hs], cos2, sin2) * DK ** -0.5
        q_s[h] = qr.astype(BF16)
        k_s[h] = kr.astype(BF16)
        qx_s[h] = (qr * jnp.exp((idx + 1.0) * _log_gamma(h))).astype(BF16)
        kz_s[h] = (kr * jnp.exp((c - 1.0 - idx) * _log_gamma(h))).astype(BF16)
        v_s[h] = vall[:, h * RET_DV:(h + 1) * RET_DV].astype(BF16)

    for n in range(nc):
        sl = slice(n * c, (n + 1) * c)
        v_c = v_s[:, sl, :]
        attn = _bdot('hid,hjd->hij', q_s[:, sl, :], k_s[:, sl, :]) * dec_s[...]
        s = s_ref[...]
        o_s[:, sl, :] = (_bdot('hij,hje->hie', attn.astype(BF16), v_c)
                         + _bdot('hcd,hde->hce', qx_s[:, sl, :], s.astype(BF16)))
        upd = _bdot('hcd,hce->hde', kz_s[:, sl, :], v_c)
        for h in range(HEADS):
            s_ref[h] = math.exp(c * _log_gamma(h)) * s[h] + upd[h]

    gate = _dot(xb, win_ref[:, 2 * hk + HEADS * RET_DV:])
    o_ref[...] = _ret_out([o_s[h] for h in range(HEADS)], gate, x, wo_ref[...], g_ref, b_ref)

    @pl.when(ti == pl.num_programs(1) - 1)
    def _():
        sout_ref[...] = s_ref[...]


def _ret_prompt(x, w, g, b, tm):
    bsz, t, _ = x.shape
    c = min(RET_CHUNK, tm)
    assert t % tm == 0 and tm % c == 0
    blk = pl.BlockSpec((None, tm, D_MODEL), lambda bi, ti: (bi, ti, 0))
    consts = (w["win"], w["freq"], w["wo"], g, b)
    head_k = pltpu.VMEM((HEADS, tm, DK), BF16)
    return pl.pallas_call(
        functools.partial(_ret_prompt_kernel, tm=tm, c=c, pos0=0),
        grid=(bsz, t // tm),
        in_specs=[blk] + [_resident(a.shape) for a in consts],
        out_specs=[blk, pl.BlockSpec((None, HEADS, DK, RET_DV), lambda bi, ti: (bi, 0, 0, 0))],
        out_shape=[_f32(*x.shape), _f32(bsz, HEADS, DK, RET_DV)],
        scratch_shapes=[pltpu.VMEM((HEADS, DK, RET_DV), F32), head_k, head_k, head_k, head_k,
                        pltpu.VMEM((HEADS, tm, RET_DV), BF16),
                        pltpu.VMEM((HEADS, c, c), F32),
                        pltpu.VMEM((HEADS, tm, RET_DV), F32)],
        compiler_params=_params(2),
        name="ret_prompt",
    )(x, *consts)


def _ret_sample_a_kernel(x_ref, freq_ref, win_ref, kq_ref, vrows_ref, gate_ref, wb_ref, proj_s, *, pos0):
    xb = x_ref[...].astype(BF16)
    _stream_projection(xb, win_ref, wb_ref, proj_s)

    @pl.when(pl.program_id(0) == pl.num_programs(0) - 1)
    def _():
        rows = x_ref.shape[0]
        cos2, sin2 = _rope_tables(jnp.full((1, 1), pos0, F32), freq_ref)
        qall, kall = proj_s[0], proj_s[1]
        gate_ref[...] = jnp.concatenate([proj_s[4], proj_s[5]], axis=-1)
        kq_ref[...] = jnp.zeros(kq_ref.shape, F32)
        vrows_ref[...] = jnp.zeros(vrows_ref.shape, F32)
        for h in range(HEADS):
            hs = slice(h * DK, (h + 1) * DK)
            qr = _rope(qall[:, hs], cos2, sin2)
            kr = _rope(kall[:, hs], cos2, sin2) * DK ** -0.5
            kq_ref[:, h, 0, :] = kr
            kq_ref[:, h, 1, :] = qr
            vh = proj_s[2 + h // 4][:, (h % 4) * RET_DV:(h % 4 + 1) * RET_DV]
            vrows_ref[:, h, 0, :] = vh
            vrows_ref[:, h, 1, :] = jnp.broadcast_to(jnp.sum(qr * kr, -1, keepdims=True), (rows, RET_DV))
            vrows_ref[:, h, 2, :] = jnp.full((rows, RET_DV), math.exp(_log_gamma(h)), F32)


def _ret_sample_b_kernel(kq_ref, v_ref, s_ref, o_ref, sout_ref):
    bb = kq_ref.shape[0]
    n = bb * HEADS
    kqb = kq_ref[...].reshape(n, SUBLANES, DK).astype(BF16)
    v8 = v_ref[...].reshape(n, SUBLANES, RET_DV)
    s = s_ref[...].reshape(n, DK, RET_DV)
    r = _bdot('nqd,nde->nqe', kqb, s.astype(BF16))
    gamma = v8[:, 2:3, :]
    o = v8[:, 1:2, :] * v8[:, 0:1, :] + gamma * r[:, 1:2, :]
    row0 = lax.broadcasted_iota(jnp.int32, (n, SUBLANES, RET_DV), 1) == 0
    s_new = s * gamma + _bdot('nqd,nqe->nde', kqb, jnp.where(row0, v8, 0.0).astype(BF16))
    o_ref[...] = jnp.broadcast_to(o, (n, SUBLANES, RET_DV)).reshape(bb, HEADS, SUBLANES, RET_DV)
    sout_ref[...] = s_new.reshape(bb, HEADS, DK, RET_DV)


def _ret_sample_c_kernel(o_ref, gate_ref, x_ref, wo_ref, g_ref, b_ref, out_ref, wob_ref):
    wob = wo_ref[...].astype(BF16)
    wob_ref[...] = wob
    out_ref[...] = _ret_out([o_ref[:, h, 0, :] for h in range(HEADS)],
                            gate_ref[...], x_ref[...], wob, g_ref, b_ref)


def _ret_sample(x2d, s0, w_in_all, w_out, layer, freq, g, b):
    bsz = x2d.shape[0]
    kq, vrows, gate, winb = _projection_call(
        functools.partial(_ret_sample_a_kernel, pos0=float(PAST_LEN)), (x2d, freq), w_in_all, layer,
        RET_IN // W_COLS, RET_IN,
        [_f32(bsz, HEADS, SUBLANES, DK), _f32(bsz, HEADS, SUBLANES, RET_DV), _f32(bsz, HEADS * RET_DV)],
        bsz, "ret_sample_a")
    o8, s_new = _state_step(_ret_sample_b_kernel, kq, (vrows,), s0, 4, "ret_sample_b")
    x1, wob = _whole_call(_ret_sample_c_kernel, (o8, gate, x2d, w_out, g, b),
                          [_f32(bsz, D_MODEL), _bf16(*w_out.shape)], "ret_sample_c")
    return x1, s_new, winb, wob


def _lane_row(values, offset):
    return jnp.zeros((1, LANES), F32).at[0, offset:offset + HEADS].set(values.astype(F32))


def _gdn_small_weights(w_in, conv_w, a_log, dt_bias, norm_g):
    wba = jnp.zeros((D_MODEL, LANES), F32).at[:, :2 * HEADS].set(w_in[:, GDN_QKVZ:])
    return {"wba": wba.astype(BF16), "conv_w": conv_w,
            "alog": _lane_row(a_log, HEADS), "dtb": _lane_row(dt_bias, HEADS), "ng": norm_g.reshape(1, GDN_DV)}


def _rope_freqs():
    half = DK // 2
    freqs = ROPE_BASE ** (-jnp.arange(half, dtype=F32) / half)
    return jnp.concatenate([freqs, freqs]).reshape(1, DK)


def kernel(x_prompt, x_sample, state_pool, state_gdn_conv, state_gdn, state_ret, pool_w, pool_scale, gdn_w_in,
           gdn_conv_w, gdn_a_log, gdn_dt_bias, gdn_norm_g, gdn_w_out, ret_w_in, ret_w_out, ffn_w13, ffn_w2,
           ln_g, ln_b):
    bsz, seq, _ = x_prompt.shape
    dec = x_sample.shape[0]
    tm = min(512, seq)
    xp = x_prompt
    xs = x_sample.reshape(dec, D_MODEL)
    pool_p, pool_s, conv_p, conv_s, gdn_p, gdn_s, ret_p, ret_s = [], [], [], [], [], [], [], []
    for i in range(DEPTH):
        kind, j = i % N_MIXERS, i // N_MIXERS
        g1, b1 = ln_g[i, 0].reshape(1, D_MODEL), ln_b[i, 0].reshape(1, D_MODEL)
        g2, b2 = ln_g[i, 1].reshape(1, D_MODEL), ln_b[i, 1].reshape(1, D_MODEL)
        if kind == 0:
            sc = pool_scale[j].reshape(1, D_MODEL)
            pool_p.append(xp[:, seq - POOL_BUF:])
            xs, pwb, nbuf = _pool_sample(xs, state_pool[j].transpose(1, 0, 2), pool_w[j], sc, g1, b1)
            pool_s.append(nbuf)
            xs, *fw = _ffn_cast(xs, ffn_w13, ffn_w2, i, g2, b2)
            xp = _pool_ffn_prompt(xp, pwb, sc, g1, b1, fw, g2, b2, min(2 * tm, seq), tm)
            continue
        if kind == 1:
            w = _gdn_small_weights(gdn_w_in[j], gdn_conv_w[j], gdn_a_log[j], gdn_dt_bias[j], gdn_norm_g[j])
            xs, cs, ss, w["wqkvz"], w["wo"] = _gdn_sample(xs, state_gdn_conv[j], state_gdn[j], gdn_w_in,
                                                         gdn_w_out[j], j, w, g1, b1)
            xp, cp, sp = _gdn_prompt(xp, w, g1, b1, tm)
            conv_p.append(cp), conv_s.append(cs), gdn_p.append(sp), gdn_s.append(ss)
        else:
            w = {"freq": _rope_freqs()}
            xs, ss, w["win"], w["wo"] = _ret_sample(xs, state_ret[j], ret_w_in, ret_w_out[j], j, w["freq"], g1, b1)
            xp, sp = _ret_prompt(xp, w, g1, b1, tm)
            ret_p.append(sp), ret_s.append(ss)
        xs, *fw = _ffn_cast(xs, ffn_w13, ffn_w2, i, g2, b2)
        xp = _ffn(xp.reshape(bsz * seq, D_MODEL), fw, g2, b2, min(2 * tm, seq), tm).reshape(bsz, seq, D_MODEL)
    return (xp, xs.reshape(dec, 1, D_MODEL), jnp.stack(pool_p), jnp.stack(pool_s).transpose(0, 2, 1, 3), jnp.stack(conv_p),
            jnp.stack(conv_s), jnp.stack(gdn_p), jnp.stack(gdn_s), jnp.stack(ret_p), jnp.stack(ret_s))
```

```python
import functools
import math

import jax
import jax.numpy as jnp
from jax import lax
from jax.experimental import pallas as pl
from jax.experimental.pallas import tpu as pltpu

F32 = jnp.float32
BF16 = jnp.bfloat16

D_MODEL = 1024
DEPTH = 4
PAST_LEN = 16384
N_MIXERS = 3

POOL_WINDOWS = (2, 4, 8, 16)
POOL_GROUP = D_MODEL // len(POOL_WINDOWS)
POOL_BUF = max(POOL_WINDOWS) - 1
POOL_HALO = 16
POOL_ROWS = 64

HEADS = 8
DK = D_MODEL // HEADS
GDN_DV = DK
GDN_CONV = 4
GDN_CONV_CH = HEADS * (2 * DK + GDN_DV)
GDN_QKVZ = GDN_CONV_CH + HEADS * GDN_DV
CONV_HALO = 8
CONV_PHASES = 4
GDN_CHUNK = 64
HEAD_GROUP = 8
RET_DV = 2 * DK
RET_IN = 2 * HEADS * DK + 2 * HEADS * RET_DV
RET_CHUNK = 256
ROPE_BASE = 10000.0
W_COLS = 1024

D_FF = -(-8 * D_MODEL // (3 * 256)) * 256
FF_CHUNK = 256
CAST_CHUNK = 256
DN_ALPHA = (2 * DEPTH) ** 0.25
LN_EPS = 1e-5
RMS_EPS = 1e-6

SUBLANES = 8
LANES = 128
VMEM_LIMIT = 56 * 1024 * 1024


def _params(n_grid):
    return pltpu.CompilerParams(dimension_semantics=("arbitrary",) * n_grid, vmem_limit_bytes=VMEM_LIMIT)


def _resident(shape):
    nd = len(shape)
    return pl.BlockSpec(shape, lambda *_: (0,) * nd, pipeline_mode=pl.Buffered(1))


def _whole(shape):
    nd = len(shape)
    return pl.BlockSpec(shape, lambda *_: (0,) * nd)


def _dot(a, b):
    return jnp.dot(a, b, preferred_element_type=F32)


def _bdot(spec, a, b):
    return jnp.einsum(spec, a, b, preferred_element_type=F32)


def _silu(x):
    return x * jax.nn.sigmoid(x)


def _layer_norm(y, g, b):
    mu = jnp.mean(y, -1, keepdims=True)
    yc = y - mu
    var = jnp.mean(yc * yc, -1, keepdims=True)
    return yc * lax.rsqrt(var + LN_EPS) * g + b


def _f32(*shape):
    return jax.ShapeDtypeStruct(shape, F32)


def _bf16(*shape):
    return jax.ShapeDtypeStruct(shape, BF16)


def _ffn_tail(x, w1_ref, w3_ref, w2_ref, g_ref, b_ref):
    xb = x.astype(BF16)
    acc = None
    for c in range(D_FF // FF_CHUNK):
        lo = c * FF_CHUNK
        a = _dot(xb, w1_ref[:, lo:lo + FF_CHUNK])
        b = _dot(xb, w3_ref[:, lo:lo + FF_CHUNK])
        h = (_silu(a) * b).astype(BF16)
        p = _dot(h, w2_ref[lo:lo + FF_CHUNK, :])
        acc = p if acc is None else acc + p
    return _layer_norm(DN_ALPHA * x + acc, g_ref[...], b_ref[...])


def _ffn_kernel(x_ref, w1_ref, w3_ref, w2_ref, g_ref, b_ref, o_ref, *, sub):
    for r0 in range(0, x_ref.shape[0], sub):
        o_ref[r0:r0 + sub, :] = _ffn_tail(x_ref[r0:r0 + sub, :], w1_ref, w3_ref, w2_ref, g_ref, b_ref)


def _ffn(x2d, fw, g, b, tm, sub):
    n = x2d.shape[0]
    assert n % tm == 0 and tm % sub == 0
    row = pl.BlockSpec((tm, D_MODEL), lambda i: (i, 0))
    consts = tuple(fw) + (g, b)
    return pl.pallas_call(
        functools.partial(_ffn_kernel, sub=sub),
        grid=(n // tm,),
        in_specs=[row] + [_resident(a.shape) for a in consts],
        out_specs=row,
        out_shape=_f32(n, D_MODEL),
        compiler_params=_params(1),
        name="ffn",
    )(x2d, *consts)


def _ffn_cast_kernel(x_ref, w1_ref, w3_ref, w2_ref, g_ref, b_ref, o_ref, w1b_ref, w3b_ref, w2b_ref, acc_ref):
    c = pl.program_id(0)

    @pl.when(c == 0)
    def _():
        acc_ref[...] = jnp.zeros(acc_ref.shape, F32)

    xb = x_ref[...].astype(BF16)
    w1b, w3b, w2b = w1_ref[...].astype(BF16), w3_ref[...].astype(BF16), w2_ref[...].astype(BF16)
    w1b_ref[...], w3b_ref[...], w2b_ref[...] = w1b, w3b, w2b
    h = (_silu(_dot(xb, w1b)) * _dot(xb, w3b)).astype(BF16)
    acc_ref[...] += _dot(h, w2b)

    @pl.when(c == pl.num_programs(0) - 1)
    def _():
        o_ref[...] = _layer_norm(DN_ALPHA * x_ref[...] + acc_ref[...], g_ref[...], b_ref[...])


def _ffn_cast(x2d, w13_all, w2_all, layer, g, b):
    n = x2d.shape[0]
    cc = CAST_CHUNK
    nch = D_FF // cc
    return pl.pallas_call(
        _ffn_cast_kernel,
        grid=(nch,),
        in_specs=[_resident(x2d.shape),
                  pl.BlockSpec((None, D_MODEL, cc), lambda c: (layer, 0, c)),
                  pl.BlockSpec((None, D_MODEL, cc), lambda c: (layer, 0, nch + c)),
                  pl.BlockSpec((None, cc, D_MODEL), lambda c: (layer, c, 0)),
                  _resident(g.shape), _resident(b.shape)],
        out_specs=[_whole(x2d.shape),
                   pl.BlockSpec((D_MODEL, cc), lambda c: (0, c)),
                   pl.BlockSpec((D_MODEL, cc), lambda c: (0, c)),
                   pl.BlockSpec((cc, D_MODEL), lambda c: (c, 0))],
        out_shape=[_f32(n, D_MODEL), _bf16(D_MODEL, D_FF), _bf16(D_MODEL, D_FF), _bf16(D_FF, D_MODEL)],
        scratch_shapes=[pltpu.VMEM((n, D_MODEL), F32)],
        compiler_params=_params(1),
        name="ffn_cast",
    )(x2d, w13_all, w13_all, w2_all, g, b)


def _rows_up(v, s):
    rows, w = v.shape
    if s % SUBLANES:
        sub = lax.broadcasted_iota(jnp.int32, v.shape, 0) % SUBLANES
        v = pltpu.roll(v.reshape(rows // SUBLANES, SUBLANES, w), s % SUBLANES, 1).reshape(rows, w)
        v = jnp.where(sub >= s % SUBLANES, v, jnp.concatenate([v[0:SUBLANES], v[0:rows - SUBLANES]], axis=0))
    whole = s - s % SUBLANES
    return jnp.concatenate([v[0:whole], v[0:rows - whole]], axis=0) if whole else v


def _trailing_sums(e, win):
    s, w = e, 1
    while w < win:
        s = s + _rows_up(s, w)
        w *= 2
    return s


def _pool_ffn_prompt_kernel(x_ref, pw_ref, sc_ref, g1_ref, b1_ref, w1_ref, w3_ref, w2_ref, g2_ref, b2_ref,
                            o_ref, ext_ref, pooled_ref, *, tm, sub, n_valid):
    ti = pl.program_id(1)
    rb = POOL_ROWS

    @pl.when(ti == 0)
    def _():
        ext_ref[0:POOL_HALO, :] = jnp.zeros((POOL_HALO, D_MODEL), F32)

    ext_ref[POOL_HALO:POOL_HALO + tm, :] = x_ref[...]

    def pool_rows(r0):
        t = ti * tm + r0 + lax.broadcasted_iota(jnp.int32, (rb, 1), 0)
        for gi, win in enumerate(POOL_WINDOWS):
            lo, hi = gi * POOL_GROUP, (gi + 1) * POOL_GROUP
            e = ext_ref[r0:r0 + rb + POOL_HALO, lo:hi]
            cnt = jnp.minimum(t + 1 + n_valid, win).astype(F32)
            pooled = _trailing_sums(e, win)[POOL_HALO:] / cnt - e[POOL_HALO:]
            pooled_ref[r0:r0 + rb, lo:hi] = pooled.astype(BF16)

    for r0 in range(0, tm, rb):
        pool_rows(r0)
    ext_ref[0:POOL_HALO, :] = ext_ref[tm:tm + POOL_HALO, :]
    for s0 in range(0, tm, sub):
        mix = jnp.concatenate(
            [_dot(pooled_ref[s0:s0 + sub, gi * POOL_GROUP:(gi + 1) * POOL_GROUP], pw_ref[gi])
             for gi in range(len(POOL_WINDOWS))], axis=-1) * sc_ref[...]
        x1 = _layer_norm(DN_ALPHA * x_ref[s0:s0 + sub, :] + mix, g1_ref[...], b1_ref[...])
        o_ref[s0:s0 + sub, :] = _ffn_tail(x1, w1_ref, w3_ref, w2_ref, g2_ref, b2_ref)


def _pool_ffn_prompt(x, pwb, sc, g1, b1, fw, g2, b2, tm, sub):
    bsz, t, _ = x.shape
    assert t % tm == 0 and tm % sub == 0 and sub % POOL_ROWS == 0
    blk = pl.BlockSpec((None, tm, D_MODEL), lambda bi, ti: (bi, ti, 0))
    consts = (pwb, sc, g1, b1) + tuple(fw) + (g2, b2)
    return pl.pallas_call(
        functools.partial(_pool_ffn_prompt_kernel, tm=tm, sub=sub, n_valid=0),
        grid=(bsz, t // tm),
        in_specs=[blk] + [_resident(a.shape) for a in consts],
        out_specs=blk,
        out_shape=_f32(*x.shape),
        scratch_shapes=[pltpu.VMEM((tm + POOL_HALO, D_MODEL), F32), pltpu.VMEM((tm, D_MODEL), BF16)],
        compiler_params=_params(2),
        name="pool_ffn_prompt",
    )(x, *consts)


def _pool_sample_kernel(x_ref, buf_ref, pw_ref, sc_ref, g_ref, b_ref, o_ref, pwb_ref, nbuf_ref, *, n_valid):
    x = x_ref[...]
    pwb = pw_ref[...].astype(BF16)
    pwb_ref[...] = pwb
    outs = []
    for gi, win in enumerate(POOL_WINDOWS):
        lo, hi = gi * POOL_GROUP, (gi + 1) * POOL_GROUP
        xs = x[:, lo:hi]
        s = xs
        for d in range(1, win):
            s = s + buf_ref[POOL_BUF - d, :, lo:hi]
        pooled = s / float(min(1 + n_valid, win)) - xs
        outs.append(_dot(pooled.astype(BF16), pwb[gi]))
    mix = jnp.concatenate(outs, axis=-1) * sc_ref[...]
    o_ref[...] = _layer_norm(DN_ALPHA * x + mix, g_ref[...], b_ref[...])
    nbuf_ref[0:POOL_BUF - 1] = buf_ref[1:POOL_BUF]
    nbuf_ref[POOL_BUF - 1] = x


def _whole_call(kernel, args, outs, name):
    return pl.pallas_call(
        kernel,
        grid=(1,),
        in_specs=[_resident(a.shape) for a in args],
        out_specs=[_whole(o.shape) for o in outs],
        out_shape=list(outs),
        compiler_params=_params(1),
        name=name,
    )(*args)


def _pool_sample(x2d, buf, pw, sc, g, b):
    return _whole_call(functools.partial(_pool_sample_kernel, n_valid=min(PAST_LEN, POOL_BUF)),
                       (x2d, buf, pw, sc, g, b), [_f32(*x2d.shape), _bf16(*pw.shape), _f32(*buf.shape)],
                       "pool_sample")


def _softplus(x):
    return jnp.maximum(x, 0.0) + jnp.log1p(jnp.exp(-jnp.abs(x)))


def _gdn_gates(ba, alog_ref, dtb_ref):
    return jax.nn.sigmoid(ba), -jnp.exp(alog_ref[...]) * _softplus(ba + dtb_ref[...])


def _l2norm(v, scale=1.0):
    return v * (lax.rsqrt(jnp.sum(v * v, -1, keepdims=True) + RMS_EPS) * scale)


def _gdn_out(o_heads, z, x, ng_ref, wo, g_ref, b_ref):
    gated = []
    for h, oh in enumerate(o_heads):
        on = oh * lax.rsqrt(jnp.mean(oh * oh, -1, keepdims=True) + RMS_EPS) * ng_ref[...]
        gated.append((on * _silu(z[:, h * GDN_DV:(h + 1) * GDN_DV])).astype(BF16))
    mix = _dot(jnp.concatenate(gated, axis=-1), wo)
    return _layer_norm(DN_ALPHA * x + mix, g_ref[...], b_ref[...])


def _pair_index(c):
    ii = lax.broadcasted_iota(jnp.int32, (c, 2 * c), 0)
    lane = lax.broadcasted_iota(jnp.int32, (c, 2 * c), 1)
    return ii, lane % c, lane < c


def _block_diag2(p):
    _, _, left = _pair_index(p.shape[1])
    return jnp.concatenate([jnp.where(left, p, 0.0).astype(BF16), jnp.where(left, 0.0, p).astype(BF16)], axis=1)


def _unit_lower_inverse_pairs(a):
    c = a.shape[1]
    ii, jj, _ = _pair_index(c)
    x = jnp.where(ii == jj, 1.0, 0.0) - jnp.where((ii // 2 == jj // 2) & (ii > jj), a, 0.0)
    s = 4
    while s <= c:
        off = jnp.where((ii // s == jj // s) & (ii // (s // 2) != jj // (s // 2)) & (ii > jj), a, 0.0)
        y = _bdot('bik,bkj->bij', off.astype(BF16), _block_diag2(x))
        x = x - _bdot('bik,bkj->bij', x.astype(BF16), _block_diag2(y))
        s *= 2
    return x


def _gdn_prompt_kernel(x_ref, wqkvz_ref, wba_ref, cw_ref, alog_ref, dtb_ref, ng_ref, wo_ref, g_ref, b_ref,
                       o_ref, conv_ref, sout_ref,
                       ext_s, act_s, s_ref, q_s, k_s, kb_s, vbk_s, kg_s, wkqg_s, u_s, at_s, dec_s, gl_s,
                       o_s, *, tm):
    c = GDN_CHUNK
    nc = tm // c
    ti = pl.program_id(1)
    nslab = GDN_CONV_CH // LANES

    @pl.when(ti == 0)
    def _():
        ext_s[:, 0:CONV_HALO, :] = jnp.zeros((nslab, CONV_HALO, LANES), F32)
        s_ref[...] = jnp.zeros(s_ref.shape, F32)

    x = x_ref[...]
    xb = x.astype(BF16)

    def project(p):
        for part in range(3):
            slab = part * HEADS + 2 * p
            pre = _dot_t(xb, wqkvz_ref[slab * DK:(slab + 2) * DK, :])
            ext_s[slab, CONV_HALO:CONV_HALO + tm, :] = pre[:, 0:DK]
            ext_s[slab + 1, CONV_HALO:CONV_HALO + tm, :] = pre[:, DK:]

    def conv_silu(part, h):
        slab = part * HEADS + h
        rows = tm // CONV_PHASES
        for ph in range(CONV_PHASES):
            y = None
            for j in range(GDN_CONV):
                start = CONV_HALO + ph - (GDN_CONV - 1 - j)
                term = ext_s[slab, pl.ds(start, rows, stride=CONV_PHASES), :] * cw_ref[j:j + 1, slab * DK:(slab + 1) * DK]
                y = term if y is None else y + term
            act_s[part, pl.ds(ph, rows, stride=CONV_PHASES), :] = _silu(y)

    beta_all, gl = _gdn_gates(_dot(xb, wba_ref[...]), alog_ref, dtb_ref)
    pos = lax.broadcasted_iota(jnp.int32, (tm, LANES), 0) % c
    gc = gl
    step = 1
    while step < c:
        gc = gc + jnp.where(pos >= step, pltpu.roll(gc, step, 0), 0.0)
        step *= 2
    gcl = jnp.broadcast_to(gc.reshape(nc, c, LANES)[:, c - 1:c, :], (nc, c, LANES)).reshape(tm, LANES)
    eg = jnp.exp(gc)
    kdec = jnp.exp(gcl - gc)
    glast = jnp.exp(gcl)
    gct = gc.T

    ii, jj, left = _pair_index(c)
    lower = ii >= jj
    for h in range(HEADS):
        if h % 2 == 0:
            project(h // 2)
        for part in range(3):
            conv_silu(part, h)
        for n in range(nc):
            rs = slice(n * c, (n + 1) * c)
            qn = _l2norm(act_s[0, rs, :], DK ** -0.5)
            kn = _l2norm(act_s[1, rs, :])
            vh = act_s[2, rs, :]
            bh = beta_all[rs, h:h + 1]
            egh = eg[rs, HEADS + h:HEADS + h + 1]
            kb = kn * bh
            q_s[h, rs] = qn.astype(BF16)
            k_s[h, rs] = kn.astype(BF16)
            kb_s[h, rs] = kb.astype(BF16)
            vbk_s[h, rs, 0:GDN_DV] = (vh * bh).astype(BF16)
            vbk_s[h, rs, GDN_DV:] = (kb * egh).astype(BF16)
            wkqg_s[h, n, c:, :] = (qn * egh).astype(BF16)
            kg_s[h, rs] = (kn * kdec[rs, HEADS + h:HEADS + h + 1]).astype(BF16)
            gl_s[h, n] = jnp.broadcast_to(glast[n * c:n * c + SUBLANES, HEADS + h:HEADS + h + 1], (SUBLANES, LANES))
        for m in range(nc // 2):
            col = jnp.where(left, gc[2 * m * c:(2 * m + 1) * c, HEADS + h:HEADS + h + 1],
                            gc[(2 * m + 1) * c:(2 * m + 2) * c, HEADS + h:HEADS + h + 1])
            row = gct[HEADS + h:HEADS + h + 1, 2 * m * c:(2 * m + 2) * c]
            dec_s[h, m] = jnp.where(lower, jnp.exp(jnp.where(lower, col - row, 0.0)), 0.0)

    for slab in range(nslab):
        conv_ref[:, slab * DK:(slab + 1) * DK] = ext_s[slab, tm + CONV_HALO - (GDN_CONV - 1):tm + CONV_HALO, :]
    ext_s[:, 0:CONV_HALO, :] = ext_s[:, tm:tm + CONV_HALO, :]

    nb = HEAD_GROUP * nc // 2

    def diag_pair(g):
        return jnp.where(left, g[:, 0:c, :], g[:, c:, :])

    def precompute(gi, carry):
        hs = pl.ds(gi * HEAD_GROUP, HEAD_GROUP)
        dec = dec_s[hs].reshape(nb, c, 2 * c)
        kn2 = k_s[hs].reshape(nb, 2 * c, DK)
        a = diag_pair(_bdot('bid,bjd->bij', kb_s[hs].reshape(nb, 2 * c, DK), kn2))
        tinv = _unit_lower_inverse_pairs(jnp.where(ii > jj, a * dec, 0.0))
        uw = _bdot('bik,bkd->bid', _block_diag2(tinv), vbk_s[hs].reshape(nb, 2 * c, 2 * GDN_DV))
        u_s[hs] = uw[:, :, 0:GDN_DV].reshape(HEAD_GROUP, nc, c, GDN_DV)
        wkqg_s[hs, :, 0:c, :] = uw[:, :, GDN_DV:].astype(BF16).reshape(HEAD_GROUP, nc, c, DK)
        attn = diag_pair(_bdot('bid,bjd->bij', q_s[hs].reshape(nb, 2 * c, DK), kn2)) * dec
        at_s[hs] = attn.astype(BF16).reshape(HEAD_GROUP, nc // 2, c, 2 * c)
        return carry

    lax.fori_loop(0, HEADS // HEAD_GROUP, precompute, 0)

    def chunk_pair(m, carry):
        at2 = at_s[:, m]
        for half in range(2):
            n = 2 * m + half
            sl = pl.ds(pl.multiple_of(n * c, c), c)
            s = s_ref[...]
            r = _bdot('hid,hde->hie', wkqg_s[:, n], s.astype(BF16))
            vnb = (u_s[:, n] - r[:, 0:c, :]).astype(BF16)
            o_s[:, sl, :] = r[:, c:, :] + _bdot('hij,hje->hie', at2[:, :, half * c:(half + 1) * c], vnb)
            s_ref[...] = s * gl_s[:, n][:, 0:1, :] + _bdot('hcd,hce->hde', kg_s[:, sl, :], vnb)
        return carry

    lax.fori_loop(0, nc // 2, chunk_pair, 0)

    z = _dot_t(xb, wqkvz_ref[GDN_CONV_CH:, :])
    o_ref[...] = _gdn_out([o_s[h] for h in range(HEADS)], z, x, ng_ref, wo_ref[...], g_ref, b_ref)

    @pl.when(ti == pl.num_programs(1) - 1)
    def _():
        sout_ref[...] = s_ref[...]


def _gdn_prompt(x, w, g, b, tm):
    bsz, t, _ = x.shape
    assert t % tm == 0 and tm % (2 * GDN_CHUNK) == 0
    c = GDN_CHUNK
    nc = tm // c
    blk = pl.BlockSpec((None, tm, D_MODEL), lambda bi, ti: (bi, ti, 0))
    consts = (w["wqkvz"], w["wba"], w["conv_w"], w["alog"], w["dtb"], w["ng"], w["wo"], g, b)
    head_bf16 = pltpu.VMEM((HEADS, tm, DK), BF16)
    return pl.pallas_call(
        functools.partial(_gdn_prompt_kernel, tm=tm),
        grid=(bsz, t // tm),
        in_specs=[blk] + [_resident(a.shape) for a in consts],
        out_specs=[blk,
                   pl.BlockSpec((None, GDN_CONV - 1, GDN_CONV_CH), lambda bi, ti: (bi, 0, 0)),
                   pl.BlockSpec((None, HEADS, DK, GDN_DV), lambda bi, ti: (bi, 0, 0, 0))],
        out_shape=[_f32(*x.shape), _f32(bsz, GDN_CONV - 1, GDN_CONV_CH), _f32(bsz, HEADS, DK, GDN_DV)],
        scratch_shapes=[pltpu.VMEM((GDN_CONV_CH // LANES, tm + CONV_HALO, LANES), F32),
                        pltpu.VMEM((3, tm, LANES), F32),
                        pltpu.VMEM((HEADS, DK, GDN_DV), F32),
                        head_bf16, head_bf16, head_bf16,
                        pltpu.VMEM((HEADS, tm, 2 * GDN_DV), BF16),
                        head_bf16,
                        pltpu.VMEM((HEADS, nc, 2 * c, DK), BF16),
                        pltpu.VMEM((HEADS, nc, c, GDN_DV), F32),
                        pltpu.VMEM((HEADS, nc // 2, c, 2 * c), BF16),
                        pltpu.VMEM((HEADS, nc // 2, c, 2 * c), F32),
                        pltpu.VMEM((HEADS, nc, SUBLANES, LANES), F32),
                        pltpu.VMEM((HEADS, tm, GDN_DV), F32)],
        compiler_params=_params(2),
        name="gdn_prompt",
    )(x, *consts)


def _dot_t(a, bt):
    return lax.dot_general(a, bt, (((1,), (1,)), ((), ())), preferred_element_type=F32)


def _stream_projection(xb, win_ref, wb_ref, proj_s, transposed=False):
    wb = win_ref[...].astype(BF16)
    wb_ref[...] = wb
    proj_s[pl.program_id(0)] = _dot_t(xb, wb) if transposed else _dot(xb, wb)


def _gdn_sample_a_kernel(x_ref, cb_ref, win_ref, wba_ref, cw_ref, alog_ref, dtb_ref,
                         rows_ref, pre_ref, z_ref, wb_ref, proj_s):
    xb = x_ref[...].astype(BF16)
    _stream_projection(xb, win_ref, wb_ref, proj_s, transposed=True)

    @pl.when(pl.program_id(0) == pl.num_programs(0) - 1)
    def _():
        nq = GDN_CONV_CH // W_COLS
        pre = jnp.concatenate([proj_s[i] for i in range(nq)], axis=-1)
        pre_ref[...] = pre
        z_ref[...] = proj_s[nq]
        y = pre * cw_ref[GDN_CONV - 1:GDN_CONV, :]
        for j in range(GDN_CONV - 1):
            y = y + cb_ref[j] * cw_ref[j:j + 1, :]
        qkv = _silu(y)
        beta_all, gl = _gdn_gates(_dot(xb, wba_ref[...]), alog_ref, dtb_ref)
        eg = jnp.exp(gl)
        rows = x_ref.shape[0]
        rows_ref[...] = jnp.zeros(rows_ref.shape, F32)
        for h in range(HEADS):
            hs = slice(h * DK, (h + 1) * DK)
            qn = _l2norm(qkv[:, hs], DK ** -0.5)
            kn = _l2norm(qkv[:, HEADS * DK + h * DK:HEADS * DK + (h + 1) * DK])
            vh = qkv[:, 2 * HEADS * DK + h * GDN_DV:2 * HEADS * DK + (h + 1) * GDN_DV]
            bh = beta_all[:, h:h + 1]
            egh = eg[:, HEADS + h:HEADS + h + 1]
            rows_ref[:, h, 0, :] = kn
            rows_ref[:, h, 1, :] = qn
            rows_ref[:, h, 2, :] = vh * bh
            rows_ref[:, h, 3, :] = jnp.broadcast_to(bh * egh, (rows, DK))
            rows_ref[:, h, 4, :] = jnp.broadcast_to(egh, (rows, DK))
            rows_ref[:, h, 5, :] = jnp.broadcast_to(jnp.sum(qn * kn, -1, keepdims=True), (rows, DK))


def _gdn_sample_b_kernel(p_ref, s_ref, o_ref, sout_ref):
    bb = p_ref.shape[0]
    n = bb * HEADS
    p = p_ref[...].reshape(n, SUBLANES, DK)
    s = s_ref[...].reshape(n, DK, GDN_DV)
    pb = p.astype(BF16)
    r = _bdot('nqd,nde->nqe', pb, s.astype(BF16))
    v_new = p[:, 2:3, :] - p[:, 3:4, :] * r[:, 0:1, :]
    o = p[:, 4:5, :] * r[:, 1:2, :] + p[:, 5:6, :] * v_new
    row0 = lax.broadcasted_iota(jnp.int32, (n, SUBLANES, GDN_DV), 1) == 0
    vn8 = jnp.where(row0, jnp.broadcast_to(v_new, (n, SUBLANES, GDN_DV)), 0.0)
    s_new = s * p[:, 4:5, :] + _bdot('nqd,nqe->nde', pb, vn8.astype(BF16))
    o_ref[...] = jnp.broadcast_to(o, (n, SUBLANES, GDN_DV)).reshape(bb, HEADS, SUBLANES, GDN_DV)
    sout_ref[...] = s_new.reshape(bb, HEADS, DK, GDN_DV)


def _gdn_sample_c_kernel(o_ref, z_ref, x_ref, ng_ref, wo_ref, g_ref, b_ref, out_ref, wob_ref):
    wob = wo_ref[...].astype(BF16)
    wob_ref[...] = wob
    out_ref[...] = _gdn_out([o_ref[:, h, 0, :] for h in range(HEADS)],
                            z_ref[...], x_ref[...], ng_ref, wob, g_ref, b_ref)


def _projection_call(kernel, resident_args, w_in_all, layer, n_steps, n_cols, outs, rows, name, transposed=False):
    if transposed:
        w_spec = pl.BlockSpec((None, W_COLS, D_MODEL), lambda c: (layer, c, 0))
        wb_spec, wb_shape = pl.BlockSpec((W_COLS, D_MODEL), lambda c: (c, 0)), _bf16(n_cols, D_MODEL)
    else:
        w_spec = pl.BlockSpec((None, D_MODEL, W_COLS), lambda c: (layer, 0, c))
        wb_spec, wb_shape = pl.BlockSpec((D_MODEL, W_COLS), lambda c: (0, c)), _bf16(D_MODEL, n_cols)
    return pl.pallas_call(
        kernel,
        grid=(n_steps,),
        in_specs=[_resident(a.shape) for a in resident_args[:2]]
        + [w_spec]
        + [_resident(a.shape) for a in resident_args[2:]],
        out_specs=[_whole(o.shape) for o in outs] + [wb_spec],
        out_shape=list(outs) + [wb_shape],
        scratch_shapes=[pltpu.VMEM((n_steps, rows, W_COLS), F32)],
        compiler_params=_params(1),
        name=name,
    )(*resident_args[:2], w_in_all, *resident_args[2:])


def _state_step(kernel, p, extra, s0, bb, name):
    bsz = s0.shape[0]
    assert bsz % bb == 0

    def blk(a):
        return pl.BlockSpec((bb,) + a.shape[1:], lambda i: (i,) + (0,) * (a.ndim - 1))

    ins = (p,) + tuple(extra) + (s0,)
    o_shape = (bsz, HEADS, SUBLANES, s0.shape[-1])
    return pl.pallas_call(
        kernel,
        grid=(bsz // bb,),
        in_specs=[blk(a) for a in ins],
        out_specs=[pl.BlockSpec((bb,) + o_shape[1:], lambda i: (i, 0, 0, 0)), blk(s0)],
        out_shape=[_f32(*o_shape), _f32(*s0.shape)],
        compiler_params=_params(1),
        name=name,
    )(*ins)


def _gdn_sample(x2d, conv_buf, s0, w_in_all, w_out, layer, w, g, b):
    bsz = x2d.shape[0]
    cb_t = conv_buf.transpose(1, 0, 2)
    rows, pre, z, wqkvz = _projection_call(
        _gdn_sample_a_kernel, (x2d, cb_t, w["wba"], w["conv_w"], w["alog"], w["dtb"]), w_in_all, layer,
        GDN_QKVZ // W_COLS, GDN_QKVZ,
        [_f32(bsz, HEADS, SUBLANES, DK), _f32(bsz, GDN_CONV_CH), _f32(bsz, D_MODEL)], bsz, "gdn_sample_a",
        transposed=True)
    o8, s_new = _state_step(_gdn_sample_b_kernel, rows, (), s0, 8, "gdn_sample_b")
    x1, wob = _whole_call(_gdn_sample_c_kernel, (o8, z, x2d, w["ng"], w_out, g, b),
                          [_f32(bsz, D_MODEL), _bf16(*w_out.shape)], "gdn_sample_c")
    new_conv = jnp.concatenate([conv_buf[:, 1:], pre[:, None, :]], axis=1)
    return x1, new_conv, s_new, wqkvz, wob


def _log_gamma(h):
    return math.log(1.0 - 2.0 ** (-5.0 - h))


def _rope_tables(pos, freq_ref):
    ang = pos * freq_ref[...]
    lane = lax.broadcasted_iota(jnp.int32, ang.shape, 1)
    return jnp.cos(ang), jnp.where(lane < DK // 2, -1.0, 1.0) * jnp.sin(ang)


def _rope(v, cos2, sin2):
    return v * cos2 + pltpu.roll(v, DK // 2, 1) * sin2


def _ret_out(o_heads, gate, x, wo, g_ref, b_ref):
    gated = []
    for h, oh in enumerate(o_heads):
        mu = jnp.mean(oh, -1, keepdims=True)
        oc = oh - mu
        on = oc * lax.rsqrt(jnp.mean(oc * oc, -1, keepdims=True) + LN_EPS)
        gated.append((_silu(gate[:, h * RET_DV:(h + 1) * RET_DV]) * on).astype(BF16))
    mix = _dot(jnp.concatenate(gated, axis=-1), wo)
    return _layer_norm(DN_ALPHA * x + mix, g_ref[...], b_ref[...])


def _ret_prompt_kernel(x_ref, win_ref, freq_ref, wo_ref, g_ref, b_ref, o_ref, sout_ref,
                       s_ref, q_s, k_s, qx_s, kz_s, v_s, dec_s, o_s, *, tm, c, pos0):
    nc = tm // c
    ti = pl.program_id(1)
    hk = HEADS * DK

    @pl.when(ti == 0)
    def _():
        s_ref[...] = jnp.zeros(s_ref.shape, F32)
        diff = (lax.broadcasted_iota(jnp.int32, (c, c), 0) - lax.broadcasted_iota(jnp.int32, (c, c), 1)).astype(F32)
        for h in range(HEADS):
            dec_s[h] = jnp.where(diff >= 0, jnp.exp(jnp.maximum(diff, 0.0) * _log_gamma(h)), 0.0)

    x = x_ref[...]
    xb = x.astype(BF16)
    row = lax.broadcasted_iota(jnp.int32, (tm, 1), 0)
    cos2, sin2 = _rope_tables((pos0 + ti * tm + row).astype(F32), freq_ref)
    idx = (row % c).astype(F32)
    qall = _dot(xb, win_ref[:, 0:hk])
    kall = _dot(xb, win_ref[:, hk:2 * hk])
    vall = _dot(xb, win_ref[:, 2 * hk:2 * hk + HEADS * RET_DV])
    for h in range(HEADS):
        hs = slice(h * DK, (h + 1) * DK)
        qr = _rope(qall[:, hs], cos2, sin2)
        kr = _rope(kall[:, hs], cos2, sin2) * DK ** -0.5
        q_s[h] = qr.astype(BF16)
        k_s[h] = kr.astype(BF16)
        qx_s[h] = (qr * jnp.exp((idx + 1.0) * _log_gamma(h))).astype(BF16)
        kz_s[h] = (kr * jnp.exp((c - 1.0 - idx) * _log_gamma(h))).astype(BF16)
        v_s[h] = vall[:, h * RET_DV:(h + 1) * RET_DV].astype(BF16)

    for n in range(nc):
        sl = slice(n * c, (n + 1) * c)
        v_c = v_s[:, sl, :]
        attn = _bdot('hid,hjd->hij', q_s[:, sl, :], k_s[:, sl, :]) * dec_s[...]
        s = s_ref[...]
        o_s[:, sl, :] = (_bdot('hij,hje->hie', attn.astype(BF16), v_c)
                         + _bdot('hcd,hde->hce', qx_s[:, sl, :], s.astype(BF16)))
        upd = _bdot('hcd,hce->hde', kz_s[:, sl, :], v_c)
        for h in range(HEADS):
            s_ref[h] = math.exp(c * _log_gamma(h)) * s[h] + upd[h]

    gate = _dot(xb, win_ref[:, 2 * hk + HEADS * RET_DV:])
    o_ref[...] = _ret_out([o_s[h] for h in range(HEADS)], gate, x, wo_ref[...], g_ref, b_ref)

    @pl.when(ti == pl.num_programs(1) - 1)
    def _():
        sout_ref[...] = s_ref[...]


def _ret_prompt(x, w, g, b, tm):
    bsz, t, _ = x.shape
    c = min(RET_CHUNK, tm)
    assert t % tm == 0 and tm % c == 0
    blk = pl.BlockSpec((None, tm, D_MODEL), lambda bi, ti: (bi, ti, 0))
    consts = (w["win"], w["freq"], w["wo"], g, b)
    head_k = pltpu.VMEM((HEADS, tm, DK), BF16)
    return pl.pallas_call(
        functools.partial(_ret_prompt_kernel, tm=tm, c=c, pos0=0),
        grid=(bsz, t // tm),
        in_specs=[blk] + [_resident(a.shape) for a in consts],
        out_specs=[blk, pl.BlockSpec((None, HEADS, DK, RET_DV), lambda bi, ti: (bi, 0, 0, 0))],
        out_shape=[_f32(*x.shape), _f32(bsz, HEADS, DK, RET_DV)],
        scratch_shapes=[pltpu.VMEM((HEADS, DK, RET_DV), F32), head_k, head_k, head_k, head_k,
                        pltpu.VMEM((HEADS, tm, RET_DV), BF16),
                        pltpu.VMEM((HEADS, c, c), F32),
                        pltpu.VMEM((HEADS, tm, RET_DV), F32)],
        compiler_params=_params(2),
        name="ret_prompt",
    )(x, *consts)


def _ret_sample_a_kernel(x_ref, freq_ref, win_ref, kq_ref, vrows_ref, gate_ref, wb_ref, proj_s, *, pos0):
    xb = x_ref[...].astype(BF16)
    _stream_projection(xb, win_ref, wb_ref, proj_s)

    @pl.when(pl.program_id(0) == pl.num_programs(0) - 1)
    def _():
        rows = x_ref.shape[0]
        cos2, sin2 = _rope_tables(jnp.full((1, 1), pos0, F32), freq_ref)
        qall, kall = proj_s[0], proj_s[1]
        gate_ref[...] = jnp.concatenate([proj_s[4], proj_s[5]], axis=-1)
        kq_ref[...] = jnp.zeros(kq_ref.shape, F32)
        vrows_ref[...] = jnp.zeros(vrows_ref.shape, F32)
        for h in range(HEADS):
            hs = slice(h * DK, (h + 1) * DK)
            qr = _rope(qall[:, hs], cos2, sin2)
            kr = _rope(kall[:, hs], cos2, sin2) * DK ** -0.5
            kq_ref[:, h, 0, :] = kr
            kq_ref[:, h, 1, :] = qr
            vh = proj_s[2 + h // 4][:, (h % 4) * RET_DV:(h % 4 + 1) * RET_DV]
            vrows_ref[:, h, 0, :] = vh
            vrows_ref[:, h, 1, :] = jnp.broadcast_to(jnp.sum(qr * kr, -1, keepdims=True), (rows, RET_DV))
            vrows_ref[:, h, 2, :] = jnp.full((rows, RET_DV), math.exp(_log_gamma(h)), F32)


def _ret_sample_b_kernel(kq_ref, v_ref, s_ref, o_ref, sout_ref):
    bb = kq_ref.shape[0]
    n = bb * HEADS
    kqb = kq_ref[...].reshape(n, SUBLANES, DK).astype(BF16)
    v8 = v_ref[...].reshape(n, SUBLANES, RET_DV)
    s = s_ref[...].reshape(n, DK, RET_DV)
    r = _bdot('nqd,nde->nqe', kqb, s.astype(BF16))
    gamma = v8[:, 2:3, :]
    o = v8[:, 1:2, :] * v8[:, 0:1, :] + gamma * r[:, 1:2, :]
    row0 = lax.broadcasted_iota(jnp.int32, (n, SUBLANES, RET_DV), 1) == 0
    s_new = s * gamma + _bdot('nqd,nqe->nde', kqb, jnp.where(row0, v8, 0.0).astype(BF16))
    o_ref[...] = jnp.broadcast_to(o, (n, SUBLANES, RET_DV)).reshape(bb, HEADS, SUBLANES, RET_DV)
    sout_ref[...] = s_new.reshape(bb, HEADS, DK, RET_DV)


def _ret_sample_c_kernel(o_ref, gate_ref, x_ref, wo_ref, g_ref, b_ref, out_ref, wob_ref):
    wob = wo_ref[...].astype(BF16)
    wob_ref[...] = wob
    out_ref[...] = _ret_out([o_ref[:, h, 0, :] for h in range(HEADS)],
                            gate_ref[...], x_ref[...], wob, g_ref, b_ref)


def _ret_sample(x2d, s0, w_in_all, w_out, layer, freq, g, b):
    bsz = x2d.shape[0]
    kq, vrows, gate, winb = _projection_call(
        functools.partial(_ret_sample_a_kernel, pos0=float(PAST_LEN)), (x2d, freq), w_in_all, layer,
        RET_IN // W_COLS, RET_IN,
        [_f32(bsz, HEADS, SUBLANES, DK), _f32(bsz, HEADS, SUBLANES, RET_DV), _f32(bsz, HEADS * RET_DV)],
        bsz, "ret_sample_a")
    o8, s_new = _state_step(_ret_sample_b_kernel, kq, (vrows,), s0, 4, "ret_sample_b")
    x1, wob = _whole_call(_ret_sample_c_kernel, (o8, gate, x2d, w_out, g, b),
                          [_f32(bsz, D_MODEL), _bf16(*w_out.shape)], "ret_sample_c")
    return x1, s_new, winb, wob


def _lane_row(values, offset):
    return jnp.zeros((1, LANES), F32).at[0, offset:offset + HEADS].set(values.astype(F32))


def _gdn_small_weights(w_in_t, conv_w, a_log, dt_bias, norm_g):
    wba = jnp.zeros((D_MODEL, LANES), F32).at[:, :2 * HEADS].set(w_in_t[GDN_QKVZ:, :].T)
    return {"wba": wba.astype(BF16), "conv_w": conv_w,
            "alog": _lane_row(a_log, HEADS), "dtb": _lane_row(dt_bias, HEADS), "ng": norm_g.reshape(1, GDN_DV)}


def _rope_freqs():
    half = DK // 2
    freqs = ROPE_BASE ** (-jnp.arange(half, dtype=F32) / half)
    return jnp.concatenate([freqs, freqs]).reshape(1, DK)


def kernel(x_prompt, x_sample, state_pool, state_gdn_conv, state_gdn, state_ret, pool_w, pool_scale, gdn_w_in,
           gdn_conv_w, gdn_a_log, gdn_dt_bias, gdn_norm_g, gdn_w_out, ret_w_in, ret_w_out, ffn_w13, ffn_w2,
           ln_g, ln_b):
    bsz, seq, _ = x_prompt.shape
    dec = x_sample.shape[0]
    tm = min(512, seq)
    xp = x_prompt
    xs = x_sample.reshape(dec, D_MODEL)
    pool_p, pool_s, conv_p, conv_s, gdn_p, gdn_s, ret_p, ret_s = [], [], [], [], [], [], [], []
    for i in range(DEPTH):
        kind, j = i % N_MIXERS, i // N_MIXERS
        g1, b1 = ln_g[i, 0].reshape(1, D_MODEL), ln_b[i, 0].reshape(1, D_MODEL)
        g2, b2 = ln_g[i, 1].reshape(1, D_MODEL), ln_b[i, 1].reshape(1, D_MODEL)
        if kind == 0:
            sc = pool_scale[j].reshape(1, D_MODEL)
            pool_p.append(xp[:, seq - POOL_BUF:])
            xs, pwb, nbuf = _pool_sample(xs, state_pool[j].transpose(1, 0, 2), pool_w[j], sc, g1, b1)
            pool_s.append(nbuf)
            xs, *fw = _ffn_cast(xs, ffn_w13, ffn_w2, i, g2, b2)
            xp = _pool_ffn_prompt(xp, pwb, sc, g1, b1, fw, g2, b2, min(2 * tm, seq), tm)
            continue
        if kind == 1:
            w_in_t = gdn_w_in.transpose(0, 2, 1)
            w = _gdn_small_weights(w_in_t[j], gdn_conv_w[j], gdn_a_log[j], gdn_dt_bias[j], gdn_norm_g[j])
            xs, cs, ss, w["wqkvz"], w["wo"] = _gdn_sample(xs, state_gdn_conv[j], state_gdn[j], w_in_t,
                                                         gdn_w_out[j], j, w, g1, b1)
            xp, cp, sp = _gdn_prompt(xp, w, g1, b1, tm)
            conv_p.append(cp), conv_s.append(cs), gdn_p.append(sp), gdn_s.append(ss)
        else:
            w = {"freq": _rope_freqs()}
            xs, ss, w["win"], w["wo"] = _ret_sample(xs, state_ret[j], ret_w_in, ret_w_out[j], j, w["freq"], g1, b1)
            xp, sp = _ret_prompt(xp, w, g1, b1, tm)
            ret_p.append(sp), ret_s.append(ss)
        xs, *fw = _ffn_cast(xs, ffn_w13, ffn_w2, i, g2, b2)
        xp = _ffn(xp.reshape(bsz * seq, D_MODEL), fw, g2, b2, min(2 * tm, seq), tm).reshape(bsz, seq, D_MODEL)
    return (xp, xs.reshape(dec, 1, D_MODEL), jnp.stack(pool_p), jnp.stack(pool_s).transpose(0, 2, 1, 3), jnp.stack(conv_p),
            jnp.stack(conv_s), jnp.stack(gdn_p), jnp.stack(gdn_s), jnp.stack(ret_p), jnp.stack(ret_s))
```

```python
import functools
import math

import jax
import jax.numpy as jnp
from jax import lax
from jax.experimental import pallas as pl
from jax.experimental.pallas import tpu as pltpu

F32 = jnp.float32
BF16 = jnp.bfloat16

D_MODEL = 1024
DEPTH = 4
PAST_LEN = 16384
N_MIXERS = 3

POOL_WINDOWS = (2, 4, 8, 16)
POOL_GROUP = D_MODEL // len(POOL_WINDOWS)
POOL_BUF = max(POOL_WINDOWS) - 1
POOL_HALO = 16
POOL_ROWS = 64

HEADS = 8
DK = D_MODEL // HEADS
GDN_DV = DK
GDN_CONV = 4
GDN_CONV_CH = HEADS * (2 * DK + GDN_DV)
GDN_QKVZ = GDN_CONV_CH + HEADS * GDN_DV
CONV_HALO = 8
CONV_PHASES = 4
GDN_CHUNK = 64
HEAD_GROUP = 8
RET_DV = 2 * DK
RET_IN = 2 * HEADS * DK + 2 * HEADS * RET_DV
RET_CHUNK = 256
ROPE_BASE = 10000.0
W_COLS = 1024

D_FF = -(-8 * D_MODEL // (3 * 256)) * 256
FF_CHUNK = 256
CAST_CHUNK = 256
DN_ALPHA = (2 * DEPTH) ** 0.25
LN_EPS = 1e-5
RMS_EPS = 1e-6

SUBLANES = 8
LANES = 128
VMEM_LIMIT = 56 * 1024 * 1024


def _params(n_grid):
    return pltpu.CompilerParams(dimension_semantics=("arbitrary",) * n_grid, vmem_limit_bytes=VMEM_LIMIT)


def _resident(shape):
    nd = len(shape)
    return pl.BlockSpec(shape, lambda *_: (0,) * nd, pipeline_mode=pl.Buffered(1))


def _whole(shape):
    nd = len(shape)
    return pl.BlockSpec(shape, lambda *_: (0,) * nd)


def _dot(a, b):
    return jnp.dot(a, b, preferred_element_type=F32)


def _bdot(spec, a, b):
    return jnp.einsum(spec, a, b, preferred_element_type=F32)


def _silu(x):
    return x * jax.nn.sigmoid(x)


def _layer_norm(y, g, b):
    mu = jnp.mean(y, -1, keepdims=True)
    yc = y - mu
    var = jnp.mean(yc * yc, -1, keepdims=True)
    return yc * lax.rsqrt(var + LN_EPS) * g + b


def _f32(*shape):
    return jax.ShapeDtypeStruct(shape, F32)


def _bf16(*shape):
    return jax.ShapeDtypeStruct(shape, BF16)


def _ffn_tail(x, w1_ref, w3_ref, w2_ref, g_ref, b_ref):
    xb = x.astype(BF16)
    acc = None
    for c in range(D_FF // FF_CHUNK):
        lo = c * FF_CHUNK
        a = _dot(xb, w1_ref[:, lo:lo + FF_CHUNK])
        b = _dot(xb, w3_ref[:, lo:lo + FF_CHUNK])
        h = (_silu(a) * b).astype(BF16)
        p = _dot(h, w2_ref[lo:lo + FF_CHUNK, :])
        acc = p if acc is None else acc + p
    return _layer_norm(DN_ALPHA * x + acc, g_ref[...], b_ref[...])


def _ffn_kernel(x_ref, w1_ref, w3_ref, w2_ref, g_ref, b_ref, o_ref, *, sub):
    for r0 in range(0, x_ref.shape[0], sub):
        o_ref[r0:r0 + sub, :] = _ffn_tail(x_ref[r0:r0 + sub, :], w1_ref, w3_ref, w2_ref, g_ref, b_ref)


def _ffn(x2d, fw, g, b, tm, sub):
    n = x2d.shape[0]
    assert n % tm == 0 and tm % sub == 0
    row = pl.BlockSpec((tm, D_MODEL), lambda i: (i, 0))
    consts = tuple(fw) + (g, b)
    return pl.pallas_call(
        functools.partial(_ffn_kernel, sub=sub),
        grid=(n // tm,),
        in_specs=[row] + [_resident(a.shape) for a in consts],
        out_specs=row,
        out_shape=_f32(n, D_MODEL),
        compiler_params=_params(1),
        name="ffn",
    )(x2d, *consts)


def _ffn_cast_kernel(x_ref, w1_ref, w3_ref, w2_ref, g_ref, b_ref, o_ref, w1b_ref, w3b_ref, w2b_ref, acc_ref):
    c = pl.program_id(0)

    @pl.when(c == 0)
    def _():
        acc_ref[...] = jnp.zeros(acc_ref.shape, F32)

    xb = x_ref[...].astype(BF16)
    w1b, w3b, w2b = w1_ref[...].astype(BF16), w3_ref[...].astype(BF16), w2_ref[...].astype(BF16)
    w1b_ref[...], w3b_ref[...], w2b_ref[...] = w1b, w3b, w2b
    h = (_silu(_dot(xb, w1b)) * _dot(xb, w3b)).astype(BF16)
    acc_ref[...] += _dot(h, w2b)

    @pl.when(c == pl.num_programs(0) - 1)
    def _():
        o_ref[...] = _layer_norm(DN_ALPHA * x_ref[...] + acc_ref[...], g_ref[...], b_ref[...])


def _ffn_cast(x2d, w13_all, w2_all, layer, g, b):
    n = x2d.shape[0]
    cc = CAST_CHUNK
    nch = D_FF // cc
    return pl.pallas_call(
        _ffn_cast_kernel,
        grid=(nch,),
        in_specs=[_resident(x2d.shape),
                  pl.BlockSpec((None, D_MODEL, cc), lambda c: (layer, 0, c)),
                  pl.BlockSpec((None, D_MODEL, cc), lambda c: (layer, 0, nch + c)),
                  pl.BlockSpec((None, cc, D_MODEL), lambda c: (layer, c, 0)),
                  _resident(g.shape), _resident(b.shape)],
        out_specs=[_whole(x2d.shape),
                   pl.BlockSpec((D_MODEL, cc), lambda c: (0, c)),
                   pl.BlockSpec((D_MODEL, cc), lambda c: (0, c)),
                   pl.BlockSpec((cc, D_MODEL), lambda c: (c, 0))],
        out_shape=[_f32(n, D_MODEL), _bf16(D_MODEL, D_FF), _bf16(D_MODEL, D_FF), _bf16(D_FF, D_MODEL)],
        scratch_shapes=[pltpu.VMEM((n, D_MODEL), F32)],
        compiler_params=_params(1),
        name="ffn_cast",
    )(x2d, w13_all, w13_all, w2_all, g, b)


def _rows_up(v, s):
    rows, w = v.shape
    if s % SUBLANES:
        sub = lax.broadcasted_iota(jnp.int32, v.shape, 0) % SUBLANES
        v = pltpu.roll(v.reshape(rows // SUBLANES, SUBLANES, w), s % SUBLANES, 1).reshape(rows, w)
        v = jnp.where(sub >= s % SUBLANES, v, jnp.concatenate([v[0:SUBLANES], v[0:rows - SUBLANES]], axis=0))
    whole = s - s % SUBLANES
    return jnp.concatenate([v[0:whole], v[0:rows - whole]], axis=0) if whole else v


def _trailing_sums(e, win):
    s, w = e, 1
    while w < win:
        s = s + _rows_up(s, w)
        w *= 2
    return s


def _pool_ffn_prompt_kernel(x_ref, pw_ref, sc_ref, g1_ref, b1_ref, w1_ref, w3_ref, w2_ref, g2_ref, b2_ref,
                            o_ref, ext_ref, pooled_ref, *, tm, sub, n_valid):
    ti = pl.program_id(1)
    rb = POOL_ROWS

    @pl.when(ti == 0)
    def _():
        ext_ref[0:POOL_HALO, :] = jnp.zeros((POOL_HALO, D_MODEL), F32)

    ext_ref[POOL_HALO:POOL_HALO + tm, :] = x_ref[...]

    def pool_rows(r0):
        t = ti * tm + r0 + lax.broadcasted_iota(jnp.int32, (rb, 1), 0)
        for gi, win in enumerate(POOL_WINDOWS):
            lo, hi = gi * POOL_GROUP, (gi + 1) * POOL_GROUP
            e = ext_ref[r0:r0 + rb + POOL_HALO, lo:hi]
            cnt = jnp.minimum(t + 1 + n_valid, win).astype(F32)
            pooled = _trailing_sums(e, win)[POOL_HALO:] / cnt - e[POOL_HALO:]
            pooled_ref[r0:r0 + rb, lo:hi] = pooled.astype(BF16)

    for r0 in range(0, tm, rb):
        pool_rows(r0)
    ext_ref[0:POOL_HALO, :] = ext_ref[tm:tm + POOL_HALO, :]
    for s0 in range(0, tm, sub):
        mix = jnp.concatenate(
            [_dot(pooled_ref[s0:s0 + sub, gi * POOL_GROUP:(gi + 1) * POOL_GROUP], pw_ref[gi])
             for gi in range(len(POOL_WINDOWS))], axis=-1) * sc_ref[...]
        x1 = _layer_norm(DN_ALPHA * x_ref[s0:s0 + sub, :] + mix, g1_ref[...], b1_ref[...])
        o_ref[s0:s0 + sub, :] = _ffn_tail(x1, w1_ref, w3_ref, w2_ref, g2_ref, b2_ref)


def _pool_ffn_prompt(x, pwb, sc, g1, b1, fw, g2, b2, tm, sub):
    bsz, t, _ = x.shape
    assert t % tm == 0 and tm % sub == 0 and sub % POOL_ROWS == 0
    blk = pl.BlockSpec((None, tm, D_MODEL), lambda bi, ti: (bi, ti, 0))
    consts = (pwb, sc, g1, b1) + tuple(fw) + (g2, b2)
    return pl.pallas_call(
        functools.partial(_pool_ffn_prompt_kernel, tm=tm, sub=sub, n_valid=0),
        grid=(bsz, t // tm),
        in_specs=[blk] + [_resident(a.shape) for a in consts],
        out_specs=blk,
        out_shape=_f32(*x.shape),
        scratch_shapes=[pltpu.VMEM((tm + POOL_HALO, D_MODEL), F32), pltpu.VMEM((tm, D_MODEL), BF16)],
        compiler_params=_params(2),
        name="pool_ffn_prompt",
    )(x, *consts)


def _pool_sample_kernel(x_ref, buf_ref, pw_ref, sc_ref, g_ref, b_ref, o_ref, pwb_ref, nbuf_ref, *, n_valid):
    x = x_ref[...]
    pwb = pw_ref[...].astype(BF16)
    pwb_ref[...] = pwb
    outs = []
    for gi, win in enumerate(POOL_WINDOWS):
        lo, hi = gi * POOL_GROUP, (gi + 1) * POOL_GROUP
        xs = x[:, lo:hi]
        s = xs
        for d in range(1, win):
            s = s + buf_ref[POOL_BUF - d, :, lo:hi]
        pooled = s / float(min(1 + n_valid, win)) - xs
        outs.append(_dot(pooled.astype(BF16), pwb[gi]))
    mix = jnp.concatenate(outs, axis=-1) * sc_ref[...]
    o_ref[...] = _layer_norm(DN_ALPHA * x + mix, g_ref[...], b_ref[...])
    nbuf_ref[0:POOL_BUF - 1] = buf_ref[1:POOL_BUF]
    nbuf_ref[POOL_BUF - 1] = x


def _layer_of(stacked, layer):
    nd = stacked.ndim - 1
    return pl.BlockSpec((None,) + stacked.shape[1:], lambda *_: (layer,) + (0,) * nd, pipeline_mode=pl.Buffered(1))


def _whole_call(kernel, args, outs, name, in_specs=None):
    return pl.pallas_call(
        kernel,
        grid=(1,),
        in_specs=in_specs or [_resident(a.shape) for a in args],
        out_specs=[_whole(o.shape) for o in outs],
        out_shape=list(outs),
        compiler_params=_params(1),
        name=name,
    )(*args)


def _pool_sample(x2d, buf, pw_all, layer, sc, g, b):
    args = (x2d, buf, pw_all, sc, g, b)
    specs = [_layer_of(a, layer) if a is pw_all else _resident(a.shape) for a in args]
    return _whole_call(functools.partial(_pool_sample_kernel, n_valid=min(PAST_LEN, POOL_BUF)), args,
                       [_f32(*x2d.shape), _bf16(*pw_all.shape[1:]), _f32(*buf.shape)], "pool_sample", specs)


def _softplus(x):
    return jnp.maximum(x, 0.0) + jnp.log1p(jnp.exp(-jnp.abs(x)))


def _gdn_gates(ba, alog_ref, dtb_ref):
    return jax.nn.sigmoid(ba), -jnp.exp(alog_ref[...]) * _softplus(ba + dtb_ref[...])


def _l2norm(v, scale=1.0):
    return v * (lax.rsqrt(jnp.sum(v * v, -1, keepdims=True) + RMS_EPS) * scale)


def _gdn_out(o_heads, z, x, ng_ref, wo, g_ref, b_ref):
    gated = []
    for h, oh in enumerate(o_heads):
        on = oh * lax.rsqrt(jnp.mean(oh * oh, -1, keepdims=True) + RMS_EPS) * ng_ref[...]
        gated.append((on * _silu(z[:, h * GDN_DV:(h + 1) * GDN_DV])).astype(BF16))
    mix = _dot(jnp.concatenate(gated, axis=-1), wo)
    return _layer_norm(DN_ALPHA * x + mix, g_ref[...], b_ref[...])


def _pair_index(c):
    ii = lax.broadcasted_iota(jnp.int32, (c, 2 * c), 0)
    lane = lax.broadcasted_iota(jnp.int32, (c, 2 * c), 1)
    return ii, lane % c, lane < c


def _block_diag2(p):
    _, _, left = _pair_index(p.shape[1])
    return jnp.concatenate([jnp.where(left, p, 0.0).astype(BF16), jnp.where(left, 0.0, p).astype(BF16)], axis=1)


def _unit_lower_inverse_pairs(a):
    c = a.shape[1]
    ii, jj, _ = _pair_index(c)
    x = jnp.where(ii == jj, 1.0, 0.0) - jnp.where((ii // 2 == jj // 2) & (ii > jj), a, 0.0)
    s = 4
    while s <= c:
        off = jnp.where((ii // s == jj // s) & (ii // (s // 2) != jj // (s // 2)) & (ii > jj), a, 0.0)
        y = _bdot('bik,bkj->bij', off.astype(BF16), _block_diag2(x))
        x = x - _bdot('bik,bkj->bij', x.astype(BF16), _block_diag2(y))
        s *= 2
    return x


def _gdn_prompt_kernel(x_ref, wqkvz_ref, wba_ref, cw_ref, alog_ref, dtb_ref, ng_ref, wo_ref, g_ref, b_ref,
                       o_ref, conv_ref, sout_ref,
                       ext_s, act_s, s_ref, q_s, k_s, kb_s, vbk_s, kg_s, wkqg_s, u_s, at_s, dec_s, gl_s,
                       o_s, *, tm):
    c = GDN_CHUNK
    nc = tm // c
    ti = pl.program_id(1)
    nslab = GDN_CONV_CH // LANES

    @pl.when(ti == 0)
    def _():
        ext_s[:, 0:CONV_HALO, :] = jnp.zeros((nslab, CONV_HALO, LANES), F32)
        s_ref[...] = jnp.zeros(s_ref.shape, F32)

    x = x_ref[...]
    xb = x.astype(BF16)

    def project(p):
        for part in range(3):
            slab = part * HEADS + 2 * p
            pre = _dot(xb, wqkvz_ref[:, slab * DK:(slab + 2) * DK])
            ext_s[slab, CONV_HALO:CONV_HALO + tm, :] = pre[:, 0:DK]
            ext_s[slab + 1, CONV_HALO:CONV_HALO + tm, :] = pre[:, DK:]

    def conv_silu(part, h):
        slab = part * HEADS + h
        rows = tm // CONV_PHASES
        for ph in range(CONV_PHASES):
            y = None
            for j in range(GDN_CONV):
                start = CONV_HALO + ph - (GDN_CONV - 1 - j)
                term = ext_s[slab, pl.ds(start, rows, stride=CONV_PHASES), :] * cw_ref[j:j + 1, slab * DK:(slab + 1) * DK]
                y = term if y is None else y + term
            act_s[h % 2, part, pl.ds(ph, rows, stride=CONV_PHASES), :] = _silu(y)

    beta_all, gl = _gdn_gates(_dot(xb, wba_ref[...]), alog_ref, dtb_ref)
    pos = lax.broadcasted_iota(jnp.int32, (tm, LANES), 0) % c
    gc = gl
    step = 1
    while step < c:
        gc = gc + jnp.where(pos >= step, pltpu.roll(gc, step, 0), 0.0)
        step *= 2
    gcl = jnp.broadcast_to(gc.reshape(nc, c, LANES)[:, c - 1:c, :], (nc, c, LANES)).reshape(tm, LANES)
    eg = jnp.exp(gc)
    kdec = jnp.exp(gcl - gc)
    glast = jnp.exp(gcl)
    gct = gc.T

    ii, jj, left = _pair_index(c)
    lower = ii >= jj
    for h in range(HEADS):
        if h % 2 == 0:
            project(h // 2)
        for part in range(3):
            conv_silu(part, h)
        for n in range(nc):
            rs = slice(n * c, (n + 1) * c)
            qn = _l2norm(act_s[h % 2, 0, rs, :], DK ** -0.5)
            kn = _l2norm(act_s[h % 2, 1, rs, :])
            vh = act_s[h % 2, 2, rs, :]
            bh = beta_all[rs, h:h + 1]
            egh = eg[rs, HEADS + h:HEADS + h + 1]
            kb = kn * bh
            q_s[h, rs] = qn.astype(BF16)
            k_s[h, rs] = kn.astype(BF16)
            kb_s[h, rs] = kb.astype(BF16)
            vbk_s[h, rs, 0:GDN_DV] = (vh * bh).astype(BF16)
            vbk_s[h, rs, GDN_DV:] = (kb * egh).astype(BF16)
            wkqg_s[h, n, c:, :] = (qn * egh).astype(BF16)
            kg_s[h, rs] = (kn * kdec[rs, HEADS + h:HEADS + h + 1]).astype(BF16)
            gl_s[h, n] = jnp.broadcast_to(glast[n * c:n * c + SUBLANES, HEADS + h:HEADS + h + 1], (SUBLANES, LANES))
        for m in range(nc // 2):
            col = jnp.where(left, gc[2 * m * c:(2 * m + 1) * c, HEADS + h:HEADS + h + 1],
                            gc[(2 * m + 1) * c:(2 * m + 2) * c, HEADS + h:HEADS + h + 1])
            row = gct[HEADS + h:HEADS + h + 1, 2 * m * c:(2 * m + 2) * c]
            dec_s[h, m] = jnp.where(lower, jnp.exp(jnp.where(lower, col - row, 0.0)), 0.0)

    for slab in range(nslab):
        conv_ref[:, slab * DK:(slab + 1) * DK] = ext_s[slab, tm + CONV_HALO - (GDN_CONV - 1):tm + CONV_HALO, :]
    ext_s[:, 0:CONV_HALO, :] = ext_s[:, tm:tm + CONV_HALO, :]

    nb = HEAD_GROUP * nc // 2

    def diag_pair(g):
        return jnp.where(left, g[:, 0:c, :], g[:, c:, :])

    def precompute(gi, carry):
        hs = pl.ds(gi * HEAD_GROUP, HEAD_GROUP)
        dec = dec_s[hs].reshape(nb, c, 2 * c)
        kn2 = k_s[hs].reshape(nb, 2 * c, DK)
        a = diag_pair(_bdot('bid,bjd->bij', kb_s[hs].reshape(nb, 2 * c, DK), kn2))
        tinv = _unit_lower_inverse_pairs(jnp.where(ii > jj, a * dec, 0.0))
        uw = _bdot('bik,bkd->bid', _block_diag2(tinv), vbk_s[hs].reshape(nb, 2 * c, 2 * GDN_DV))
        u_s[hs] = uw[:, :, 0:GDN_DV].reshape(HEAD_GROUP, nc, c, GDN_DV)
        wkqg_s[hs, :, 0:c, :] = uw[:, :, GDN_DV:].astype(BF16).reshape(HEAD_GROUP, nc, c, DK)
        attn = diag_pair(_bdot('bid,bjd->bij', q_s[hs].reshape(nb, 2 * c, DK), kn2)) * dec
        at_s[hs] = attn.astype(BF16).reshape(HEAD_GROUP, nc // 2, c, 2 * c)
        return carry

    lax.fori_loop(0, HEADS // HEAD_GROUP, precompute, 0)

    def chunk_pair(m, carry):
        at2 = at_s[:, m]
        for half in range(2):
            n = 2 * m + half
            sl = pl.ds(pl.multiple_of(n * c, c), c)
            s = s_ref[...]
            r = _bdot('hid,hde->hie', wkqg_s[:, n], s.astype(BF16))
            vnb = (u_s[:, n] - r[:, 0:c, :]).astype(BF16)
            o_s[:, sl, :] = r[:, c:, :] + _bdot('hij,hje->hie', at2[:, :, half * c:(half + 1) * c], vnb)
            s_ref[...] = s * gl_s[:, n][:, 0:1, :] + _bdot('hcd,hce->hde', kg_s[:, sl, :], vnb)
        return carry

    lax.fori_loop(0, nc // 2, chunk_pair, 0)

    z = _dot(xb, wqkvz_ref[:, GDN_CONV_CH:])
    o_ref[...] = _gdn_out([o_s[h] for h in range(HEADS)], z, x, ng_ref, wo_ref[...], g_ref, b_ref)

    @pl.when(ti == pl.num_programs(1) - 1)
    def _():
        sout_ref[...] = s_ref[...]


def _gdn_prompt(x, w, g, b, tm):
    bsz, t, _ = x.shape
    assert t % tm == 0 and tm % (2 * GDN_CHUNK) == 0
    c = GDN_CHUNK
    nc = tm // c
    blk = pl.BlockSpec((None, tm, D_MODEL), lambda bi, ti: (bi, ti, 0))
    consts = (w["wqkvz"], w["wba"], w["conv_w"], w["alog"], w["dtb"], w["ng"], w["wo"], g, b)
    head_bf16 = pltpu.VMEM((HEADS, tm, DK), BF16)
    return pl.pallas_call(
        functools.partial(_gdn_prompt_kernel, tm=tm),
        grid=(bsz, t // tm),
        in_specs=[blk] + [_resident(a.shape) for a in consts],
        out_specs=[blk,
                   pl.BlockSpec((None, GDN_CONV - 1, GDN_CONV_CH), lambda bi, ti: (bi, 0, 0)),
                   pl.BlockSpec((None, HEADS, DK, GDN_DV), lambda bi, ti: (bi, 0, 0, 0))],
        out_shape=[_f32(*x.shape), _f32(bsz, GDN_CONV - 1, GDN_CONV_CH), _f32(bsz, HEADS, DK, GDN_DV)],
        scratch_shapes=[pltpu.VMEM((GDN_CONV_CH // LANES, tm + CONV_HALO, LANES), F32),
                        pltpu.VMEM((2, 3, tm, LANES), F32),
                        pltpu.VMEM((HEADS, DK, GDN_DV), F32),
                        head_bf16, head_bf16, head_bf16,
                        pltpu.VMEM((HEADS, tm, 2 * GDN_DV), BF16),
                        head_bf16,
                        pltpu.VMEM((HEADS, nc, 2 * c, DK), BF16),
                        pltpu.VMEM((HEADS, nc, c, GDN_DV), F32),
                        pltpu.VMEM((HEADS, nc // 2, c, 2 * c), BF16),
                        pltpu.VMEM((HEADS, nc // 2, c, 2 * c), F32),
                        pltpu.VMEM((HEADS, nc, SUBLANES, LANES), F32),
                        pltpu.VMEM((HEADS, tm, GDN_DV), F32)],
        compiler_params=_params(2),
        name="gdn_prompt",
    )(x, *consts)


def _stream_projection(xb, win_ref, wb_ref, proj_s):
    wb = win_ref[...].astype(BF16)
    wb_ref[...] = wb
    proj_s[pl.program_id(0)] = _dot(xb, wb)


def _gdn_sample_a_kernel(x_ref, cb_ref, win_ref, wba_ref, cw_ref, alog_ref, dtb_ref,
                         rows_ref, pre_ref, z_ref, wb_ref, proj_s):
    xb = x_ref[...].astype(BF16)
    _stream_projection(xb, win_ref, wb_ref, proj_s)

    @pl.when(pl.program_id(0) == pl.num_programs(0) - 1)
    def _():
        nq = GDN_CONV_CH // W_COLS
        pre = jnp.concatenate([proj_s[i] for i in range(nq)], axis=-1)
        pre_ref[...] = pre
        z_ref[...] = proj_s[nq]
        y = pre * cw_ref[GDN_CONV - 1:GDN_CONV, :]
        for j in range(GDN_CONV - 1):
            y = y + cb_ref[j] * cw_ref[j:j + 1, :]
        qkv = _silu(y)
        beta_all, gl = _gdn_gates(_dot(xb, wba_ref[...]), alog_ref, dtb_ref)
        eg = jnp.exp(gl)
        rows = x_ref.shape[0]
        rows_ref[...] = jnp.zeros(rows_ref.shape, F32)
        for h in range(HEADS):
            hs = slice(h * DK, (h + 1) * DK)
            qn = _l2norm(qkv[:, hs], DK ** -0.5)
            kn = _l2norm(qkv[:, HEADS * DK + h * DK:HEADS * DK + (h + 1) * DK])
            vh = qkv[:, 2 * HEADS * DK + h * GDN_DV:2 * HEADS * DK + (h + 1) * GDN_DV]
            bh = beta_all[:, h:h + 1]
            egh = eg[:, HEADS + h:HEADS + h + 1]
            rows_ref[:, h, 0, :] = kn
            rows_ref[:, h, 1, :] = qn
            rows_ref[:, h, 2, :] = vh * bh
            rows_ref[:, h, 3, :] = jnp.broadcast_to(bh * egh, (rows, DK))
            rows_ref[:, h, 4, :] = jnp.broadcast_to(egh, (rows, DK))
            rows_ref[:, h, 5, :] = jnp.broadcast_to(jnp.sum(qn * kn, -1, keepdims=True), (rows, DK))


def _gdn_sample_b_kernel(p_ref, s_ref, o_ref, sout_ref):
    bb = p_ref.shape[0]
    n = bb * HEADS
    p = p_ref[...].reshape(n, SUBLANES, DK)
    s = s_ref[...].reshape(n, DK, GDN_DV)
    pb = p.astype(BF16)
    r = _bdot('nqd,nde->nqe', pb, s.astype(BF16))
    v_new = p[:, 2:3, :] - p[:, 3:4, :] * r[:, 0:1, :]
    o = p[:, 4:5, :] * r[:, 1:2, :] + p[:, 5:6, :] * v_new
    row0 = lax.broadcasted_iota(jnp.int32, (n, SUBLANES, GDN_DV), 1) == 0
    vn8 = jnp.where(row0, jnp.broadcast_to(v_new, (n, SUBLANES, GDN_DV)), 0.0)
    s_new = s * p[:, 4:5, :] + _bdot('nqd,nqe->nde', pb, vn8.astype(BF16))
    o_ref[...] = jnp.broadcast_to(o, (n, SUBLANES, GDN_DV)).reshape(bb, HEADS, SUBLANES, GDN_DV)
    sout_ref[...] = s_new.reshape(bb, HEADS, DK, GDN_DV)


def _gdn_sample_c_kernel(o_ref, z_ref, x_ref, ng_ref, wo_ref, g_ref, b_ref, out_ref, wob_ref):
    wob = wo_ref[...].astype(BF16)
    wob_ref[...] = wob
    out_ref[...] = _gdn_out([o_ref[:, h, 0, :] for h in range(HEADS)],
                            z_ref[...], x_ref[...], ng_ref, wob, g_ref, b_ref)


def _projection_call(kernel, resident_args, w_in_all, layer, n_steps, n_cols, outs, rows, name):
    return pl.pallas_call(
        kernel,
        grid=(n_steps,),
        in_specs=[_resident(a.shape) for a in resident_args[:2]]
        + [pl.BlockSpec((None, D_MODEL, W_COLS), lambda c: (layer, 0, c))]
        + [_resident(a.shape) for a in resident_args[2:]],
        out_specs=[_whole(o.shape) for o in outs] + [pl.BlockSpec((D_MODEL, W_COLS), lambda c: (0, c))],
        out_shape=list(outs) + [_bf16(D_MODEL, n_cols)],
        scratch_shapes=[pltpu.VMEM((n_steps, rows, W_COLS), F32)],
        compiler_params=_params(1),
        name=name,
    )(*resident_args[:2], w_in_all, *resident_args[2:])


def _state_step(kernel, p, extra, s0, bb, name):
    bsz = s0.shape[0]
    assert bsz % bb == 0

    def blk(a):
        return pl.BlockSpec((bb,) + a.shape[1:], lambda i: (i,) + (0,) * (a.ndim - 1))

    ins = (p,) + tuple(extra) + (s0,)
    o_shape = (bsz, HEADS, SUBLANES, s0.shape[-1])
    return pl.pallas_call(
        kernel,
        grid=(bsz // bb,),
        in_specs=[blk(a) for a in ins],
        out_specs=[pl.BlockSpec((bb,) + o_shape[1:], lambda i: (i, 0, 0, 0)), blk(s0)],
        out_shape=[_f32(*o_shape), _f32(*s0.shape)],
        compiler_params=_params(1),
        name=name,
    )(*ins)


def _gdn_sample(x2d, conv_buf, s0, w_in_all, w_out, layer, w, g, b):
    bsz = x2d.shape[0]
    cb_t = conv_buf.transpose(1, 0, 2)
    rows, pre, z, wqkvz = _projection_call(
        _gdn_sample_a_kernel, (x2d, cb_t, w["wba"], w["conv_w"], w["alog"], w["dtb"]), w_in_all, layer,
        GDN_QKVZ // W_COLS, GDN_QKVZ,
        [_f32(bsz, HEADS, SUBLANES, DK), _f32(bsz, GDN_CONV_CH), _f32(bsz, D_MODEL)], bsz, "gdn_sample_a")
    o8, s_new = _state_step(_gdn_sample_b_kernel, rows, (), s0, 8, "gdn_sample_b")
    x1, wob = _whole_call(_gdn_sample_c_kernel, (o8, z, x2d, w["ng"], w_out, g, b),
                          [_f32(bsz, D_MODEL), _bf16(*w_out.shape)], "gdn_sample_c")
    new_conv = jnp.concatenate([conv_buf[:, 1:], pre[:, None, :]], axis=1)
    return x1, new_conv, s_new, wqkvz, wob


def _log_gamma(h):
    return math.log(1.0 - 2.0 ** (-5.0 - h))


def _rope_tables(pos, freq_ref):
    ang = pos * freq_ref[...]
    lane = lax.broadcasted_iota(jnp.int32, ang.shape, 1)
    return jnp.cos(ang), jnp.where(lane < DK // 2, -1.0, 1.0) * jnp.sin(ang)


def _rope(v, cos2, sin2):
    return v * cos2 + pltpu.roll(v, DK // 2, 1) * sin2


def _ret_out(o_heads, gate, x, wo, g_ref, b_ref):
    gated = []
    for h, oh in enumerate(o_heads):
        mu = jnp.mean(oh, -1, keepdims=True)
        oc = oh - mu
        on = oc * lax.rsqrt(jnp.mean(oc * oc, -1, keepdims=True) + LN_EPS)
        gated.append((_silu(gate[:, h * RET_DV:(h + 1) * RET_DV]) * on).astype(BF16))
    mix = _dot(jnp.concatenate(gated, axis=-1), wo)
    return _layer_norm(DN_ALPHA * x + mix, g_ref[...], b_ref[...])


def _ret_prompt_kernel(x_ref, win_ref, freq_ref, wo_ref, g_ref, b_ref, o_ref, sout_ref,
                       s_ref, q_s, k_s, qx_s, kz_s, v_s, dec_s, o_s, *, tm, c, pos0):
    nc = tm // c
    ti = pl.program_id(1)
    hk = HEADS * DK

    @pl.when(ti == 0)
    def _():
        s_ref[...] = jnp.zeros(s_ref.shape, F32)
        diff = (lax.broadcasted_iota(jnp.int32, (c, c), 0) - lax.broadcasted_iota(jnp.int32, (c, c), 1)).astype(F32)
        for h in range(HEADS):
            dec_s[h] = jnp.where(diff >= 0, jnp.exp(jnp.maximum(diff, 0.0) * _log_gamma(h)), 0.0)

    x = x_ref[...]
    xb = x.astype(BF16)
    row = lax.broadcasted_iota(jnp.int32, (tm, 1), 0)
    cos2, sin2 = _rope_tables((pos0 + ti * tm + row).astype(F32), freq_ref)
    idx = (row % c).astype(F32)
    qall = _dot(xb, win_ref[:, 0:hk])
    kall = _dot(xb, win_ref[:, hk:2 * hk])
    vall = _dot(xb, win_ref[:, 2 * hk:2 * hk + HEADS * RET_DV])
    for h in range(HEADS):
        hs = slice(h * DK, (h + 1) * DK)
        qr = _rope(qall[:, hs], cos2, sin2)
        kr = _rope(kall[:, hs], cos2, sin2) * DK ** -0.5
        q_s[h] = qr.astype(BF16)
        k_s[h] = kr.astype(BF16)
        qx_s[h] = (qr * jnp.exp((idx + 1.0) * _log_gamma(h))).astype(BF16)
        kz_s[h] = (kr * jnp.exp((c - 1.0 - idx) * _log_gamma(h))).astype(BF16)
        v_s[h] = vall[:, h * RET_DV:(h + 1) * RET_DV].astype(BF16)

    for n in range(nc):
        sl = slice(n * c, (n + 1) * c)
        v_c = v_s[:, sl, :]
        attn = _bdot('hid,hjd->hij', q_s[:, sl, :], k_s[:, sl, :]) * dec_s[...]
        s = s_ref[...]
        o_s[:, sl, :] = (_bdot('hij,hje->hie', attn.astype(BF16), v_c)
                         + _bdot('hcd,hde->hce', qx_s[:, sl, :], s.astype(BF16)))
        upd = _bdot('hcd,hce->hde', kz_s[:, sl, :], v_c)
        for h in range(HEADS):
            s_ref[h] = math.exp(c * _log_gamma(h)) * s[h] + upd[h]

    gate = _dot(xb, win_ref[:, 2 * hk + HEADS * RET_DV:])
    o_ref[...] = _ret_out([o_s[h] for h in range(HEADS)], gate, x, wo_ref[...], g_ref, b_ref)

    @pl.when(ti == pl.num_programs(1) - 1)
    def _():
        sout_ref[...] = s_ref[...]


def _ret_prompt(x, w, g, b, tm):
    bsz, t, _ = x.shape
    c = min(RET_CHUNK, tm)
    assert t % tm == 0 and tm % c == 0
    blk = pl.BlockSpec((None, tm, D_MODEL), lambda bi, ti: (bi, ti, 0))
    consts = (w["win"], w["freq"], w["wo"], g, b)
    head_k = pltpu.VMEM((HEADS, tm, DK), BF16)
    return pl.pallas_call(
        functools.partial(_ret_prompt_kernel, tm=tm, c=c, pos0=0),
        grid=(bsz, t // tm),
        in_specs=[blk] + [_resident(a.shape) for a in consts],
        out_specs=[blk, pl.BlockSpec((None, HEADS, DK, RET_DV), lambda bi, ti: (bi, 0, 0, 0))],
        out_shape=[_f32(*x.shape), _f32(bsz, HEADS, DK, RET_DV)],
        scratch_shapes=[pltpu.VMEM((HEADS, DK, RET_DV), F32), head_k, head_k, head_k, head_k,
                        pltpu.VMEM((HEADS, tm, RET_DV), BF16),
                        pltpu.VMEM((HEADS, c, c), F32),
                        pltpu.VMEM((HEADS, tm, RET_DV), F32)],
        compiler_params=_params(2),
        name="ret_prompt",
    )(x, *consts)


def _ret_sample_a_kernel(x_ref, freq_ref, win_ref, kq_ref, vrows_ref, gate_ref, wb_ref, proj_s, *, pos0):
    xb = x_ref[...].astype(BF16)
    _stream_projection(xb, win_ref, wb_ref, proj_s)

    @pl.when(pl.program_id(0) == pl.num_programs(0) - 1)
    def _():
        rows = x_ref.shape[0]
        cos2, sin2 = _rope_tables(jnp.full((1, 1), pos0, F32), freq_ref)
        qall, kall = proj_s[0], proj_s[1]
        gate_ref[...] = jnp.concatenate([proj_s[4], proj_s[5]], axis=-1)
        kq_ref[...] = jnp.zeros(kq_ref.shape, F32)
        vrows_ref[...] = jnp.zeros(vrows_ref.shape, F32)
        for h in range(HEADS):
            hs = slice(h * DK, (h + 1) * DK)
            qr = _rope(qall[:, hs], cos2, sin2)
            kr = _rope(kall[:, hs], cos2, sin2) * DK ** -0.5
            kq_ref[:, h, 0, :] = kr
            kq_ref[:, h, 1, :] = qr
            vh = proj_s[2 + h // 4][:, (h % 4) * RET_DV:(h % 4 + 1) * RET_DV]
            vrows_ref[:, h, 0, :] = vh
            vrows_ref[:, h, 1, :] = jnp.broadcast_to(jnp.sum(qr * kr, -1, keepdims=True), (rows, RET_DV))
            vrows_ref[:, h, 2, :] = jnp.full((rows, RET_DV), math.exp(_log_gamma(h)), F32)


def _ret_sample_b_kernel(kq_ref, v_ref, s_ref, o_ref, sout_ref):
    bb = kq_ref.shape[0]
    n = bb * HEADS
    kqb = kq_ref[...].reshape(n, SUBLANES, DK).astype(BF16)
    v8 = v_ref[...].reshape(n, SUBLANES, RET_DV)
    s = s_ref[...].reshape(n, DK, RET_DV)
    r = _bdot('nqd,nde->nqe', kqb, s.astype(BF16))
    gamma = v8[:, 2:3, :]
    o = v8[:, 1:2, :] * v8[:, 0:1, :] + gamma * r[:, 1:2, :]
    row0 = lax.broadcasted_iota(jnp.int32, (n, SUBLANES, RET_DV), 1) == 0
    s_new = s * gamma + _bdot('nqd,nqe->nde', kqb, jnp.where(row0, v8, 0.0).astype(BF16))
    o_ref[...] = jnp.broadcast_to(o, (n, SUBLANES, RET_DV)).reshape(bb, HEADS, SUBLANES, RET_DV)
    sout_ref[...] = s_new.reshape(bb, HEADS, DK, RET_DV)


def _ret_sample_c_kernel(o_ref, gate_ref, x_ref, wo_ref, g_ref, b_ref, out_ref, wob_ref):
    wob = wo_ref[...].astype(BF16)
    wob_ref[...] = wob
    out_ref[...] = _ret_out([o_ref[:, h, 0, :] for h in range(HEADS)],
                            gate_ref[...], x_ref[...], wob, g_ref, b_ref)


def _ret_sample(x2d, s0, w_in_all, w_out, layer, freq, g, b):
    bsz = x2d.shape[0]
    kq, vrows, gate, winb = _projection_call(
        functools.partial(_ret_sample_a_kernel, pos0=float(PAST_LEN)), (x2d, freq), w_in_all, layer,
        RET_IN // W_COLS, RET_IN,
        [_f32(bsz, HEADS, SUBLANES, DK), _f32(bsz, HEADS, SUBLANES, RET_DV), _f32(bsz, HEADS * RET_DV)],
        bsz, "ret_sample_a")
    o8, s_new = _state_step(_ret_sample_b_kernel, kq, (vrows,), s0, 4, "ret_sample_b")
    x1, wob = _whole_call(_ret_sample_c_kernel, (o8, gate, x2d, w_out, g, b),
                          [_f32(bsz, D_MODEL), _bf16(*w_out.shape)], "ret_sample_c")
    return x1, s_new, winb, wob


def _lane_row(values, offset):
    return jnp.zeros((1, LANES), F32).at[0, offset:offset + HEADS].set(values.astype(F32))


def _gdn_small_weights(w_in, conv_w, a_log, dt_bias, norm_g):
    wba = jnp.zeros((D_MODEL, LANES), F32).at[:, :2 * HEADS].set(w_in[:, GDN_QKVZ:])
    return {"wba": wba.astype(BF16), "conv_w": conv_w,
            "alog": _lane_row(a_log, HEADS), "dtb": _lane_row(dt_bias, HEADS), "ng": norm_g.reshape(1, GDN_DV)}


def _rope_freqs():
    half = DK // 2
    freqs = ROPE_BASE ** (-jnp.arange(half, dtype=F32) / half)
    return jnp.concatenate([freqs, freqs]).reshape(1, DK)


def kernel(x_prompt, x_sample, state_pool, state_gdn_conv, state_gdn, state_ret, pool_w, pool_scale, gdn_w_in,
           gdn_conv_w, gdn_a_log, gdn_dt_bias, gdn_norm_g, gdn_w_out, ret_w_in, ret_w_out, ffn_w13, ffn_w2,
           ln_g, ln_b):
    bsz, seq, _ = x_prompt.shape
    dec = x_sample.shape[0]
    tm = min(512, seq)
    xp = x_prompt
    xs = x_sample.reshape(dec, D_MODEL)
    pool_p, pool_s, conv_p, conv_s, gdn_p, gdn_s, ret_p, ret_s = [], [], [], [], [], [], [], []
    for i in range(DEPTH):
        kind, j = i % N_MIXERS, i // N_MIXERS
        g1, b1 = ln_g[i, 0].reshape(1, D_MODEL), ln_b[i, 0].reshape(1, D_MODEL)
        g2, b2 = ln_g[i, 1].reshape(1, D_MODEL), ln_b[i, 1].reshape(1, D_MODEL)
        if kind == 0:
            sc = pool_scale[j].reshape(1, D_MODEL)
            pool_p.append(xp[:, seq - POOL_BUF:])
            xs, pwb, nbuf = _pool_sample(xs, state_pool[j].transpose(1, 0, 2), pool_w, j, sc, g1, b1)
            pool_s.append(nbuf)
            xs, *fw = _ffn_cast(xs, ffn_w13, ffn_w2, i, g2, b2)
            xp = _pool_ffn_prompt(xp, pwb, sc, g1, b1, fw, g2, b2, min(2 * tm, seq), tm)
            continue
        if kind == 1:
            w = _gdn_small_weights(gdn_w_in[j], gdn_conv_w[j], gdn_a_log[j], gdn_dt_bias[j], gdn_norm_g[j])
            xs, cs, ss, w["wqkvz"], w["wo"] = _gdn_sample(xs, state_gdn_conv[j], state_gdn[j], gdn_w_in,
                                                         gdn_w_out[j], j, w, g1, b1)
            xp, cp, sp = _gdn_prompt(xp, w, g1, b1, tm)
            conv_p.append(cp), conv_s.append(cs), gdn_p.append(sp), gdn_s.append(ss)
        else:
            w = {"freq": _rope_freqs()}
            xs, ss, w["win"], w["wo"] = _ret_sample(xs, state_ret[j], ret_w_in, ret_w_out[j], j, w["freq"], g1, b1)
            xp, sp = _ret_prompt(xp, w, g1, b1, tm)
            ret_p.append(sp), ret_s.append(ss)
        xs, *fw = _ffn_cast(xs, ffn_w13, ffn_w2, i, g2, b2)
        xp = _ffn(xp.reshape(bsz * seq, D_MODEL), fw, g2, b2, min(2 * tm, seq), tm).reshape(bsz, seq, D_MODEL)
    return (xp, xs.reshape(dec, 1, D_MODEL), jnp.stack(pool_p), jnp.stack(pool_s).transpose(0, 2, 1, 3), jnp.stack(conv_p),
            jnp.stack(conv_s), jnp.stack(gdn_p), jnp.stack(gdn_s), jnp.stack(ret_p), jnp.stack(ret_s))
```

```python
import functools
import math

import jax
import jax.numpy as jnp
from jax import lax
from jax.experimental import pallas as pl
from jax.experimental.pallas import tpu as pltpu

F32 = jnp.float32
BF16 = jnp.bfloat16

D_MODEL = 1024
DEPTH = 4
PAST_LEN = 16384
N_MIXERS = 3

POOL_WINDOWS = (2, 4, 8, 16)
POOL_GROUP = D_MODEL // len(POOL_WINDOWS)
POOL_BUF = max(POOL_WINDOWS) - 1
POOL_HALO = 16
POOL_ROWS = 64

HEADS = 8
DK = D_MODEL // HEADS
GDN_DV = DK
GDN_CONV = 4
GDN_CONV_CH = HEADS * (2 * DK + GDN_DV)
GDN_QKVZ = GDN_CONV_CH + HEADS * GDN_DV
CONV_HALO = 8
CONV_PHASES = 4
GDN_CHUNK = 64
HEAD_GROUP = 8
RET_DV = 2 * DK
RET_IN = 2 * HEADS * DK + 2 * HEADS * RET_DV
RET_CHUNK = 256
ROPE_BASE = 10000.0
W_COLS = 1024

D_FF = -(-8 * D_MODEL // (3 * 256)) * 256
FF_CHUNK = 256
CAST_CHUNK = 256
DN_ALPHA = (2 * DEPTH) ** 0.25
LN_EPS = 1e-5
RMS_EPS = 1e-6

SUBLANES = 8
LANES = 128
VMEM_LIMIT = 56 * 1024 * 1024


def _params(n_grid):
    return pltpu.CompilerParams(dimension_semantics=("arbitrary",) * n_grid, vmem_limit_bytes=VMEM_LIMIT)


def _resident(shape):
    nd = len(shape)
    return pl.BlockSpec(shape, lambda *_: (0,) * nd, pipeline_mode=pl.Buffered(1))


def _whole(shape):
    nd = len(shape)
    return pl.BlockSpec(shape, lambda *_: (0,) * nd)


def _dot(a, b):
    return jnp.dot(a, b, preferred_element_type=F32)


def _bdot(spec, a, b):
    return jnp.einsum(spec, a, b, preferred_element_type=F32)


def _silu(x):
    return x * jax.nn.sigmoid(x)


def _layer_norm(y, g, b):
    mu = jnp.mean(y, -1, keepdims=True)
    yc = y - mu
    var = jnp.mean(yc * yc, -1, keepdims=True)
    return yc * lax.rsqrt(var + LN_EPS) * g + b


def _f32(*shape):
    return jax.ShapeDtypeStruct(shape, F32)


def _bf16(*shape):
    return jax.ShapeDtypeStruct(shape, BF16)


def _ffn_tail(x, w1_ref, w3_ref, w2_ref, g_ref, b_ref):
    xb = x.astype(BF16)
    acc = None
    for c in range(D_FF // FF_CHUNK):
        lo = c * FF_CHUNK
        a = _dot(xb, w1_ref[:, lo:lo + FF_CHUNK])
        b = _dot(xb, w3_ref[:, lo:lo + FF_CHUNK])
        h = (_silu(a) * b).astype(BF16)
        p = _dot(h, w2_ref[lo:lo + FF_CHUNK, :])
        acc = p if acc is None else acc + p
    return _layer_norm(DN_ALPHA * x + acc, g_ref[...], b_ref[...])


def _ffn_kernel(x_ref, w1_ref, w3_ref, w2_ref, g_ref, b_ref, o_ref, *, sub):
    for r0 in range(0, x_ref.shape[0], sub):
        o_ref[r0:r0 + sub, :] = _ffn_tail(x_ref[r0:r0 + sub, :], w1_ref, w3_ref, w2_ref, g_ref, b_ref)


def _ffn(x2d, fw, g, b, tm, sub):
    n = x2d.shape[0]
    assert n % tm == 0 and tm % sub == 0
    row = pl.BlockSpec((tm, D_MODEL), lambda i: (i, 0))
    consts = tuple(fw) + (g, b)
    return pl.pallas_call(
        functools.partial(_ffn_kernel, sub=sub),
        grid=(n // tm,),
        in_specs=[row] + [_resident(a.shape) for a in consts],
        out_specs=row,
        out_shape=_f32(n, D_MODEL),
        compiler_params=_params(1),
        name="ffn",
    )(x2d, *consts)


def _ffn_cast_kernel(x_ref, w1_ref, w3_ref, w2_ref, g_ref, b_ref, o_ref, w1b_ref, w3b_ref, w2b_ref, acc_ref):
    c = pl.program_id(0)

    @pl.when(c == 0)
    def _():
        acc_ref[...] = jnp.zeros(acc_ref.shape, F32)

    xb = x_ref[...].astype(BF16)
    w1b, w3b, w2b = w1_ref[...].astype(BF16), w3_ref[...].astype(BF16), w2_ref[...].astype(BF16)
    w1b_ref[...], w3b_ref[...], w2b_ref[...] = w1b, w3b, w2b
    h = (_silu(_dot(xb, w1b)) * _dot(xb, w3b)).astype(BF16)
    acc_ref[...] += _dot(h, w2b)

    @pl.when(c == pl.num_programs(0) - 1)
    def _():
        o_ref[...] = _layer_norm(DN_ALPHA * x_ref[...] + acc_ref[...], g_ref[...], b_ref[...])


def _ffn_cast(x2d, w13_all, w2_all, layer, g, b):
    n = x2d.shape[0]
    cc = CAST_CHUNK
    nch = D_FF // cc
    return pl.pallas_call(
        _ffn_cast_kernel,
        grid=(nch,),
        in_specs=[_resident(x2d.shape),
                  pl.BlockSpec((None, D_MODEL, cc), lambda c: (layer, 0, c)),
                  pl.BlockSpec((None, D_MODEL, cc), lambda c: (layer, 0, nch + c)),
                  pl.BlockSpec((None, cc, D_MODEL), lambda c: (layer, c, 0)),
                  _resident(g.shape), _resident(b.shape)],
        out_specs=[_whole(x2d.shape),
                   pl.BlockSpec((D_MODEL, cc), lambda c: (0, c)),
                   pl.BlockSpec((D_MODEL, cc), lambda c: (0, c)),
                   pl.BlockSpec((cc, D_MODEL), lambda c: (c, 0))],
        out_shape=[_f32(n, D_MODEL), _bf16(D_MODEL, D_FF), _bf16(D_MODEL, D_FF), _bf16(D_FF, D_MODEL)],
        scratch_shapes=[pltpu.VMEM((n, D_MODEL), F32)],
        compiler_params=_params(1),
        name="ffn_cast",
    )(x2d, w13_all, w13_all, w2_all, g, b)


def _rows_up(v, s):
    rows, w = v.shape
    if s % SUBLANES:
        sub = lax.broadcasted_iota(jnp.int32, v.shape, 0) % SUBLANES
        v = pltpu.roll(v.reshape(rows // SUBLANES, SUBLANES, w), s % SUBLANES, 1).reshape(rows, w)
        v = jnp.where(sub >= s % SUBLANES, v, jnp.concatenate([v[0:SUBLANES], v[0:rows - SUBLANES]], axis=0))
    whole = s - s % SUBLANES
    return jnp.concatenate([v[0:whole], v[0:rows - whole]], axis=0) if whole else v


def _trailing_sums(e, win):
    s, w = e, 1
    while w < win:
        s = s + _rows_up(s, w)
        w *= 2
    return s


def _pool_ffn_prompt_kernel(x_ref, pw_ref, sc_ref, g1_ref, b1_ref, w1_ref, w3_ref, w2_ref, g2_ref, b2_ref,
                            o_ref, ext_ref, pooled_ref, *, tm, sub, n_valid):
    ti = pl.program_id(1)
    rb = POOL_ROWS

    @pl.when(ti == 0)
    def _():
        ext_ref[0:POOL_HALO, :] = jnp.zeros((POOL_HALO, D_MODEL), F32)

    ext_ref[POOL_HALO:POOL_HALO + tm, :] = x_ref[...]

    def pool_rows(r0):
        t = ti * tm + r0 + lax.broadcasted_iota(jnp.int32, (rb, 1), 0)
        for gi, win in enumerate(POOL_WINDOWS):
            lo, hi = gi * POOL_GROUP, (gi + 1) * POOL_GROUP
            e = ext_ref[r0:r0 + rb + POOL_HALO, lo:hi]
            cnt = jnp.minimum(t + 1 + n_valid, win).astype(F32)
            pooled = _trailing_sums(e, win)[POOL_HALO:] / cnt - e[POOL_HALO:]
            pooled_ref[r0:r0 + rb, lo:hi] = pooled.astype(BF16)

    for r0 in range(0, tm, rb):
        pool_rows(r0)
    ext_ref[0:POOL_HALO, :] = ext_ref[tm:tm + POOL_HALO, :]
    for s0 in range(0, tm, sub):
        mix = jnp.concatenate(
            [_dot(pooled_ref[s0:s0 + sub, gi * POOL_GROUP:(gi + 1) * POOL_GROUP], pw_ref[gi])
             for gi in range(len(POOL_WINDOWS))], axis=-1) * sc_ref[...]
        x1 = _layer_norm(DN_ALPHA * x_ref[s0:s0 + sub, :] + mix, g1_ref[...], b1_ref[...])
        o_ref[s0:s0 + sub, :] = _ffn_tail(x1, w1_ref, w3_ref, w2_ref, g2_ref, b2_ref)


def _pool_ffn_prompt(x, pwb, sc, g1, b1, fw, g2, b2, tm, sub):
    bsz, t, _ = x.shape
    assert t % tm == 0 and tm % sub == 0 and sub % POOL_ROWS == 0
    blk = pl.BlockSpec((None, tm, D_MODEL), lambda bi, ti: (bi, ti, 0))
    consts = (pwb, sc, g1, b1) + tuple(fw) + (g2, b2)
    return pl.pallas_call(
        functools.partial(_pool_ffn_prompt_kernel, tm=tm, sub=sub, n_valid=0),
        grid=(bsz, t // tm),
        in_specs=[blk] + [_resident(a.shape) for a in consts],
        out_specs=blk,
        out_shape=_f32(*x.shape),
        scratch_shapes=[pltpu.VMEM((tm + POOL_HALO, D_MODEL), F32), pltpu.VMEM((tm, D_MODEL), BF16)],
        compiler_params=_params(2),
        name="pool_ffn_prompt",
    )(x, *consts)


def _pool_sample_kernel(x_ref, buf_ref, pw_ref, sc_ref, g_ref, b_ref, o_ref, pwb_ref, nbuf_ref, *, n_valid):
    x = x_ref[...]
    pwb = pw_ref[...].astype(BF16)
    pwb_ref[...] = pwb
    outs = []
    for gi, win in enumerate(POOL_WINDOWS):
        lo, hi = gi * POOL_GROUP, (gi + 1) * POOL_GROUP
        xs = x[:, lo:hi]
        s = xs
        for d in range(1, win):
            s = s + buf_ref[POOL_BUF - d, :, lo:hi]
        pooled = s / float(min(1 + n_valid, win)) - xs
        outs.append(_dot(pooled.astype(BF16), pwb[gi]))
    mix = jnp.concatenate(outs, axis=-1) * sc_ref[...]
    o_ref[...] = _layer_norm(DN_ALPHA * x + mix, g_ref[...], b_ref[...])
    nbuf_ref[0:POOL_BUF - 1] = buf_ref[1:POOL_BUF]
    nbuf_ref[POOL_BUF - 1] = x


def _layer_of(stacked, layer):
    nd = stacked.ndim - 1
    return pl.BlockSpec((None,) + stacked.shape[1:], lambda *_: (layer,) + (0,) * nd, pipeline_mode=pl.Buffered(1))


def _whole_call(kernel, args, outs, name, in_specs=None):
    return pl.pallas_call(
        kernel,
        grid=(1,),
        in_specs=in_specs or [_resident(a.shape) for a in args],
        out_specs=[_whole(o.shape) for o in outs],
        out_shape=list(outs),
        compiler_params=_params(1),
        name=name,
    )(*args)


def _pool_sample(x2d, buf_all, pw_all, layer, sc, g, b):
    args = (x2d, buf_all, pw_all, sc, g, b)
    specs = [_layer_of(a, layer) if a is buf_all or a is pw_all else _resident(a.shape) for a in args]
    return _whole_call(functools.partial(_pool_sample_kernel, n_valid=min(PAST_LEN, POOL_BUF)), args,
                       [_f32(*x2d.shape), _bf16(*pw_all.shape[1:]), _f32(*buf_all.shape[1:])], "pool_sample", specs)


def _softplus(x):
    return jnp.maximum(x, 0.0) + jnp.log1p(jnp.exp(-jnp.abs(x)))


def _gdn_gates(ba, alog_ref, dtb_ref):
    return jax.nn.sigmoid(ba), -jnp.exp(alog_ref[...]) * _softplus(ba + dtb_ref[...])


def _l2norm(v, scale=1.0):
    return v * (lax.rsqrt(jnp.sum(v * v, -1, keepdims=True) + RMS_EPS) * scale)


def _gdn_out(o_heads, z, x, ng_ref, wo, g_ref, b_ref):
    gated = []
    for h, oh in enumerate(o_heads):
        on = oh * lax.rsqrt(jnp.mean(oh * oh, -1, keepdims=True) + RMS_EPS) * ng_ref[...]
        gated.append((on * _silu(z[:, h * GDN_DV:(h + 1) * GDN_DV])).astype(BF16))
    mix = _dot(jnp.concatenate(gated, axis=-1), wo)
    return _layer_norm(DN_ALPHA * x + mix, g_ref[...], b_ref[...])


def _pair_index(c):
    ii = lax.broadcasted_iota(jnp.int32, (c, 2 * c), 0)
    lane = lax.broadcasted_iota(jnp.int32, (c, 2 * c), 1)
    return ii, lane % c, lane < c


def _block_diag2(p):
    _, _, left = _pair_index(p.shape[1])
    return jnp.concatenate([jnp.where(left, p, 0.0).astype(BF16), jnp.where(left, 0.0, p).astype(BF16)], axis=1)


def _unit_lower_inverse_pairs(a):
    c = a.shape[1]
    ii, jj, _ = _pair_index(c)
    x = jnp.where(ii == jj, 1.0, 0.0) - jnp.where((ii // 2 == jj // 2) & (ii > jj), a, 0.0)
    s = 4
    while s <= c:
        off = jnp.where((ii // s == jj // s) & (ii // (s // 2) != jj // (s // 2)) & (ii > jj), a, 0.0)
        y = _bdot('bik,bkj->bij', off.astype(BF16), _block_diag2(x))
        x = x - _bdot('bik,bkj->bij', x.astype(BF16), _block_diag2(y))
        s *= 2
    return x


def _gdn_prompt_kernel(x_ref, wqkvz_ref, wba_ref, cw_ref, alog_ref, dtb_ref, ng_ref, wo_ref, g_ref, b_ref,
                       o_ref, conv_ref, sout_ref,
                       ext_s, act_s, s_ref, q_s, k_s, kb_s, vbk_s, kg_s, wkqg_s, u_s, at_s, dec_s, gl_s,
                       o_s, *, tm):
    c = GDN_CHUNK
    nc = tm // c
    ti = pl.program_id(1)
    nslab = GDN_CONV_CH // LANES

    @pl.when(ti == 0)
    def _():
        ext_s[:, 0:CONV_HALO, :] = jnp.zeros((nslab, CONV_HALO, LANES), F32)
        s_ref[...] = jnp.zeros(s_ref.shape, F32)

    x = x_ref[...]
    xb = x.astype(BF16)

    def project(p):
        for part in range(3):
            slab = part * HEADS + 2 * p
            pre = _dot(xb, wqkvz_ref[:, slab * DK:(slab + 2) * DK])
            ext_s[slab, CONV_HALO:CONV_HALO + tm, :] = pre[:, 0:DK]
            ext_s[slab + 1, CONV_HALO:CONV_HALO + tm, :] = pre[:, DK:]

    def conv_silu(part, h):
        slab = part * HEADS + h
        rows = tm // CONV_PHASES
        for ph in range(CONV_PHASES):
            y = None
            for j in range(GDN_CONV):
                start = CONV_HALO + ph - (GDN_CONV - 1 - j)
                term = ext_s[slab, pl.ds(start, rows, stride=CONV_PHASES), :] * cw_ref[j:j + 1, slab * DK:(slab + 1) * DK]
                y = term if y is None else y + term
            act_s[h % 2, part, pl.ds(ph, rows, stride=CONV_PHASES), :] = _silu(y)

    beta_all, gl = _gdn_gates(_dot(xb, wba_ref[...]), alog_ref, dtb_ref)
    pos = lax.broadcasted_iota(jnp.int32, (tm, LANES), 0) % c
    gc = gl
    step = 1
    while step < c:
        gc = gc + jnp.where(pos >= step, pltpu.roll(gc, step, 0), 0.0)
        step *= 2
    gcl = jnp.broadcast_to(gc.reshape(nc, c, LANES)[:, c - 1:c, :], (nc, c, LANES)).reshape(tm, LANES)
    eg = jnp.exp(gc)
    kdec = jnp.exp(gcl - gc)
    glast = jnp.exp(gcl)
    gct = gc.T

    ii, jj, left = _pair_index(c)
    lower = ii >= jj
    for h in range(HEADS):
        if h % 2 == 0:
            project(h // 2)
        for part in range(3):
            conv_silu(part, h)
        for n in range(nc):
            rs = slice(n * c, (n + 1) * c)
            qn = _l2norm(act_s[h % 2, 0, rs, :], DK ** -0.5)
            kn = _l2norm(act_s[h % 2, 1, rs, :])
            vh = act_s[h % 2, 2, rs, :]
            bh = beta_all[rs, h:h + 1]
            egh = eg[rs, HEADS + h:HEADS + h + 1]
            kb = kn * bh
            q_s[h, rs] = qn.astype(BF16)
            k_s[h, rs] = kn.astype(BF16)
            kb_s[h, rs] = kb.astype(BF16)
            vbk_s[h, rs, 0:GDN_DV] = (vh * bh).astype(BF16)
            vbk_s[h, rs, GDN_DV:] = (kb * egh).astype(BF16)
            wkqg_s[h, n, c:, :] = (qn * egh).astype(BF16)
            kg_s[h, rs] = (kn * kdec[rs, HEADS + h:HEADS + h + 1]).astype(BF16)
            gl_s[h, n] = jnp.broadcast_to(glast[n * c:n * c + SUBLANES, HEADS + h:HEADS + h + 1], (SUBLANES, LANES))
        for m in range(nc // 2):
            col = jnp.where(left, gc[2 * m * c:(2 * m + 1) * c, HEADS + h:HEADS + h + 1],
                            gc[(2 * m + 1) * c:(2 * m + 2) * c, HEADS + h:HEADS + h + 1])
            row = gct[HEADS + h:HEADS + h + 1, 2 * m * c:(2 * m + 2) * c]
            dec_s[h, m] = jnp.where(lower, jnp.exp(jnp.where(lower, col - row, 0.0)), 0.0)

    for slab in range(nslab):
        conv_ref[:, slab * DK:(slab + 1) * DK] = ext_s[slab, tm + CONV_HALO - (GDN_CONV - 1):tm + CONV_HALO, :]
    ext_s[:, 0:CONV_HALO, :] = ext_s[:, tm:tm + CONV_HALO, :]

    nb = HEAD_GROUP * nc // 2

    def diag_pair(g):
        return jnp.where(left, g[:, 0:c, :], g[:, c:, :])

    def precompute(gi, carry):
        hs = pl.ds(gi * HEAD_GROUP, HEAD_GROUP)
        dec = dec_s[hs].reshape(nb, c, 2 * c)
        kn2 = k_s[hs].reshape(nb, 2 * c, DK)
        a = diag_pair(_bdot('bid,bjd->bij', kb_s[hs].reshape(nb, 2 * c, DK), kn2))
        tinv = _unit_lower_inverse_pairs(jnp.where(ii > jj, a * dec, 0.0))
        uw = _bdot('bik,bkd->bid', _block_diag2(tinv), vbk_s[hs].reshape(nb, 2 * c, 2 * GDN_DV))
        u_s[hs] = uw[:, :, 0:GDN_DV].reshape(HEAD_GROUP, nc, c, GDN_DV)
        wkqg_s[hs, :, 0:c, :] = uw[:, :, GDN_DV:].astype(BF16).reshape(HEAD_GROUP, nc, c, DK)
        attn = diag_pair(_bdot('bid,bjd->bij', q_s[hs].reshape(nb, 2 * c, DK), kn2)) * dec
        at_s[hs] = attn.astype(BF16).reshape(HEAD_GROUP, nc // 2, c, 2 * c)
        return carry

    lax.fori_loop(0, HEADS // HEAD_GROUP, precompute, 0)

    def chunk_pair(m, carry):
        at2 = at_s[:, m]
        for half in range(2):
            n = 2 * m + half
            sl = pl.ds(pl.multiple_of(n * c, c), c)
            s = s_ref[...]
            r = _bdot('hid,hde->hie', wkqg_s[:, n], s.astype(BF16))
            vnb = (u_s[:, n] - r[:, 0:c, :]).astype(BF16)
            o_s[:, sl, :] = r[:, c:, :] + _bdot('hij,hje->hie', at2[:, :, half * c:(half + 1) * c], vnb)
            s_ref[...] = s * gl_s[:, n][:, 0:1, :] + _bdot('hcd,hce->hde', kg_s[:, sl, :], vnb)
        return carry

    lax.fori_loop(0, nc // 2, chunk_pair, 0)

    z = _dot(xb, wqkvz_ref[:, GDN_CONV_CH:])
    o_ref[...] = _gdn_out([o_s[h] for h in range(HEADS)], z, x, ng_ref, wo_ref[...], g_ref, b_ref)

    @pl.when(ti == pl.num_programs(1) - 1)
    def _():
        sout_ref[...] = s_ref[...]


def _gdn_prompt(x, w, g, b, tm):
    bsz, t, _ = x.shape
    assert t % tm == 0 and tm % (2 * GDN_CHUNK) == 0
    c = GDN_CHUNK
    nc = tm // c
    blk = pl.BlockSpec((None, tm, D_MODEL), lambda bi, ti: (bi, ti, 0))
    consts = (w["wqkvz"], w["wba"], w["conv_w"], w["alog"], w["dtb"], w["ng"], w["wo"], g, b)
    head_bf16 = pltpu.VMEM((HEADS, tm, DK), BF16)
    return pl.pallas_call(
        functools.partial(_gdn_prompt_kernel, tm=tm),
        grid=(bsz, t // tm),
        in_specs=[blk] + [_resident(a.shape) for a in consts],
        out_specs=[blk,
                   pl.BlockSpec((None, GDN_CONV - 1, GDN_CONV_CH), lambda bi, ti: (bi, 0, 0)),
                   pl.BlockSpec((None, HEADS, DK, GDN_DV), lambda bi, ti: (bi, 0, 0, 0))],
        out_shape=[_f32(*x.shape), _f32(bsz, GDN_CONV - 1, GDN_CONV_CH), _f32(bsz, HEADS, DK, GDN_DV)],
        scratch_shapes=[pltpu.VMEM((GDN_CONV_CH // LANES, tm + CONV_HALO, LANES), F32),
                        pltpu.VMEM((2, 3, tm, LANES), F32),
                        pltpu.VMEM((HEADS, DK, GDN_DV), F32),
                        head_bf16, head_bf16, head_bf16,
                        pltpu.VMEM((HEADS, tm, 2 * GDN_DV), BF16),
                        head_bf16,
                        pltpu.VMEM((HEADS, nc, 2 * c, DK), BF16),
                        pltpu.VMEM((HEADS, nc, c, GDN_DV), F32),
                        pltpu.VMEM((HEADS, nc // 2, c, 2 * c), BF16),
                        pltpu.VMEM((HEADS, nc // 2, c, 2 * c), F32),
                        pltpu.VMEM((HEADS, nc, SUBLANES, LANES), F32),
                        pltpu.VMEM((HEADS, tm, GDN_DV), F32)],
        compiler_params=_params(2),
        name="gdn_prompt",
    )(x, *consts)


def _stream_projection(xb, win_ref, wb_ref, proj_s):
    wb = win_ref[...].astype(BF16)
    wb_ref[...] = wb
    proj_s[pl.program_id(0)] = _dot(xb, wb)


def _gdn_sample_a_kernel(x_ref, cb_ref, win_ref, wba_ref, cw_ref, alog_ref, dtb_ref,
                         rows_ref, pre_ref, z_ref, wb_ref, proj_s):
    xb = x_ref[...].astype(BF16)
    _stream_projection(xb, win_ref, wb_ref, proj_s)

    @pl.when(pl.program_id(0) == pl.num_programs(0) - 1)
    def _():
        nq = GDN_CONV_CH // W_COLS
        pre = jnp.concatenate([proj_s[i] for i in range(nq)], axis=-1)
        pre_ref[...] = pre
        z_ref[...] = proj_s[nq]
        y = pre * cw_ref[GDN_CONV - 1:GDN_CONV, :]
        for j in range(GDN_CONV - 1):
            y = y + cb_ref[j] * cw_ref[j:j + 1, :]
        qkv = _silu(y)
        beta_all, gl = _gdn_gates(_dot(xb, wba_ref[...]), alog_ref, dtb_ref)
        eg = jnp.exp(gl)
        rows = x_ref.shape[0]
        rows_ref[...] = jnp.zeros(rows_ref.shape, F32)
        for h in range(HEADS):
            hs = slice(h * DK, (h + 1) * DK)
            qn = _l2norm(qkv[:, hs], DK ** -0.5)
            kn = _l2norm(qkv[:, HEADS * DK + h * DK:HEADS * DK + (h + 1) * DK])
            vh = qkv[:, 2 * HEADS * DK + h * GDN_DV:2 * HEADS * DK + (h + 1) * GDN_DV]
            bh = beta_all[:, h:h + 1]
            egh = eg[:, HEADS + h:HEADS + h + 1]
            rows_ref[:, h, 0, :] = kn
            rows_ref[:, h, 1, :] = qn
            rows_ref[:, h, 2, :] = vh * bh
            rows_ref[:, h, 3, :] = jnp.broadcast_to(bh * egh, (rows, DK))
            rows_ref[:, h, 4, :] = jnp.broadcast_to(egh, (rows, DK))
            rows_ref[:, h, 5, :] = jnp.broadcast_to(jnp.sum(qn * kn, -1, keepdims=True), (rows, DK))


def _gdn_sample_b_kernel(p_ref, s_ref, o_ref, sout_ref):
    bb = p_ref.shape[0]
    n = bb * HEADS
    p = p_ref[...].reshape(n, SUBLANES, DK)
    s = s_ref[...].reshape(n, DK, GDN_DV)
    pb = p.astype(BF16)
    r = _bdot('nqd,nde->nqe', pb, s.astype(BF16))
    v_new = p[:, 2:3, :] - p[:, 3:4, :] * r[:, 0:1, :]
    o = p[:, 4:5, :] * r[:, 1:2, :] + p[:, 5:6, :] * v_new
    row0 = lax.broadcasted_iota(jnp.int32, (n, SUBLANES, GDN_DV), 1) == 0
    vn8 = jnp.where(row0, jnp.broadcast_to(v_new, (n, SUBLANES, GDN_DV)), 0.0)
    s_new = s * p[:, 4:5, :] + _bdot('nqd,nqe->nde', pb, vn8.astype(BF16))
    o_ref[...] = jnp.broadcast_to(o, (n, SUBLANES, GDN_DV)).reshape(bb, HEADS, SUBLANES, GDN_DV)
    sout_ref[...] = s_new.reshape(bb, HEADS, DK, GDN_DV)


def _gdn_sample_c_kernel(o_ref, z_ref, x_ref, ng_ref, wo_ref, g_ref, b_ref, out_ref, wob_ref):
    wob = wo_ref[...].astype(BF16)
    wob_ref[...] = wob
    out_ref[...] = _gdn_out([o_ref[:, h, 0, :] for h in range(HEADS)],
                            z_ref[...], x_ref[...], ng_ref, wob, g_ref, b_ref)


def _projection_call(kernel, resident_args, w_in_all, layer, n_steps, n_cols, outs, rows, name):
    return pl.pallas_call(
        kernel,
        grid=(n_steps,),
        in_specs=[_resident(a.shape) for a in resident_args[:2]]
        + [pl.BlockSpec((None, D_MODEL, W_COLS), lambda c: (layer, 0, c))]
        + [_resident(a.shape) for a in resident_args[2:]],
        out_specs=[_whole(o.shape) for o in outs] + [pl.BlockSpec((D_MODEL, W_COLS), lambda c: (0, c))],
        out_shape=list(outs) + [_bf16(D_MODEL, n_cols)],
        scratch_shapes=[pltpu.VMEM((n_steps, rows, W_COLS), F32)],
        compiler_params=_params(1),
        name=name,
    )(*resident_args[:2], w_in_all, *resident_args[2:])


def _state_step(kernel, p, extra, s0, bb, name):
    bsz = s0.shape[0]
    assert bsz % bb == 0

    def blk(a):
        return pl.BlockSpec((bb,) + a.shape[1:], lambda i: (i,) + (0,) * (a.ndim - 1))

    ins = (p,) + tuple(extra) + (s0,)
    o_shape = (bsz, HEADS, SUBLANES, s0.shape[-1])
    return pl.pallas_call(
        kernel,
        grid=(bsz // bb,),
        in_specs=[blk(a) for a in ins],
        out_specs=[pl.BlockSpec((bb,) + o_shape[1:], lambda i: (i, 0, 0, 0)), blk(s0)],
        out_shape=[_f32(*o_shape), _f32(*s0.shape)],
        compiler_params=_params(1),
        name=name,
    )(*ins)


def _gdn_sample(x2d, conv_buf, s0, w_in_all, w_out, layer, w, g, b):
    bsz = x2d.shape[0]
    cb_t = conv_buf.transpose(1, 0, 2)
    rows, pre, z, wqkvz = _projection_call(
        _gdn_sample_a_kernel, (x2d, cb_t, w["wba"], w["conv_w"], w["alog"], w["dtb"]), w_in_all, layer,
        GDN_QKVZ // W_COLS, GDN_QKVZ,
        [_f32(bsz, HEADS, SUBLANES, DK), _f32(bsz, GDN_CONV_CH), _f32(bsz, D_MODEL)], bsz, "gdn_sample_a")
    o8, s_new = _state_step(_gdn_sample_b_kernel, rows, (), s0, 8, "gdn_sample_b")
    x1, wob = _whole_call(_gdn_sample_c_kernel, (o8, z, x2d, w["ng"], w_out, g, b),
                          [_f32(bsz, D_MODEL), _bf16(*w_out.shape)], "gdn_sample_c")
    new_conv = jnp.concatenate([conv_buf[:, 1:], pre[:, None, :]], axis=1)
    return x1, new_conv, s_new, wqkvz, wob


def _log_gamma(h):
    return math.log(1.0 - 2.0 ** (-5.0 - h))


def _rope_tables(pos, freq_ref):
    ang = pos * freq_ref[...]
    lane = lax.broadcasted_iota(jnp.int32, ang.shape, 1)
    return jnp.cos(ang), jnp.where(lane < DK // 2, -1.0, 1.0) * jnp.sin(ang)


def _rope(v, cos2, sin2):
    return v * cos2 + pltpu.roll(v, DK // 2, 1) * sin2


def _ret_out(o_heads, gate, x, wo, g_ref, b_ref):
    gated = []
    for h, oh in enumerate(o_heads):
        mu = jnp.mean(oh, -1, keepdims=True)
        oc = oh - mu
        on = oc * lax.rsqrt(jnp.mean(oc * oc, -1, keepdims=True) + LN_EPS)
        gated.append((_silu(gate[:, h * RET_DV:(h + 1) * RET_DV]) * on).astype(BF16))
    mix = _dot(jnp.concatenate(gated, axis=-1), wo)
    return _layer_norm(DN_ALPHA * x + mix, g_ref[...], b_ref[...])


def _ret_prompt_kernel(x_ref, win_ref, freq_ref, wo_ref, g_ref, b_ref, o_ref, sout_ref,
                       s_ref, q_s, k_s, qx_s, kz_s, v_s, dec_s, o_s, *, tm, c, pos0):
    nc = tm // c
    ti = pl.program_id(1)
    hk = HEADS * DK

    @pl.when(ti == 0)
    def _():
        s_ref[...] = jnp.zeros(s_ref.shape, F32)
        diff = (lax.broadcasted_iota(jnp.int32, (c, c), 0) - lax.broadcasted_iota(jnp.int32, (c, c), 1)).astype(F32)
        for h in range(HEADS):
            dec_s[h] = jnp.where(diff >= 0, jnp.exp(jnp.maximum(diff, 0.0) * _log_gamma(h)), 0.0)

    x = x_ref[...]
    xb = x.astype(BF16)
    row = lax.broadcasted_iota(jnp.int32, (tm, 1), 0)
    cos2, sin2 = _rope_tables((pos0 + ti * tm + row).astype(F32), freq_ref)
    idx = (row % c).astype(F32)
    qall = _dot(xb, win_ref[:, 0:hk])
    kall = _dot(xb, win_ref[:, hk:2 * hk])
    vall = _dot(xb, win_ref[:, 2 * hk:2 * hk + HEADS * RET_DV])
    for h in range(HEADS):
        hs = slice(h * DK, (h + 1) * DK)
        qr = _rope(qall[:, hs], cos2, sin2)
        kr = _rope(kall[:, hs], cos2, sin2) * DK ** -0.5
        q_s[h] = qr.astype(BF16)
        k_s[h] = kr.astype(BF16)
        qx_s[h] = (qr * jnp.exp((idx + 1.0) * _log_gamma(h))).astype(BF16)
        kz_s[h] = (kr * jnp.exp((c - 1.0 - idx) * _log_gamma(h))).astype(BF16)
        v_s[h] = vall[:, h * RET_DV:(h + 1) * RET_DV].astype(BF16)

    for n in range(nc):
        sl = slice(n * c, (n + 1) * c)
        v_c = v_s[:, sl, :]
        attn = _bdot('hid,hjd->hij', q_s[:, sl, :], k_s[:, sl, :]) * dec_s[...]
        s = s_ref[...]
        o_s[:, sl, :] = (_bdot('hij,hje->hie', attn.astype(BF16), v_c)
                         + _bdot('hcd,hde->hce', qx_s[:, sl, :], s.astype(BF16)))
        upd = _bdot('hcd,hce->hde', kz_s[:, sl, :], v_c)
        for h in range(HEADS):
            s_ref[h] = math.exp(c * _log_gamma(h)) * s[h] + upd[h]

    gate = _dot(xb, win_ref[:, 2 * hk + HEADS * RET_DV:])
    o_ref[...] = _ret_out([o_s[h] for h in range(HEADS)], gate, x, wo_ref[...], g_ref, b_ref)

    @pl.when(ti == pl.num_programs(1) - 1)
    def _():
        sout_ref[...] = s_ref[...]


def _ret_prompt(x, w, g, b, tm):
    bsz, t, _ = x.shape
    c = min(RET_CHUNK, tm)
    assert t % tm == 0 and tm % c == 0
    blk = pl.BlockSpec((None, tm, D_MODEL), lambda bi, ti: (bi, ti, 0))
    consts = (w["win"], w["freq"], w["wo"], g, b)
    head_k = pltpu.VMEM((HEADS, tm, DK), BF16)
    return pl.pallas_call(
        functools.partial(_ret_prompt_kernel, tm=tm, c=c, pos0=0),
        grid=(bsz, t // tm),
        in_specs=[blk] + [_resident(a.shape) for a in consts],
        out_specs=[blk, pl.BlockSpec((None, HEADS, DK, RET_DV), lambda bi, ti: (bi, 0, 0, 0))],
        out_shape=[_f32(*x.shape), _f32(bsz, HEADS, DK, RET_DV)],
        scratch_shapes=[pltpu.VMEM((HEADS, DK, RET_DV), F32), head_k, head_k, head_k, head_k,
                        pltpu.VMEM((HEADS, tm, RET_DV), BF16),
                        pltpu.VMEM((HEADS, c, c), F32),
                        pltpu.VMEM((HEADS, tm, RET_DV), F32)],
        compiler_params=_params(2),
        name="ret_prompt",
    )(x, *consts)


def _ret_sample_a_kernel(x_ref, freq_ref, win_ref, kq_ref, vrows_ref, gate_ref, wb_ref, proj_s, *, pos0):
    xb = x_ref[...].astype(BF16)
    _stream_projection(xb, win_ref, wb_ref, proj_s)

    @pl.when(pl.program_id(0) == pl.num_programs(0) - 1)
    def _():
        rows = x_ref.shape[0]
        cos2, sin2 = _rope_tables(jnp.full((1, 1), pos0, F32), freq_ref)
        qall, kall = proj_s[0], proj_s[1]
        gate_ref[...] = jnp.concatenate([proj_s[4], proj_s[5]], axis=-1)
        kq_ref[...] = jnp.zeros(kq_ref.shape, F32)
        vrows_ref[...] = jnp.zeros(vrows_ref.shape, F32)
        for h in range(HEADS):
            hs = slice(h * DK, (h + 1) * DK)
            qr = _rope(qall[:, hs], cos2, sin2)
            kr = _rope(kall[:, hs], cos2, sin2) * DK ** -0.5
            kq_ref[:, h, 0, :] = kr
            kq_ref[:, h, 1, :] = qr
            vh = proj_s[2 + h // 4][:, (h % 4) * RET_DV:(h % 4 + 1) * RET_DV]
            vrows_ref[:, h, 0, :] = vh
            vrows_ref[:, h, 1, :] = jnp.broadcast_to(jnp.sum(qr * kr, -1, keepdims=True), (rows, RET_DV))
            vrows_ref[:, h, 2, :] = jnp.full((rows, RET_DV), math.exp(_log_gamma(h)), F32)


def _ret_sample_b_kernel(kq_ref, v_ref, s_ref, o_ref, sout_ref):
    bb = kq_ref.shape[0]
    n = bb * HEADS
    kqb = kq_ref[...].reshape(n, SUBLANES, DK).astype(BF16)
    v8 = v_ref[...].reshape(n, SUBLANES, RET_DV)
    s = s_ref[...].reshape(n, DK, RET_DV)
    r = _bdot('nqd,nde->nqe', kqb, s.astype(BF16))
    gamma = v8[:, 2:3, :]
    o = v8[:, 1:2, :] * v8[:, 0:1, :] + gamma * r[:, 1:2, :]
    row0 = lax.broadcasted_iota(jnp.int32, (n, SUBLANES, RET_DV), 1) == 0
    s_new = s * gamma + _bdot('nqd,nqe->nde', kqb, jnp.where(row0, v8, 0.0).astype(BF16))
    o_ref[...] = jnp.broadcast_to(o, (n, SUBLANES, RET_DV)).reshape(bb, HEADS, SUBLANES, RET_DV)
    sout_ref[...] = s_new.reshape(bb, HEADS, DK, RET_DV)


def _ret_sample_c_kernel(o_ref, gate_ref, x_ref, wo_ref, g_ref, b_ref, out_ref, wob_ref):
    wob = wo_ref[...].astype(BF16)
    wob_ref[...] = wob
    out_ref[...] = _ret_out([o_ref[:, h, 0, :] for h in range(HEADS)],
                            gate_ref[...], x_ref[...], wob, g_ref, b_ref)


def _ret_sample(x2d, s0, w_in_all, w_out, layer, freq, g, b):
    bsz = x2d.shape[0]
    kq, vrows, gate, winb = _projection_call(
        functools.partial(_ret_sample_a_kernel, pos0=float(PAST_LEN)), (x2d, freq), w_in_all, layer,
        RET_IN // W_COLS, RET_IN,
        [_f32(bsz, HEADS, SUBLANES, DK), _f32(bsz, HEADS, SUBLANES, RET_DV), _f32(bsz, HEADS * RET_DV)],
        bsz, "ret_sample_a")
    o8, s_new = _state_step(_ret_sample_b_kernel, kq, (vrows,), s0, 4, "ret_sample_b")
    x1, wob = _whole_call(_ret_sample_c_kernel, (o8, gate, x2d, w_out, g, b),
                          [_f32(bsz, D_MODEL), _bf16(*w_out.shape)], "ret_sample_c")
    return x1, s_new, winb, wob


def _lane_row(values, offset):
    return jnp.zeros((1, LANES), F32).at[0, offset:offset + HEADS].set(values.astype(F32))


def _gdn_small_weights(w_in, conv_w, a_log, dt_bias, norm_g):
    wba = jnp.zeros((D_MODEL, LANES), F32).at[:, :2 * HEADS].set(w_in[:, GDN_QKVZ:])
    return {"wba": wba.astype(BF16), "conv_w": conv_w,
            "alog": _lane_row(a_log, HEADS), "dtb": _lane_row(dt_bias, HEADS), "ng": norm_g.reshape(1, GDN_DV)}


def _rope_freqs():
    half = DK // 2
    freqs = ROPE_BASE ** (-jnp.arange(half, dtype=F32) / half)
    return jnp.concatenate([freqs, freqs]).reshape(1, DK)


def kernel(x_prompt, x_sample, state_pool, state_gdn_conv, state_gdn, state_ret, pool_w, pool_scale, gdn_w_in,
           gdn_conv_w, gdn_a_log, gdn_dt_bias, gdn_norm_g, gdn_w_out, ret_w_in, ret_w_out, ffn_w13, ffn_w2,
           ln_g, ln_b):
    bsz, seq, _ = x_prompt.shape
    dec = x_sample.shape[0]
    tm = min(512, seq)
    xp = x_prompt
    xs = x_sample.reshape(dec, D_MODEL)
    pool_p, pool_s, conv_p, conv_s, gdn_p, gdn_s, ret_p, ret_s = [], [], [], [], [], [], [], []
    for i in range(DEPTH):
        kind, j = i % N_MIXERS, i // N_MIXERS
        g1, b1 = ln_g[i, 0].reshape(1, D_MODEL), ln_b[i, 0].reshape(1, D_MODEL)
        g2, b2 = ln_g[i, 1].reshape(1, D_MODEL), ln_b[i, 1].reshape(1, D_MODEL)
        if kind == 0:
            sc = pool_scale[j].reshape(1, D_MODEL)
            pool_p.append(xp[:, seq - POOL_BUF:])
            xs, pwb, nbuf = _pool_sample(xs, state_pool.transpose(0, 2, 1, 3), pool_w, j, sc, g1, b1)
            pool_s.append(nbuf)
            xs, *fw = _ffn_cast(xs, ffn_w13, ffn_w2, i, g2, b2)
            xp = _pool_ffn_prompt(xp, pwb, sc, g1, b1, fw, g2, b2, min(2 * tm, seq), tm)
            continue
        if kind == 1:
            w = _gdn_small_weights(gdn_w_in[j], gdn_conv_w[j], gdn_a_log[j], gdn_dt_bias[j], gdn_norm_g[j])
            xs, cs, ss, w["wqkvz"], w["wo"] = _gdn_sample(xs, state_gdn_conv[j], state_gdn[j], gdn_w_in,
                                                         gdn_w_out[j], j, w, g1, b1)
            xp, cp, sp = _gdn_prompt(xp, w, g1, b1, tm)
            conv_p.append(cp), conv_s.append(cs), gdn_p.append(sp), gdn_s.append(ss)
        else:
            w = {"freq": _rope_freqs()}
            xs, ss, w["win"], w["wo"] = _ret_sample(xs, state_ret[j], ret_w_in, ret_w_out[j], j, w["freq"], g1, b1)
            xp, sp = _ret_prompt(xp, w, g1, b1, tm)
            ret_p.append(sp), ret_s.append(ss)
        xs, *fw = _ffn_cast(xs, ffn_w13, ffn_w2, i, g2, b2)
        xp = _ffn(xp.reshape(bsz * seq, D_MODEL), fw, g2, b2, min(2 * tm, seq), tm).reshape(bsz, seq, D_MODEL)
    return (xp, xs.reshape(dec, 1, D_MODEL), jnp.stack(pool_p), jnp.stack(pool_s).transpose(0, 2, 1, 3), jnp.stack(conv_p),
            jnp.stack(conv_s), jnp.stack(gdn_p), jnp.stack(gdn_s), jnp.stack(ret_p), jnp.stack(ret_s))
```

```python
import functools
import math

import jax
import jax.numpy as jnp
from jax import lax
from jax.experimental import pallas as pl
from jax.experimental.pallas import tpu as pltpu

F32 = jnp.float32
BF16 = jnp.bfloat16

D_MODEL = 1024
DEPTH = 4
PAST_LEN = 16384
N_MIXERS = 3

POOL_WINDOWS = (2, 4, 8, 16)
POOL_GROUP = D_MODEL // len(POOL_WINDOWS)
POOL_BUF = max(POOL_WINDOWS) - 1
POOL_HALO = 16
POOL_ROWS = 64

HEADS = 8
DK = D_MODEL // HEADS
GDN_DV = DK
GDN_CONV = 4
GDN_CONV_CH = HEADS * (2 * DK + GDN_DV)
GDN_QKVZ = GDN_CONV_CH + HEADS * GDN_DV
CONV_HALO = 8
CONV_PHASES = 4
GDN_CHUNK = 64
HEAD_GROUP = 8
RET_DV = 2 * DK
RET_IN = 2 * HEADS * DK + 2 * HEADS * RET_DV
RET_CHUNK = 256
ROPE_BASE = 10000.0
W_COLS = 1024

D_FF = -(-8 * D_MODEL // (3 * 256)) * 256
FF_CHUNK = 256
CAST_CHUNK = 256
DN_ALPHA = (2 * DEPTH) ** 0.25
LN_EPS = 1e-5
RMS_EPS = 1e-6

SUBLANES = 8
LANES = 128
VMEM_LIMIT = 56 * 1024 * 1024


def _params(n_grid):
    return pltpu.CompilerParams(dimension_semantics=("arbitrary",) * n_grid, vmem_limit_bytes=VMEM_LIMIT)


def _resident(shape):
    nd = len(shape)
    return pl.BlockSpec(shape, lambda *_: (0,) * nd, pipeline_mode=pl.Buffered(1))


def _whole(shape):
    nd = len(shape)
    return pl.BlockSpec(shape, lambda *_: (0,) * nd)


def _dot(a, b):
    return jnp.dot(a, b, preferred_element_type=F32)


def _bdot(spec, a, b):
    return jnp.einsum(spec, a, b, preferred_element_type=F32)


def _silu(x):
    return x * jax.nn.sigmoid(x)


def _layer_norm(y, g, b):
    mu = jnp.mean(y, -1, keepdims=True)
    yc = y - mu
    var = jnp.mean(yc * yc, -1, keepdims=True)
    return yc * lax.rsqrt(var + LN_EPS) * g + b


def _f32(*shape):
    return jax.ShapeDtypeStruct(shape, F32)


def _bf16(*shape):
    return jax.ShapeDtypeStruct(shape, BF16)


def _ffn_tail(x, w1_ref, w3_ref, w2_ref, g_ref, b_ref):
    xb = x.astype(BF16)
    acc = None
    for c in range(D_FF // FF_CHUNK):
        lo = c * FF_CHUNK
        a = _dot(xb, w1_ref[:, lo:lo + FF_CHUNK])
        b = _dot(xb, w3_ref[:, lo:lo + FF_CHUNK])
        h = (_silu(a) * b).astype(BF16)
        p = _dot(h, w2_ref[lo:lo + FF_CHUNK, :])
        acc = p if acc is None else acc + p
    return _layer_norm(DN_ALPHA * x + acc, g_ref[...], b_ref[...])


def _ffn_kernel(x_ref, w1_ref, w3_ref, w2_ref, g_ref, b_ref, o_ref, *, sub):
    for r0 in range(0, x_ref.shape[0], sub):
        o_ref[r0:r0 + sub, :] = _ffn_tail(x_ref[r0:r0 + sub, :], w1_ref, w3_ref, w2_ref, g_ref, b_ref)


def _ffn(x2d, fw, g, b, tm, sub):
    n = x2d.shape[0]
    assert n % tm == 0 and tm % sub == 0
    row = pl.BlockSpec((tm, D_MODEL), lambda i: (i, 0))
    consts = tuple(fw) + (g, b)
    return pl.pallas_call(
        functools.partial(_ffn_kernel, sub=sub),
        grid=(n // tm,),
        in_specs=[row] + [_resident(a.shape) for a in consts],
        out_specs=row,
        out_shape=_f32(n, D_MODEL),
        compiler_params=_params(1),
        name="ffn",
    )(x2d, *consts)


def _ffn_cast_kernel(x_ref, w1_ref, w3_ref, w2_ref, g_ref, b_ref, o_ref, w1b_ref, w3b_ref, w2b_ref, acc_ref):
    c = pl.program_id(0)

    @pl.when(c == 0)
    def _():
        acc_ref[...] = jnp.zeros(acc_ref.shape, F32)

    xb = x_ref[...].astype(BF16)
    w1b, w3b, w2b = w1_ref[...].astype(BF16), w3_ref[...].astype(BF16), w2_ref[...].astype(BF16)
    w1b_ref[...], w3b_ref[...], w2b_ref[...] = w1b, w3b, w2b
    h = (_silu(_dot(xb, w1b)) * _dot(xb, w3b)).astype(BF16)
    acc_ref[...] += _dot(h, w2b)

    @pl.when(c == pl.num_programs(0) - 1)
    def _():
        o_ref[...] = _layer_norm(DN_ALPHA * x_ref[...] + acc_ref[...], g_ref[...], b_ref[...])


def _ffn_cast(x2d, w13_all, w2_all, layer, g, b):
    n = x2d.shape[0]
    cc = CAST_CHUNK
    nch = D_FF // cc
    return pl.pallas_call(
        _ffn_cast_kernel,
        grid=(nch,),
        in_specs=[_resident(x2d.shape),
                  pl.BlockSpec((None, D_MODEL, cc), lambda c: (layer, 0, c)),
                  pl.BlockSpec((None, D_MODEL, cc), lambda c: (layer, 0, nch + c)),
                  pl.BlockSpec((None, cc, D_MODEL), lambda c: (layer, c, 0)),
                  _resident(g.shape), _resident(b.shape)],
        out_specs=[_whole(x2d.shape),
                   pl.BlockSpec((D_MODEL, cc), lambda c: (0, c)),
                   pl.BlockSpec((D_MODEL, cc), lambda c: (0, c)),
                   pl.BlockSpec((cc, D_MODEL), lambda c: (c, 0))],
        out_shape=[_f32(n, D_MODEL), _bf16(D_MODEL, D_FF), _bf16(D_MODEL, D_FF), _bf16(D_FF, D_MODEL)],
        scratch_shapes=[pltpu.VMEM((n, D_MODEL), F32)],
        compiler_params=_params(1),
        name="ffn_cast",
    )(x2d, w13_all, w13_all, w2_all, g, b)


def _rows_up(v, s):
    rows, w = v.shape
    if s % SUBLANES:
        sub = lax.broadcasted_iota(jnp.int32, v.shape, 0) % SUBLANES
        v = pltpu.roll(v.reshape(rows // SUBLANES, SUBLANES, w), s % SUBLANES, 1).reshape(rows, w)
        v = jnp.where(sub >= s % SUBLANES, v, jnp.concatenate([v[0:SUBLANES], v[0:rows - SUBLANES]], axis=0))
    whole = s - s % SUBLANES
    return jnp.concatenate([v[0:whole], v[0:rows - whole]], axis=0) if whole else v


def _trailing_sums(e, win):
    s, w = e, 1
    while w < win:
        s = s + _rows_up(s, w)
        w *= 2
    return s


def _pool_ffn_prompt_kernel(x_ref, pw_ref, sc_ref, g1_ref, b1_ref, w1_ref, w3_ref, w2_ref, g2_ref, b2_ref,
                            o_ref, ext_ref, pooled_ref, *, tm, sub, n_valid):
    ti = pl.program_id(1)
    rb = POOL_ROWS

    @pl.when(ti == 0)
    def _():
        ext_ref[0:POOL_HALO, :] = jnp.zeros((POOL_HALO, D_MODEL), F32)

    ext_ref[POOL_HALO:POOL_HALO + tm, :] = x_ref[...]

    def pool_rows(r0):
        t = ti * tm + r0 + lax.broadcasted_iota(jnp.int32, (rb, 1), 0)
        for gi, win in enumerate(POOL_WINDOWS):
            lo, hi = gi * POOL_GROUP, (gi + 1) * POOL_GROUP
            e = ext_ref[r0:r0 + rb + POOL_HALO, lo:hi]
            cnt = jnp.minimum(t + 1 + n_valid, win).astype(F32)
            pooled = _trailing_sums(e, win)[POOL_HALO:] / cnt - e[POOL_HALO:]
            pooled_ref[r0:r0 + rb, lo:hi] = pooled.astype(BF16)

    for r0 in range(0, tm, rb):
        pool_rows(r0)
    ext_ref[0:POOL_HALO, :] = ext_ref[tm:tm + POOL_HALO, :]
    for s0 in range(0, tm, sub):
        mix = jnp.concatenate(
            [_dot(pooled_ref[s0:s0 + sub, gi * POOL_GROUP:(gi + 1) * POOL_GROUP], pw_ref[gi])
             for gi in range(len(POOL_WINDOWS))], axis=-1) * sc_ref[...]
        x1 = _layer_norm(DN_ALPHA * x_ref[s0:s0 + sub, :] + mix, g1_ref[...], b1_ref[...])
        o_ref[s0:s0 + sub, :] = _ffn_tail(x1, w1_ref, w3_ref, w2_ref, g2_ref, b2_ref)


def _pool_ffn_prompt(x, pwb, sc, g1, b1, fw, g2, b2, tm, sub):
    bsz, t, _ = x.shape
    assert t % tm == 0 and tm % sub == 0 and sub % POOL_ROWS == 0
    blk = pl.BlockSpec((None, tm, D_MODEL), lambda bi, ti: (bi, ti, 0))
    consts = (pwb, sc, g1, b1) + tuple(fw) + (g2, b2)
    return pl.pallas_call(
        functools.partial(_pool_ffn_prompt_kernel, tm=tm, sub=sub, n_valid=0),
        grid=(bsz, t // tm),
        in_specs=[blk] + [_resident(a.shape) for a in consts],
        out_specs=blk,
        out_shape=_f32(*x.shape),
        scratch_shapes=[pltpu.VMEM((tm + POOL_HALO, D_MODEL), F32), pltpu.VMEM((tm, D_MODEL), BF16)],
        compiler_params=_params(2),
        name="pool_ffn_prompt",
    )(x, *consts)


def _pool_sample_kernel(x_ref, buf_ref, pw_ref, sc_ref, g_ref, b_ref, o_ref, pwb_ref, nbuf_ref, *, n_valid):
    x = x_ref[...]
    pwb = pw_ref[...].astype(BF16)
    pwb_ref[...] = pwb
    outs = []
    for gi, win in enumerate(POOL_WINDOWS):
        lo, hi = gi * POOL_GROUP, (gi + 1) * POOL_GROUP
        xs = x[:, lo:hi]
        s = xs
        for d in range(1, win):
            s = s + buf_ref[POOL_BUF - d, :, lo:hi]
        pooled = s / float(min(1 + n_valid, win)) - xs
        outs.append(_dot(pooled.astype(BF16), pwb[gi]))
    mix = jnp.concatenate(outs, axis=-1) * sc_ref[...]
    o_ref[...] = _layer_norm(DN_ALPHA * x + mix, g_ref[...], b_ref[...])
    nbuf_ref[0:POOL_BUF - 1] = buf_ref[1:POOL_BUF]
    nbuf_ref[POOL_BUF - 1] = x


def _layer_of(stacked, layer):
    nd = stacked.ndim - 1
    return pl.BlockSpec((None,) + stacked.shape[1:], lambda *_: (layer,) + (0,) * nd, pipeline_mode=pl.Buffered(1))


def _whole_call(kernel, args, outs, name, in_specs=None):
    return pl.pallas_call(
        kernel,
        grid=(1,),
        in_specs=in_specs or [_resident(a.shape) for a in args],
        out_specs=[_whole(o.shape) for o in outs],
        out_shape=list(outs),
        compiler_params=_params(1),
        name=name,
    )(*args)


def _pool_sample(x2d, buf_all, pw_all, layer, sc, g, b):
    args = (x2d, buf_all, pw_all, sc, g, b)
    specs = [_layer_of(a, layer) if a is buf_all or a is pw_all else _resident(a.shape) for a in args]
    return _whole_call(functools.partial(_pool_sample_kernel, n_valid=min(PAST_LEN, POOL_BUF)), args,
                       [_f32(*x2d.shape), _bf16(*pw_all.shape[1:]), _f32(*buf_all.shape[1:])], "pool_sample", specs)


def _softplus(x):
    return jnp.maximum(x, 0.0) + jnp.log1p(jnp.exp(-jnp.abs(x)))


def _gdn_gates(ba, alog_ref, dtb_ref):
    return jax.nn.sigmoid(ba), -jnp.exp(alog_ref[...]) * _softplus(ba + dtb_ref[...])


def _l2norm(v, scale=1.0):
    return v * (lax.rsqrt(jnp.sum(v * v, -1, keepdims=True) + RMS_EPS) * scale)


def _gdn_out(o_heads, z, x, ng_ref, wo, g_ref, b_ref):
    gated = []
    for h, oh in enumerate(o_heads):
        on = oh * lax.rsqrt(jnp.mean(oh * oh, -1, keepdims=True) + RMS_EPS) * ng_ref[...]
        gated.append((on * _silu(z[:, h * GDN_DV:(h + 1) * GDN_DV])).astype(BF16))
    mix = _dot(jnp.concatenate(gated, axis=-1), wo)
    return _layer_norm(DN_ALPHA * x + mix, g_ref[...], b_ref[...])


def _pair_index(c):
    ii = lax.broadcasted_iota(jnp.int32, (c, 2 * c), 0)
    lane = lax.broadcasted_iota(jnp.int32, (c, 2 * c), 1)
    return ii, lane % c, lane < c


def _block_diag2(p):
    _, _, left = _pair_index(p.shape[1])
    return jnp.concatenate([jnp.where(left, p, 0.0).astype(BF16), jnp.where(left, 0.0, p).astype(BF16)], axis=1)


def _unit_lower_inverse_pairs(a):
    c = a.shape[1]
    ii, jj, _ = _pair_index(c)
    x = jnp.where(ii == jj, 1.0, 0.0) - jnp.where((ii // 2 == jj // 2) & (ii > jj), a, 0.0)
    s = 4
    while s <= c:
        off = jnp.where((ii // s == jj // s) & (ii // (s // 2) != jj // (s // 2)) & (ii > jj), a, 0.0)
        y = _bdot('bik,bkj->bij', off.astype(BF16), _block_diag2(x))
        x = x - _bdot('bik,bkj->bij', x.astype(BF16), _block_diag2(y))
        s *= 2
    return x


def _gdn_prompt_kernel(x_ref, wqkvz_ref, wba_ref, cw_ref, alog_ref, dtb_ref, ng_ref, wo_ref, g_ref, b_ref,
                       o_ref, conv_ref, sout_ref,
                       ext_s, act_s, s_ref, q_s, k_s, kb_s, vbk_s, kg_s, wkqg_s, u_s, at_s, dec_s, gl_s,
                       o_s, *, tm):
    c = GDN_CHUNK
    nc = tm // c
    ti = pl.program_id(1)
    nslab = GDN_CONV_CH // LANES

    @pl.when(ti == 0)
    def _():
        ext_s[:, 0:CONV_HALO, :] = jnp.zeros((nslab, CONV_HALO, LANES), F32)
        s_ref[...] = jnp.zeros(s_ref.shape, F32)

    x = x_ref[...]
    xb = x.astype(BF16)

    def project(p):
        for part in range(3):
            slab = part * HEADS + 2 * p
            pre = _dot(xb, wqkvz_ref[:, slab * DK:(slab + 2) * DK])
            ext_s[slab, CONV_HALO:CONV_HALO + tm, :] = pre[:, 0:DK]
            ext_s[slab + 1, CONV_HALO:CONV_HALO + tm, :] = pre[:, DK:]

    def conv_silu(part, h):
        slab = part * HEADS + h
        rows = tm // CONV_PHASES
        for ph in range(CONV_PHASES):
            y = None
            for j in range(GDN_CONV):
                start = CONV_HALO + ph - (GDN_CONV - 1 - j)
                term = ext_s[slab, pl.ds(start, rows, stride=CONV_PHASES), :] * cw_ref[j:j + 1, slab * DK:(slab + 1) * DK]
                y = term if y is None else y + term
            act_s[h % 2, part, pl.ds(ph, rows, stride=CONV_PHASES), :] = _silu(y)

    beta_all, gl = _gdn_gates(_dot(xb, wba_ref[...]), alog_ref, dtb_ref)
    pos = lax.broadcasted_iota(jnp.int32, (tm, LANES), 0) % c
    gc = gl
    step = 1
    while step < c:
        gc = gc + jnp.where(pos >= step, pltpu.roll(gc, step, 0), 0.0)
        step *= 2
    gcl = jnp.broadcast_to(gc.reshape(nc, c, LANES)[:, c - 1:c, :], (nc, c, LANES)).reshape(tm, LANES)
    eg = jnp.exp(gc)
    kdec = jnp.exp(gcl - gc)
    glast = jnp.exp(gcl)
    gct = gc.T

    ii, jj, left = _pair_index(c)
    lower = ii >= jj
    for h in range(HEADS):
        if h % 2 == 0:
            project(h // 2)
        for part in range(3):
            conv_silu(part, h)
        for n in range(nc):
            rs = slice(n * c, (n + 1) * c)
            qn = _l2norm(act_s[h % 2, 0, rs, :], DK ** -0.5)
            kn = _l2norm(act_s[h % 2, 1, rs, :])
            vh = act_s[h % 2, 2, rs, :]
            bh = beta_all[rs, h:h + 1]
            egh = eg[rs, HEADS + h:HEADS + h + 1]
            kb = kn * bh
            q_s[h, rs] = qn.astype(BF16)
            k_s[h, rs] = kn.astype(BF16)
            kb_s[h, rs] = kb.astype(BF16)
            vbk_s[h, rs, 0:GDN_DV] = (vh * bh).astype(BF16)
            vbk_s[h, rs, GDN_DV:] = (kb * egh).astype(BF16)
            wkqg_s[h, n, c:, :] = (qn * egh).astype(BF16)
            kg_s[h, rs] = (kn * kdec[rs, HEADS + h:HEADS + h + 1]).astype(BF16)
            gl_s[h, n] = jnp.broadcast_to(glast[n * c:n * c + SUBLANES, HEADS + h:HEADS + h + 1], (SUBLANES, LANES))
        for m in range(nc // 2):
            col = jnp.where(left, gc[2 * m * c:(2 * m + 1) * c, HEADS + h:HEADS + h + 1],
                            gc[(2 * m + 1) * c:(2 * m + 2) * c, HEADS + h:HEADS + h + 1])
            row = gct[HEADS + h:HEADS + h + 1, 2 * m * c:(2 * m + 2) * c]
            dec_s[h, m] = jnp.where(lower, jnp.exp(jnp.where(lower, col - row, 0.0)), 0.0)

    for slab in range(nslab):
        conv_ref[:, slab * DK:(slab + 1) * DK] = ext_s[slab, tm + CONV_HALO - (GDN_CONV - 1):tm + CONV_HALO, :]
    ext_s[:, 0:CONV_HALO, :] = ext_s[:, tm:tm + CONV_HALO, :]

    nb = HEAD_GROUP * nc // 2

    def diag_pair(g):
        return jnp.where(left, g[:, 0:c, :], g[:, c:, :])

    def precompute(gi, carry):
        hs = pl.ds(gi * HEAD_GROUP, HEAD_GROUP)
        dec = dec_s[hs].reshape(nb, c, 2 * c)
        kn2 = k_s[hs].reshape(nb, 2 * c, DK)
        a = diag_pair(_bdot('bid,bjd->bij', kb_s[hs].reshape(nb, 2 * c, DK), kn2))
        tinv = _unit_lower_inverse_pairs(jnp.where(ii > jj, a * dec, 0.0))
        uw = _bdot('bik,bkd->bid', _block_diag2(tinv), vbk_s[hs].reshape(nb, 2 * c, 2 * GDN_DV))
        u_s[hs] = uw[:, :, 0:GDN_DV].reshape(HEAD_GROUP, nc, c, GDN_DV)
        wkqg_s[hs, :, 0:c, :] = uw[:, :, GDN_DV:].astype(BF16).reshape(HEAD_GROUP, nc, c, DK)
        attn = diag_pair(_bdot('bid,bjd->bij', q_s[hs].reshape(nb, 2 * c, DK), kn2)) * dec
        at_s[hs] = attn.astype(BF16).reshape(HEAD_GROUP, nc // 2, c, 2 * c)
        return carry

    lax.fori_loop(0, HEADS // HEAD_GROUP, precompute, 0)

    def chunk_pair(m, carry):
        at2 = at_s[:, m]
        for half in range(2):
            n = 2 * m + half
            sl = pl.ds(pl.multiple_of(n * c, c), c)
            s = s_ref[...]
            r = _bdot('hid,hde->hie', wkqg_s[:, n], s.astype(BF16))
            vnb = (u_s[:, n] - r[:, 0:c, :]).astype(BF16)
            o_s[:, sl, :] = r[:, c:, :] + _bdot('hij,hje->hie', at2[:, :, half * c:(half + 1) * c], vnb)
            s_ref[...] = s * gl_s[:, n][:, 0:1, :] + _bdot('hcd,hce->hde', kg_s[:, sl, :], vnb)
        return carry

    for m in range(nc // 2):
        chunk_pair(m, 0)

    z = _dot(xb, wqkvz_ref[:, GDN_CONV_CH:])
    o_ref[...] = _gdn_out([o_s[h] for h in range(HEADS)], z, x, ng_ref, wo_ref[...], g_ref, b_ref)

    @pl.when(ti == pl.num_programs(1) - 1)
    def _():
        sout_ref[...] = s_ref[...]


def _gdn_prompt(x, w, g, b, tm):
    bsz, t, _ = x.shape
    assert t % tm == 0 and tm % (2 * GDN_CHUNK) == 0
    c = GDN_CHUNK
    nc = tm // c
    blk = pl.BlockSpec((None, tm, D_MODEL), lambda bi, ti: (bi, ti, 0))
    consts = (w["wqkvz"], w["wba"], w["conv_w"], w["alog"], w["dtb"], w["ng"], w["wo"], g, b)
    head_bf16 = pltpu.VMEM((HEADS, tm, DK), BF16)
    return pl.pallas_call(
        functools.partial(_gdn_prompt_kernel, tm=tm),
        grid=(bsz, t // tm),
        in_specs=[blk] + [_resident(a.shape) for a in consts],
        out_specs=[blk,
                   pl.BlockSpec((None, GDN_CONV - 1, GDN_CONV_CH), lambda bi, ti: (bi, 0, 0)),
                   pl.BlockSpec((None, HEADS, DK, GDN_DV), lambda bi, ti: (bi, 0, 0, 0))],
        out_shape=[_f32(*x.shape), _f32(bsz, GDN_CONV - 1, GDN_CONV_CH), _f32(bsz, HEADS, DK, GDN_DV)],
        scratch_shapes=[pltpu.VMEM((GDN_CONV_CH // LANES, tm + CONV_HALO, LANES), F32),
                        pltpu.VMEM((2, 3, tm, LANES), F32),
                        pltpu.VMEM((HEADS, DK, GDN_DV), F32),
                        head_bf16, head_bf16, head_bf16,
                        pltpu.VMEM((HEADS, tm, 2 * GDN_DV), BF16),
                        head_bf16,
                        pltpu.VMEM((HEADS, nc, 2 * c, DK), BF16),
                        pltpu.VMEM((HEADS, nc, c, GDN_DV), F32),
                        pltpu.VMEM((HEADS, nc // 2, c, 2 * c), BF16),
                        pltpu.VMEM((HEADS, nc // 2, c, 2 * c), F32),
                        pltpu.VMEM((HEADS, nc, SUBLANES, LANES), F32),
                        pltpu.VMEM((HEADS, tm, GDN_DV), F32)],
        compiler_params=_params(2),
        name="gdn_prompt",
    )(x, *consts)


def _stream_projection(xb, win_ref, wb_ref, proj_s):
    wb = win_ref[...].astype(BF16)
    wb_ref[...] = wb
    proj_s[pl.program_id(0)] = _dot(xb, wb)


def _gdn_sample_a_kernel(x_ref, cb_ref, win_ref, wba_ref, cw_ref, alog_ref, dtb_ref,
                         rows_ref, pre_ref, z_ref, wb_ref, proj_s):
    xb = x_ref[...].astype(BF16)
    _stream_projection(xb, win_ref, wb_ref, proj_s)

    @pl.when(pl.program_id(0) == pl.num_programs(0) - 1)
    def _():
        nq = GDN_CONV_CH // W_COLS
        pre = jnp.concatenate([proj_s[i] for i in range(nq)], axis=-1)
        pre_ref[...] = pre
        z_ref[...] = proj_s[nq]
        y = pre * cw_ref[GDN_CONV - 1:GDN_CONV, :]
        for j in range(GDN_CONV - 1):
            y = y + cb_ref[j] * cw_ref[j:j + 1, :]
        qkv = _silu(y)
        beta_all, gl = _gdn_gates(_dot(xb, wba_ref[...]), alog_ref, dtb_ref)
        eg = jnp.exp(gl)
        rows = x_ref.shape[0]
        rows_ref[...] = jnp.zeros(rows_ref.shape, F32)
        for h in range(HEADS):
            hs = slice(h * DK, (h + 1) * DK)
            qn = _l2norm(qkv[:, hs], DK ** -0.5)
            kn = _l2norm(qkv[:, HEADS * DK + h * DK:HEADS * DK + (h + 1) * DK])
            vh = qkv[:, 2 * HEADS * DK + h * GDN_DV:2 * HEADS * DK + (h + 1) * GDN_DV]
            bh = beta_all[:, h:h + 1]
            egh = eg[:, HEADS + h:HEADS + h + 1]
            rows_ref[:, h, 0, :] = kn
            rows_ref[:, h, 1, :] = qn
            rows_ref[:, h, 2, :] = vh * bh
            rows_ref[:, h, 3, :] = jnp.broadcast_to(bh * egh, (rows, DK))
            rows_ref[:, h, 4, :] = jnp.broadcast_to(egh, (rows, DK))
            rows_ref[:, h, 5, :] = jnp.broadcast_to(jnp.sum(qn * kn, -1, keepdims=True), (rows, DK))


def _gdn_sample_b_kernel(p_ref, s_ref, o_ref, sout_ref):
    bb = p_ref.shape[0]
    n = bb * HEADS
    p = p_ref[...].reshape(n, SUBLANES, DK)
    s = s_ref[...].reshape(n, DK, GDN_DV)
    pb = p.astype(BF16)
    r = _bdot('nqd,nde->nqe', pb, s.astype(BF16))
    v_new = p[:, 2:3, :] - p[:, 3:4, :] * r[:, 0:1, :]
    o = p[:, 4:5, :] * r[:, 1:2, :] + p[:, 5:6, :] * v_new
    row0 = lax.broadcasted_iota(jnp.int32, (n, SUBLANES, GDN_DV), 1) == 0
    vn8 = jnp.where(row0, jnp.broadcast_to(v_new, (n, SUBLANES, GDN_DV)), 0.0)
    s_new = s * p[:, 4:5, :] + _bdot('nqd,nqe->nde', pb, vn8.astype(BF16))
    o_ref[...] = jnp.broadcast_to(o, (n, SUBLANES, GDN_DV)).reshape(bb, HEADS, SUBLANES, GDN_DV)
    sout_ref[...] = s_new.reshape(bb, HEADS, DK, GDN_DV)


def _gdn_sample_c_kernel(o_ref, z_ref, x_ref, ng_ref, wo_ref, g_ref, b_ref, out_ref, wob_ref):
    wob = wo_ref[...].astype(BF16)
    wob_ref[...] = wob
    out_ref[...] = _gdn_out([o_ref[:, h, 0, :] for h in range(HEADS)],
                            z_ref[...], x_ref[...], ng_ref, wob, g_ref, b_ref)


def _projection_call(kernel, resident_args, w_in_all, layer, n_steps, n_cols, outs, rows, name):
    return pl.pallas_call(
        kernel,
        grid=(n_steps,),
        in_specs=[_resident(a.shape) for a in resident_args[:2]]
        + [pl.BlockSpec((None, D_MODEL, W_COLS), lambda c: (layer, 0, c))]
        + [_resident(a.shape) for a in resident_args[2:]],
        out_specs=[_whole(o.shape) for o in outs] + [pl.BlockSpec((D_MODEL, W_COLS), lambda c: (0, c))],
        out_shape=list(outs) + [_bf16(D_MODEL, n_cols)],
        scratch_shapes=[pltpu.VMEM((n_steps, rows, W_COLS), F32)],
        compiler_params=_params(1),
        name=name,
    )(*resident_args[:2], w_in_all, *resident_args[2:])


def _state_step(kernel, p, extra, s0, bb, name):
    bsz = s0.shape[0]
    assert bsz % bb == 0

    def blk(a):
        return pl.BlockSpec((bb,) + a.shape[1:], lambda i: (i,) + (0,) * (a.ndim - 1))

    ins = (p,) + tuple(extra) + (s0,)
    o_shape = (bsz, HEADS, SUBLANES, s0.shape[-1])
    return pl.pallas_call(
        kernel,
        grid=(bsz // bb,),
        in_specs=[blk(a) for a in ins],
        out_specs=[pl.BlockSpec((bb,) + o_shape[1:], lambda i: (i, 0, 0, 0)), blk(s0)],
        out_shape=[_f32(*o_shape), _f32(*s0.shape)],
        compiler_params=_params(1),
        name=name,
    )(*ins)


def _gdn_sample(x2d, conv_buf, s0, w_in_all, w_out, layer, w, g, b):
    bsz = x2d.shape[0]
    cb_t = conv_buf.transpose(1, 0, 2)
    rows, pre, z, wqkvz = _projection_call(
        _gdn_sample_a_kernel, (x2d, cb_t, w["wba"], w["conv_w"], w["alog"], w["dtb"]), w_in_all, layer,
        GDN_QKVZ // W_COLS, GDN_QKVZ,
        [_f32(bsz, HEADS, SUBLANES, DK), _f32(bsz, GDN_CONV_CH), _f32(bsz, D_MODEL)], bsz, "gdn_sample_a")
    o8, s_new = _state_step(_gdn_sample_b_kernel, rows, (), s0, 8, "gdn_sample_b")
    x1, wob = _whole_call(_gdn_sample_c_kernel, (o8, z, x2d, w["ng"], w_out, g, b),
                          [_f32(bsz, D_MODEL), _bf16(*w_out.shape)], "gdn_sample_c")
    new_conv = jnp.concatenate([conv_buf[:, 1:], pre[:, None, :]], axis=1)
    return x1, new_conv, s_new, wqkvz, wob


def _log_gamma(h):
    return math.log(1.0 - 2.0 ** (-5.0 - h))


def _rope_tables(pos, freq_ref):
    ang = pos * freq_ref[...]
    lane = lax.broadcasted_iota(jnp.int32, ang.shape, 1)
    return jnp.cos(ang), jnp.where(lane < DK // 2, -1.0, 1.0) * jnp.sin(ang)


def _rope(v, cos2, sin2):
    return v * cos2 + pltpu.roll(v, DK // 2, 1) * sin2


def _ret_out(o_heads, gate, x, wo, g_ref, b_ref):
    gated = []
    for h, oh in enumerate(o_heads):
        mu = jnp.mean(oh, -1, keepdims=True)
        oc = oh - mu
        on = oc * lax.rsqrt(jnp.mean(oc * oc, -1, keepdims=True) + LN_EPS)
        gated.append((_silu(gate[:, h * RET_DV:(h + 1) * RET_DV]) * on).astype(BF16))
    mix = _dot(jnp.concatenate(gated, axis=-1), wo)
    return _layer_norm(DN_ALPHA * x + mix, g_ref[...], b_ref[...])


def _ret_prompt_kernel(x_ref, win_ref, freq_ref, wo_ref, g_ref, b_ref, o_ref, sout_ref,
                       s_ref, q_s, k_s, qx_s, kz_s, v_s, dec_s, o_s, *, tm, c, pos0):
    nc = tm // c
    ti = pl.program_id(1)
    hk = HEADS * DK

    @pl.when(ti == 0)
    def _():
        s_ref[...] = jnp.zeros(s_ref.shape, F32)
        diff = (lax.broadcasted_iota(jnp.int32, (c, c), 0) - lax.broadcasted_iota(jnp.int32, (c, c), 1)).astype(F32)
        for h in range(HEADS):
            dec_s[h] = jnp.where(diff >= 0, jnp.exp(jnp.maximum(diff, 0.0) * _log_gamma(h)), 0.0)

    x = x_ref[...]
    xb = x.astype(BF16)
    row = lax.broadcasted_iota(jnp.int32, (tm, 1), 0)
    cos2, sin2 = _rope_tables((pos0 + ti * tm + row).astype(F32), freq_ref)
    idx = (row % c).astype(F32)
    qall = _dot(xb, win_ref[:, 0:hk])
    kall = _dot(xb, win_ref[:, hk:2 * hk])
    vall = _dot(xb, win_ref[:, 2 * hk:2 * hk + HEADS * RET_DV])
    for h in range(HEADS):
        hs = slice(h * DK, (h + 1) * DK)
        qr = _rope(qall[:, hs], cos2, sin2)
        kr = _rope(kall[:, hs], cos2, sin2) * DK ** -0.5
        q_s[h] = qr.astype(BF16)
        k_s[h] = kr.astype(BF16)
        qx_s[h] = (qr * jnp.exp((idx + 1.0) * _log_gamma(h))).astype(BF16)
        kz_s[h] = (kr * jnp.exp((c - 1.0 - idx) * _log_gamma(h))).astype(BF16)
        v_s[h] = vall[:, h * RET_DV:(h + 1) * RET_DV].astype(BF16)

    for n in range(nc):
        sl = slice(n * c, (n + 1) * c)
        v_c = v_s[:, sl, :]
        attn = _bdot('hid,hjd->hij', q_s[:, sl, :], k_s[:, sl, :]) * dec_s[...]
        s = s_ref[...]
        o_s[:, sl, :] = (_bdot('hij,hje->hie', attn.astype(BF16), v_c)
                         + _bdot('hcd,hde->hce', qx_s[:, sl, :], s.astype(BF16)))
        upd = _bdot('hcd,hce->hde', kz_s[:, sl, :], v_c)
        for h in range(HEADS):
            s_ref[h] = math.exp(c * _log_gamma(h)) * s[h] + upd[h]

    gate = _dot(xb, win_ref[:, 2 * hk + HEADS * RET_DV:])
    o_ref[...] = _ret_out([o_s[h] for h in range(HEADS)], gate, x, wo_ref[...], g_ref, b_ref)

    @pl.when(ti == pl.num_programs(1) - 1)
    def _():
        sout_ref[...] = s_ref[...]


def _ret_prompt(x, w, g, b, tm):
    bsz, t, _ = x.shape
    c = min(RET_CHUNK, tm)
    assert t % tm == 0 and tm % c == 0
    blk = pl.BlockSpec((None, tm, D_MODEL), lambda bi, ti: (bi, ti, 0))
    consts = (w["win"], w["freq"], w["wo"], g, b)
    head_k = pltpu.VMEM((HEADS, tm, DK), BF16)
    return pl.pallas_call(
        functools.partial(_ret_prompt_kernel, tm=tm, c=c, pos0=0),
        grid=(bsz, t // tm),
        in_specs=[blk] + [_resident(a.shape) for a in consts],
        out_specs=[blk, pl.BlockSpec((None, HEADS, DK, RET_DV), lambda bi, ti: (bi, 0, 0, 0))],
        out_shape=[_f32(*x.shape), _f32(bsz, HEADS, DK, RET_DV)],
        scratch_shapes=[pltpu.VMEM((HEADS, DK, RET_DV), F32), head_k, head_k, head_k, head_k,
                        pltpu.VMEM((HEADS, tm, RET_DV), BF16),
                        pltpu.VMEM((HEADS, c, c), F32),
                        pltpu.VMEM((HEADS, tm, RET_DV), F32)],
        compiler_params=_params(2),
        name="ret_prompt",
    )(x, *consts)


def _ret_sample_a_kernel(x_ref, freq_ref, win_ref, kq_ref, vrows_ref, gate_ref, wb_ref, proj_s, *, pos0):
    xb = x_ref[...].astype(BF16)
    _stream_projection(xb, win_ref, wb_ref, proj_s)

    @pl.when(pl.program_id(0) == pl.num_programs(0) - 1)
    def _():
        rows = x_ref.shape[0]
        cos2, sin2 = _rope_tables(jnp.full((1, 1), pos0, F32), freq_ref)
        qall, kall = proj_s[0], proj_s[1]
        gate_ref[...] = jnp.concatenate([proj_s[4], proj_s[5]], axis=-1)
        kq_ref[...] = jnp.zeros(kq_ref.shape, F32)
        vrows_ref[...] = jnp.zeros(vrows_ref.shape, F32)
        for h in range(HEADS):
            hs = slice(h * DK, (h + 1) * DK)
            qr = _rope(qall[:, hs], cos2, sin2)
            kr = _rope(kall[:, hs], cos2, sin2) * DK ** -0.5
            kq_ref[:, h, 0, :] = kr
            kq_ref[:, h, 1, :] = qr
            vh = proj_s[2 + h // 4][:, (h % 4) * RET_DV:(h % 4 + 1) * RET_DV]
            vrows_ref[:, h, 0, :] = vh
            vrows_ref[:, h, 1, :] = jnp.broadcast_to(jnp.sum(qr * kr, -1, keepdims=True), (rows, RET_DV))
            vrows_ref[:, h, 2, :] = jnp.full((rows, RET_DV), math.exp(_log_gamma(h)), F32)


def _ret_sample_b_kernel(kq_ref, v_ref, s_ref, o_ref, sout_ref):
    bb = kq_ref.shape[0]
    n = bb * HEADS
    kqb = kq_ref[...].reshape(n, SUBLANES, DK).astype(BF16)
    v8 = v_ref[...].reshape(n, SUBLANES, RET_DV)
    s = s_ref[...].reshape(n, DK, RET_DV)
    r = _bdot('nqd,nde->nqe', kqb, s.astype(BF16))
    gamma = v8[:, 2:3, :]
    o = v8[:, 1:2, :] * v8[:, 0:1, :] + gamma * r[:, 1:2, :]
    row0 = lax.broadcasted_iota(jnp.int32, (n, SUBLANES, RET_DV), 1) == 0
    s_new = s * gamma + _bdot('nqd,nqe->nde', kqb, jnp.where(row0, v8, 0.0).astype(BF16))
    o_ref[...] = jnp.broadcast_to(o, (n, SUBLANES, RET_DV)).reshape(bb, HEADS, SUBLANES, RET_DV)
    sout_ref[...] = s_new.reshape(bb, HEADS, DK, RET_DV)


def _ret_sample_c_kernel(o_ref, gate_ref, x_ref, wo_ref, g_ref, b_ref, out_ref, wob_ref):
    wob = wo_ref[...].astype(BF16)
    wob_ref[...] = wob
    out_ref[...] = _ret_out([o_ref[:, h, 0, :] for h in range(HEADS)],
                            gate_ref[...], x_ref[...], wob, g_ref, b_ref)


def _ret_sample(x2d, s0, w_in_all, w_out, layer, freq, g, b):
    bsz = x2d.shape[0]
    kq, vrows, gate, winb = _projection_call(
        functools.partial(_ret_sample_a_kernel, pos0=float(PAST_LEN)), (x2d, freq), w_in_all, layer,
        RET_IN // W_COLS, RET_IN,
        [_f32(bsz, HEADS, SUBLANES, DK), _f32(bsz, HEADS, SUBLANES, RET_DV), _f32(bsz, HEADS * RET_DV)],
        bsz, "ret_sample_a")
    o8, s_new = _state_step(_ret_sample_b_kernel, kq, (vrows,), s0, 4, "ret_sample_b")
    x1, wob = _whole_call(_ret_sample_c_kernel, (o8, gate, x2d, w_out, g, b),
                          [_f32(bsz, D_MODEL), _bf16(*w_out.shape)], "ret_sample_c")
    return x1, s_new, winb, wob


def _lane_row(values, offset):
    return jnp.zeros((1, LANES), F32).at[0, offset:offset + HEADS].set(values.astype(F32))


def _gdn_small_weights(w_in, conv_w, a_log, dt_bias, norm_g):
    wba = jnp.zeros((D_MODEL, LANES), F32).at[:, :2 * HEADS].set(w_in[:, GDN_QKVZ:])
    return {"wba": wba.astype(BF16), "conv_w": conv_w,
            "alog": _lane_row(a_log, HEADS), "dtb": _lane_row(dt_bias, HEADS), "ng": norm_g.reshape(1, GDN_DV)}


def _rope_freqs():
    half = DK // 2
    freqs = ROPE_BASE ** (-jnp.arange(half, dtype=F32) / half)
    return jnp.concatenate([freqs, freqs]).reshape(1, DK)


def kernel(x_prompt, x_sample, state_pool, state_gdn_conv, state_gdn, state_ret, pool_w, pool_scale, gdn_w_in,
           gdn_conv_w, gdn_a_log, gdn_dt_bias, gdn_norm_g, gdn_w_out, ret_w_in, ret_w_out, ffn_w13, ffn_w2,
           ln_g, ln_b):
    bsz, seq, _ = x_prompt.shape
    dec = x_sample.shape[0]
    tm = min(512, seq)
    xp = x_prompt
    xs = x_sample.reshape(dec, D_MODEL)
    pool_p, pool_s, conv_p, conv_s, gdn_p, gdn_s, ret_p, ret_s = [], [], [], [], [], [], [], []
    for i in range(DEPTH):
        kind, j = i % N_MIXERS, i // N_MIXERS
        g1, b1 = ln_g[i, 0].reshape(1, D_MODEL), ln_b[i, 0].reshape(1, D_MODEL)
        g2, b2 = ln_g[i, 1].reshape(1, D_MODEL), ln_b[i, 1].reshape(1, D_MODEL)
        if kind == 0:
            sc = pool_scale[j].reshape(1, D_MODEL)
            pool_p.append(xp[:, seq - POOL_BUF:])
            xs, pwb, nbuf = _pool_sample(xs, state_pool.transpose(0, 2, 1, 3), pool_w, j, sc, g1, b1)
            pool_s.append(nbuf)
            xs, *fw = _ffn_cast(xs, ffn_w13, ffn_w2, i, g2, b2)
            xp = _pool_ffn_prompt(xp, pwb, sc, g1, b1, fw, g2, b2, min(2 * tm, seq), tm)
            continue
        if kind == 1:
            w = _gdn_small_weights(gdn_w_in[j], gdn_conv_w[j], gdn_a_log[j], gdn_dt_bias[j], gdn_norm_g[j])
            xs, cs, ss, w["wqkvz"], w["wo"] = _gdn_sample(xs, state_gdn_conv[j], state_gdn[j], gdn_w_in,
                                                         gdn_w_out[j], j, w, g1, b1)
            xp, cp, sp = _gdn_prompt(xp, w, g1, b1, tm)
            conv_p.append(cp), conv_s.append(cs), gdn_p.append(sp), gdn_s.append(ss)
        else:
            w = {"freq": _rope_freqs()}
            xs, ss, w["win"], w["wo"] = _ret_sample(xs, state_ret[j], ret_w_in, ret_w_out[j], j, w["freq"], g1, b1)
            xp, sp = _ret_prompt(xp, w, g1, b1, tm)
            ret_p.append(sp), ret_s.append(ss)
        xs, *fw = _ffn_cast(xs, ffn_w13, ffn_w2, i, g2, b2)
        xp = _ffn(xp.reshape(bsz * seq, D_MODEL), fw, g2, b2, min(2 * tm, seq), tm).reshape(bsz, seq, D_MODEL)
    return (xp, xs.reshape(dec, 1, D_MODEL), jnp.stack(pool_p), jnp.stack(pool_s).transpose(0, 2, 1, 3), jnp.stack(conv_p),
            jnp.stack(conv_s), jnp.stack(gdn_p), jnp.stack(gdn_s), jnp.stack(ret_p), jnp.stack(ret_s))
```
